```python
import math
import jax
import jax.numpy as jnp
from jax import lax
import numpy as np

D_MODEL = 1024
BATCH = 32
SEQ = 2048
DEPTH = 4

N_MIXERS = 4
GRID_W = 64
CONV_K = 5
NORM_EPS = 1e-6

GDN_HEADS = 8
GDN_DK = 128
GDN_DV = 128
GDN_CHUNK = 64
GDN_QK = GDN_HEADS * GDN_DK
GDN_V = GDN_HEADS * GDN_DV
GDN_CONV_CH = 2 * GDN_QK + GDN_V
GDN_IN = GDN_CONV_CH + GDN_V + 4 * GDN_HEADS

SSD_INNER = 2 * D_MODEL
SSD_HEADDIM = 64
SSD_HEADS = SSD_INNER // SSD_HEADDIM
SSD_GROUPS = 4
SSD_STATE = 128
SSD_CHUNK = 64
SSD_BC = SSD_GROUPS * SSD_STATE
SSD_CONV_CH = SSD_INNER + 2 * SSD_BC
SSD_IN = SSD_INNER + SSD_CONV_CH + 2 * SSD_HEADS

SWA_HEADS = 16
SWA_KV_HEADS = 4
SWA_HEADDIM = 64
SWA_WINDOW = 128
SWA_IN = (SWA_HEADS + 2 * SWA_KV_HEADS) * SWA_HEADDIM
SWA_OUT = SWA_HEADS * SWA_HEADDIM
ROPE_THETA = 10000.0

NA_HEADS = 16
NA_HEADDIM = 64
NA_ROWS = 8
NA_COLS = 16
NA_WIDTH = NA_HEADS * NA_HEADDIM
NA_IN = 3 * NA_WIDTH

PEER_HEADS = 8
PEER_NKEYS = 128
PEER_EXPERTS = PEER_NKEYS * PEER_NKEYS
PEER_QDIM = 256
PEER_HALF = PEER_QDIM // 2
PEER_TOPK = 16
PEER_BLOCK = 128

kernel_name = "hybrid_gdn_ssd_swa_natten_peer_encoder"


def _rmsnorm(x, w):
    xf = x.astype(jnp.float32)
    y = xf * lax.rsqrt(jnp.mean(xf * xf, axis=-1, keepdims=True) + NORM_EPS)
    return (y * w.astype(jnp.float32)).astype(x.dtype)


def _l2norm(t):
    t = t.astype(jnp.float32)
    return t * lax.rsqrt(jnp.sum(t * t, axis=-1, keepdims=True) + NORM_EPS)


def _dwconv(x, w):
    ch = w.shape[1]
    return lax.conv_general_dilated(
        x, w[:, None, :].astype(x.dtype), window_strides=(1,),
        padding=[(CONV_K // 2, CONV_K // 2)],
        dimension_numbers=('NWC', 'WIO', 'NWC'), feature_group_count=ch)


def _flip(t):
    return jnp.flip(t, axis=1)


def _gated_delta_rule(q, k, v, g, beta):
    Bsz, T, H, DK = q.shape
    DV = v.shape[-1]
    C = GDN_CHUNK
    NC = T // C

    def chunks(a):
        a = a.astype(jnp.float32).reshape(Bsz, NC, C, H, *a.shape[3:])
        return jnp.moveaxis(a, (1, 3), (0, 2))

    q = chunks(q) * DK ** -0.5
    k, v, g, beta = chunks(k), chunks(v), chunks(g), chunks(beta)
    gc = jnp.cumsum(g, axis=-1)
    incl = jnp.tril(jnp.ones((C, C), bool))
    decay = jnp.exp(jnp.where(incl, gc[..., :, None] - gc[..., None, :], -jnp.inf))
    kb = k * beta[..., None]
    lower = jnp.tril(jnp.einsum('nbhid,nbhjd->nbhij', kb, k) * decay, -1)
    eye = jnp.eye(C, dtype=jnp.float32)
    rhs = jnp.concatenate([v * beta[..., None], kb * jnp.exp(gc)[..., None]], axis=-1)
    sol = lax.linalg.triangular_solve(jnp.broadcast_to(eye + lower, lower.shape), rhs,
                                      left_side=True, lower=True, unit_diagonal=True)
    u, w = sol[..., :DV], sol[..., DV:]
    qk = jnp.einsum('nbhid,nbhjd->nbhij', q, k) * decay

    def step(S, inp):
        q_c, k_c, u_c, w_c, qk_c, g_c = inp
        v_new = u_c - jnp.einsum('bhck,bhkv->bhcv', w_c, S)
        o = (jnp.einsum('bhck,bhkv->bhcv', q_c * jnp.exp(g_c)[..., None], S)
             + jnp.einsum('bhij,bhjv->bhiv', qk_c, v_new))
        g_last = g_c[..., -1:]
        S = (S * jnp.exp(g_last)[..., None]
             + jnp.einsum('bhck,bhcv->bhkv', k_c * jnp.exp(g_last - g_c)[..., None], v_new))
        return S, o

    S0 = jnp.zeros((Bsz, H, DK, DV), jnp.float32)
    _, o = lax.scan(step, S0, (q, k, u, w, qk, gc))
    return jnp.moveaxis(o, (0, 2), (1, 3)).reshape(Bsz, T, H, DV)


def _mixer_gdn(h, w_in, conv_w, a_log, dt_bias, norm_w, w_out):
    Bsz, T, _ = h.shape
    H = GDN_HEADS
    proj = h @ w_in
    qkv = jax.nn.silu(_dwconv(proj[..., :GDN_CONV_CH], conv_w))
    z = proj[..., GDN_CONV_CH:GDN_CONV_CH + GDN_V]
    gates = proj[..., GDN_CONV_CH + GDN_V:].astype(jnp.float32).reshape(Bsz, T, 2, 2, H)
    q = _l2norm(qkv[..., :GDN_QK].reshape(Bsz, T, H, GDN_DK))
    k = _l2norm(qkv[..., GDN_QK:2 * GDN_QK].reshape(Bsz, T, H, GDN_DK))
    v = qkv[..., 2 * GDN_QK:].reshape(Bsz, T, H, GDN_DV)
    g = -jnp.exp(a_log.astype(jnp.float32)) * jax.nn.softplus(gates[:, :, 0] + dt_bias)
    beta = jax.nn.sigmoid(gates[:, :, 1])
    o = (_gated_delta_rule(q, k, v, g[:, :, 0], beta[:, :, 0])
         + _flip(_gated_delta_rule(_flip(q), _flip(k), _flip(v), _flip(g[:, :, 1]), _flip(beta[:, :, 1]))))
    o = _rmsnorm(o, norm_w) * jax.nn.silu(z.astype(jnp.float32)).reshape(Bsz, T, H, GDN_DV)
    return o.reshape(Bsz, T, GDN_V).astype(h.dtype) @ w_out


def _ssd_scan(xs, dt, a, bm, cm):
    Bsz, T, G, R, P = xs.shape
    L = SSD_CHUNK
    NC = T // L
    da = (dt * a).reshape(Bsz, NC, L, G, R)
    X = (xs.astype(jnp.float32) * dt[..., None]).reshape(Bsz, NC, L, G, R, P)
    Bc = bm.astype(jnp.float32).reshape(Bsz, NC, L, G, -1)
    Cc = cm.astype(jnp.float32).reshape(Bsz, NC, L, G, -1)
    acum = jnp.cumsum(da, axis=2)
    incl = jnp.tril(jnp.ones((L, L), bool))[:, :, None, None]
    seg = jnp.exp(jnp.where(incl, acum[:, :, :, None] - acum[:, :, None, :], -jnp.inf))
    cb = jnp.einsum('bclgn,bcsgn->bclsg', Cc, Bc)
    y_diag = jnp.einsum('bclsgr,bcsgrp->bclgrp', cb[..., None] * seg, X)

    def step(S, inp):
        x_c, b_c, c_c, a_c = inp
        y_off = jnp.einsum('blgn,bgrpn,blgr->blgrp', c_c, S, jnp.exp(a_c))
        a_last = a_c[:, -1]
        S = (S * jnp.exp(a_last)[..., None, None]
             + jnp.einsum('blgn,blgr,blgrp->bgrpn', b_c, jnp.exp(a_last[:, None] - a_c), x_c))
        return S, y_off

    S0 = jnp.zeros((Bsz, G, R, P, Bc.shape[-1]), jnp.float32)
    sw = lambda t: jnp.swapaxes(t, 0, 1)
    _, y_off = lax.scan(step, S0, (sw(X), sw(Bc), sw(Cc), sw(acum)))
    return (y_diag + sw(y_off)).reshape(Bsz, T, G, R, P)


def _mixer_ssd(h, w_in, conv_w, conv_b, a_log, dt_bias, d_skip, norm_w, w_out):
    Bsz, T, _ = h.shape
    G, R, P, N = SSD_GROUPS, SSD_HEADS // SSD_GROUPS, SSD_HEADDIM, SSD_STATE
    proj = h @ w_in
    z = proj[..., :SSD_INNER]
    xbc = jax.nn.silu(_dwconv(proj[..., SSD_INNER:SSD_INNER + SSD_CONV_CH], conv_w) + conv_b)
    dt_raw = proj[..., SSD_INNER + SSD_CONV_CH:].astype(jnp.float32).reshape(Bsz, T, 2, G, R)
    xs = xbc[..., :SSD_INNER].reshape(Bsz, T, G, R, P)
    bm = xbc[..., SSD_INNER:SSD_INNER + SSD_BC].reshape(Bsz, T, G, N)
    cm = xbc[..., SSD_INNER + SSD_BC:].reshape(Bsz, T, G, N)
    dt = jax.nn.softplus(dt_raw + dt_bias.astype(jnp.float32).reshape(2, G, R))
    a = -jnp.exp(a_log.astype(jnp.float32)).reshape(2, G, R)
    y = (_ssd_scan(xs, dt[:, :, 0], a[0], bm, cm)
         + _flip(_ssd_scan(_flip(xs), _flip(dt[:, :, 1]), a[1], _flip(bm), _flip(cm))))
    y = y + d_skip.astype(jnp.float32).reshape(G, R, 1) * xs.astype(jnp.float32)
    y = y.reshape(Bsz, T, G, R * P) * jax.nn.silu(z.astype(jnp.float32)).reshape(Bsz, T, G, R * P)
    y = _rmsnorm(y, norm_w.reshape(G, R * P))
    return y.reshape(Bsz, T, SSD_INNER).astype(h.dtype) @ w_out


def _rope(t):
    Bsz, T, H, Dh = t.shape
    half = Dh // 2
    inv_freq = ROPE_THETA ** (-jnp.arange(half, dtype=jnp.float32) / half)
    ang = jnp.arange(T, dtype=jnp.float32)[:, None] * inv_freq[None, :]
    cos, sin = jnp.cos(ang)[:, None, :], jnp.sin(ang)[:, None, :]
    t1 = t[..., :half].astype(jnp.float32)
    t2 = t[..., half:].astype(jnp.float32)
    return jnp.concatenate([t1 * cos - t2 * sin, t2 * cos + t1 * sin], axis=-1).astype(t.dtype)


def _window_gqa(q, k, v, sink):
    Bsz, T, HQ, Dh = q.shape
    HKV = k.shape[2]
    rep = HQ // HKV
    W = SWA_WINDOW
    NB = T // W
    qb = q.reshape(Bsz, NB, W, HKV, rep, Dh)

    def band(t):
        tb = jnp.pad(t, ((0, 0), (W, W), (0, 0), (0, 0))).reshape(Bsz, NB + 2, W, HKV, Dh)
        return jnp.concatenate([tb[:, :-2], tb[:, 1:-1], tb[:, 2:]], axis=2)

    kw, vw = band(k), band(v)
    s = jnp.einsum('bnqgrd,bnkgd->bngrqk', qb, kw).astype(jnp.float32) * Dh ** -0.5
    qpos = jnp.arange(NB)[:, None, None] * W + jnp.arange(W)[None, :, None]
    kpos = jnp.arange(NB)[:, None, None] * W - W + jnp.arange(3 * W)[None, None, :]
    valid = (jnp.abs(qpos - kpos) <= W) & (kpos >= 0) & (kpos < T)
    s = jnp.where(valid[None, :, None, None], s, -jnp.inf)
    sk = sink.astype(jnp.float32).reshape(1, 1, HKV, rep, 1, 1)
    m = jnp.maximum(s.max(axis=-1, keepdims=True), sk)
    p = jnp.exp(s - m)
    p = p / (p.sum(axis=-1, keepdims=True) + jnp.exp(sk - m))
    o = jnp.einsum('bngrqk,bnkgd->bnqgrd', p.astype(vw.dtype), vw)
    return o.reshape(Bsz, T, HQ, Dh)


def _mixer_swa(h, w_in, sink, w_out):
    Bsz, T, _ = h.shape
    proj = h @ w_in
    qd, kd = SWA_HEADS * SWA_HEADDIM, SWA_KV_HEADS * SWA_HEADDIM
    q = _rope(proj[..., :qd].reshape(Bsz, T, SWA_HEADS, SWA_HEADDIM))
    k = _rope(proj[..., qd:qd + kd].reshape(Bsz, T, SWA_KV_HEADS, SWA_HEADDIM))
    v = proj[..., qd + kd:].reshape(Bsz, T, SWA_KV_HEADS, SWA_HEADDIM)
    o = _window_gqa(q, k, v, sink)
    return o.reshape(Bsz, T, SWA_OUT) @ w_out


def _neighbourhood_attn(q, k, v, rpb):
    Bsz, T, H, Dh = q.shape
    rows = T // GRID_W
    wr = min(NA_ROWS, rows)
    ncb = GRID_W // NA_COLS
    span = 2 * NA_COLS
    qcol = np.arange(GRID_W).reshape(ncb, NA_COLS)
    kcol = np.clip(np.arange(ncb) * NA_COLS - NA_COLS // 2, 0, GRID_W - span)[:, None] + np.arange(span)
    cstart = np.clip(qcol - NA_COLS // 2, 0, GRID_W - NA_COLS)
    col_valid = ((kcol[:, None, :] >= cstart[:, :, None])
                 & (kcol[:, None, :] < cstart[:, :, None] + NA_COLS))
    col_idx = np.clip(kcol[:, None, :] - qcol[:, :, None] + NA_COLS - 1, 0, 2 * NA_COLS - 2)
    qg = q.reshape(Bsz, rows, ncb, NA_COLS, H, Dh)
    kg = k.reshape(Bsz, rows, GRID_W, H, Dh)
    vg = v.reshape(Bsz, rows, GRID_W, H, Dh)
    rpb_cols = rpb.astype(jnp.float32)[:, :, col_idx]

    def row_block(r):
        rs = jnp.clip(r - wr // 2, 0, rows - wr)
        k_blk = lax.dynamic_slice_in_dim(kg, rs, wr, axis=1)[:, :, kcol]
        v_blk = lax.dynamic_slice_in_dim(vg, rs, wr, axis=1)[:, :, kcol]
        q_r = lax.dynamic_index_in_dim(qg, r, axis=1, keepdims=False)
        s = jnp.einsum('bmqhd,bwmkhd->bhmqwk', q_r, k_blk).astype(jnp.float32) * Dh ** -0.5
        bias = rpb_cols[:, rs + jnp.arange(wr) - r + NA_ROWS - 1]
        s = s + jnp.moveaxis(bias, 1, 3)[None]
        s = jnp.where(col_valid[None, None, :, :, None, :], s, -jnp.inf)
        p = jax.nn.softmax(s, axis=(-2, -1))
        o = jnp.einsum('bhmqwk,bwmkhd->bmqhd', p.astype(v_blk.dtype), v_blk)
        return o.reshape(Bsz, GRID_W, H, Dh)

    o = lax.map(row_block, jnp.arange(rows))
    return jnp.moveaxis(o, 0, 1).reshape(Bsz, T, H, Dh)


def _mixer_na(h, w_in, rpb, w_out):
    Bsz, T, _ = h.shape
    proj = h @ w_in
    q = proj[..., :NA_WIDTH].reshape(Bsz, T, NA_HEADS, NA_HEADDIM)
    k = proj[..., NA_WIDTH:2 * NA_WIDTH].reshape(Bsz, T, NA_HEADS, NA_HEADDIM)
    v = proj[..., 2 * NA_WIDTH:].reshape(Bsz, T, NA_HEADS, NA_HEADDIM)
    o = _neighbourhood_attn(q, k, v, rpb)
    return o.reshape(Bsz, T, NA_WIDTH) @ w_out


def _peer(h, w_q, sub_keys, u_tab, v_tab):
    Bsz, T, D = h.shape
    ht = h.reshape(-1, D)
    n_tok = ht.shape[0]
    q = (ht @ w_q).reshape(n_tok, PEER_HEADS, 2, PEER_HALF)
    s = jnp.einsum('thpd,hpkd->thpk', q, sub_keys).astype(jnp.float32)
    s_top, i_top = lax.top_k(s, PEER_TOPK)
    cand = (s_top[:, :, 0, :, None] + s_top[:, :, 1, None, :]).reshape(n_tok, PEER_HEADS, PEER_TOPK ** 2)
    cand_id = (i_top[:, :, 0, :, None] * PEER_NKEYS + i_top[:, :, 1, None, :]).reshape(n_tok, PEER_HEADS, PEER_TOPK ** 2)
    best, pos = lax.top_k(cand, PEER_TOPK)
    expert = jnp.take_along_axis(cand_id, pos, axis=-1)
    gate = jax.nn.softmax(best, axis=-1)
    nb = n_tok // PEER_BLOCK

    def block(args):
        xb, eb, gb = args
        act = jax.nn.gelu(jnp.einsum('thkd,td->thk', u_tab[eb], xb).astype(jnp.float32), approximate=False) * gb
        return jnp.einsum('thk,thkd->td', act.astype(v_tab.dtype), v_tab[eb])

    y = lax.map(block, (ht.reshape(nb, PEER_BLOCK, D),
                        expert.reshape(nb, PEER_BLOCK, PEER_HEADS, PEER_TOPK),
                        gate.reshape(nb, PEER_BLOCK, PEER_HEADS, PEER_TOPK)))
    return y.reshape(Bsz, T, D).astype(h.dtype)


def setup_inputs(seed: int = 0) -> dict:
    key = jax.random.key(seed)
    keys = iter(jax.random.split(key, 40))
    D = D_MODEL

    def normal(shape, scale):
        return scale * jax.random.normal(next(keys), shape, jnp.float32)

    def gain(shape):
        return 1.0 + normal(shape, 0.02)

    def a_log(shape):
        return jnp.log(jax.random.uniform(next(keys), shape, jnp.float32, 1.0, 16.0))

    def dt_bias(shape):
        dt = jnp.exp(jax.random.uniform(next(keys), shape, jnp.float32, math.log(1e-3), math.log(1e-1)))
        return dt + jnp.log(-jnp.expm1(-dt))

    nA, nB, nC, nD = [len(range(m, DEPTH, N_MIXERS)) for m in range(N_MIXERS)]
    return {
        'x': normal((BATCH, SEQ, D), 1.0),
        'norm_mix': gain((DEPTH, D)),
        'norm_ffn': gain((DEPTH, D)),
        'norm_final': gain((D,)),
        'gdn_w_in': normal((nA, D, GDN_IN), D ** -0.5),
        'gdn_conv': normal((nA, CONV_K, GDN_CONV_CH), CONV_K ** -0.5),
        'gdn_a_log': a_log((nA, 2, GDN_HEADS)),
        'gdn_dt_bias': dt_bias((nA, 2, GDN_HEADS)),
        'gdn_norm': gain((nA, GDN_DV)),
        'gdn_w_out': normal((nA, GDN_V, D), GDN_V ** -0.5),
        'ssd_w_in': normal((nB, D, SSD_IN), D ** -0.5),
        'ssd_conv': normal((nB, CONV_K, SSD_CONV_CH), CONV_K ** -0.5),
        'ssd_conv_b': normal((nB, SSD_CONV_CH), 0.02),
        'ssd_a_log': a_log((nB, 2, SSD_HEADS)),
        'ssd_dt_bias': dt_bias((nB, 2, SSD_HEADS)),
        'ssd_d': 1.0 + normal((nB, SSD_HEADS), 0.1),
        'ssd_norm': gain((nB, SSD_INNER)),
        'ssd_w_out': normal((nB, SSD_INNER, D), SSD_INNER ** -0.5),
        'swa_w_in': normal((nC, D, SWA_IN), D ** -0.5),
        'swa_sink': normal((nC, SWA_HEADS), 1.0),
        'swa_w_out': normal((nC, SWA_OUT, D), SWA_OUT ** -0.5),
        'na_w_in': normal((nD, D, NA_IN), D ** -0.5),
        'na_rpb': normal((nD, NA_HEADS, 2 * NA_ROWS - 1, 2 * NA_COLS - 1), 0.1),
        'na_w_out': normal((nD, NA_WIDTH, D), NA_WIDTH ** -0.5),
        'peer_w_q': normal((DEPTH, D, PEER_HEADS * PEER_QDIM), D ** -0.5),
        'peer_keys': normal((DEPTH, PEER_HEADS, 2, PEER_NKEYS, PEER_HALF), PEER_HALF ** -0.5),
        'peer_u': normal((DEPTH, PEER_EXPERTS, D), D ** -0.5),
        'peer_v': normal((DEPTH, PEER_EXPERTS, D), PEER_TOPK ** -0.5),
    }


def reference(x, norm_mix, norm_ffn, norm_final,
              gdn_w_in, gdn_conv, gdn_a_log, gdn_dt_bias, gdn_norm, gdn_w_out,
              ssd_w_in, ssd_conv, ssd_conv_b, ssd_a_log, ssd_dt_bias, ssd_d, ssd_norm, ssd_w_out,
              swa_w_in, swa_sink, swa_w_out,
              na_w_in, na_rpb, na_w_out,
              peer_w_q, peer_keys, peer_u, peer_v):
    for i in range(DEPTH):
        mixer, j = i % N_MIXERS, i // N_MIXERS
        h = _rmsnorm(x, norm_mix[i])
        if mixer == 0:
            y = _mixer_gdn(h, gdn_w_in[j], gdn_conv[j], gdn_a_log[j], gdn_dt_bias[j], gdn_norm[j], gdn_w_out[j])
        elif mixer == 1:
            y = _mixer_ssd(h, ssd_w_in[j], ssd_conv[j], ssd_conv_b[j], ssd_a_log[j], ssd_dt_bias[j],
                           ssd_d[j], ssd_norm[j], ssd_w_out[j])
        elif mixer == 2:
            y = _mixer_swa(h, swa_w_in[j], swa_sink[j], swa_w_out[j])
        else:
            y = _mixer_na(h, na_w_in[j], na_rpb[j], na_w_out[j])
        x = x + y.astype(x.dtype)
        x = x + _peer(_rmsnorm(x, norm_ffn[i]), peer_w_q[i], peer_keys[i], peer_u[i], peer_v[i])
    return _rmsnorm(x, norm_final)
```

```python
import functools
import math

import jax
import jax.numpy as jnp
import numpy as np
from jax import lax
from jax.experimental import pallas as pl
from jax.experimental.pallas import tpu as pltpu

F32 = jnp.float32
BF16 = jnp.bfloat16
NEG = -1e30

D_MODEL = 1024
SEQ = 2048
GRID_W = 64
CONV_K = 5
NORM_EPS = 1e-6

GDN_HEADS = 8
GDN_DK = 128
GDN_CHUNK = 64
GDN_QK = 1024
GDN_V = 1024
GDN_CONV_CH = 3072

SSD_INNER = 2048
SSD_HEADDIM = 64
SSD_HEADS = 32
SSD_GROUPS = 4
SSD_STATE = 128
SSD_CHUNK = 64
SSD_BC = 512
SSD_CONV_CH = 3072

SWA_HEADS = 16
SWA_KV_HEADS = 4
SWA_HEADDIM = 64
SWA_WINDOW = 128
ROPE_THETA = 10000.0

NA_HEADS = 16
NA_HEADDIM = 64
NA_ROWS = 8
NA_COLS = 16
NA_WIDTH = 1024

PEER_HEADS = 8
PEER_NKEYS = 128
PEER_QDIM = 256
PEER_HALF = 128
PEER_TOPK = 16

GDN_PHASE_A_CHUNKS = 8
SSD_PHASE_A_CHUNKS = 4

VMEM_LIMIT_BYTES = 52 * 1024 * 1024


def _cparams(*sem):
    return pltpu.CompilerParams(dimension_semantics=sem, vmem_limit_bytes=VMEM_LIMIT_BYTES)


def _norm_matmul_body(x_ref, g_ref, w_ref, o_ref, hn_ref):
    @pl.when(pl.program_id(1) == 0)
    def _():
        x = x_ref[...]
        ms = jnp.mean(x * x, axis=-1, keepdims=True)
        hn_ref[...] = (x * lax.rsqrt(ms + NORM_EPS) * g_ref[...]).astype(BF16)

    o_ref[...] = jnp.dot(hn_ref[...], w_ref[...], preferred_element_type=F32).astype(o_ref.dtype)


def norm_matmul(x, gain, w, out_dtype, tm=512, tn=512):
    m, d = x.shape
    n = w.shape[1]
    tn = min(tn, n)
    assert m % tm == 0 and n % tn == 0
    return pl.pallas_call(
        _norm_matmul_body,
        grid=(m // tm, n // tn),
        in_specs=[
            pl.BlockSpec((tm, d), lambda i, j: (i, 0)),
            pl.BlockSpec((1, d), lambda i, j: (0, 0)),
            pl.BlockSpec((d, tn), lambda i, j: (0, j)),
        ],
        out_specs=pl.BlockSpec((tm, tn), lambda i, j: (i, j)),
        out_shape=jax.ShapeDtypeStruct((m, n), out_dtype),
        scratch_shapes=[pltpu.VMEM((tm, d), BF16)],
        compiler_params=_cparams("parallel", "arbitrary"),
        name="norm_matmul",
    )(x, gain.reshape(1, d), w)


def _matmul_residual_body(x_ref, y_ref, w_ref, o_ref):
    o_ref[...] = x_ref[...] + jnp.dot(y_ref[...], w_ref[...], preferred_element_type=F32)


def matmul_residual(x, y, w, tm=512):
    m, d = x.shape
    k = y.shape[1]
    return pl.pallas_call(
        _matmul_residual_body,
        grid=(m // tm,),
        in_specs=[
            pl.BlockSpec((tm, d), lambda i: (i, 0)),
            pl.BlockSpec((tm, k), lambda i: (i, 0)),
            pl.BlockSpec((k, d), lambda i: (0, 0)),
        ],
        out_specs=pl.BlockSpec((tm, d), lambda i: (i, 0)),
        out_shape=jax.ShapeDtypeStruct((m, d), F32),
        compiler_params=_cparams("parallel"),
        name="matmul_residual",
    )(x, y, w)


def _final_norm_body(x_ref, g_ref, o_ref):
    x = x_ref[...]
    ms = jnp.mean(x * x, axis=-1, keepdims=True)
    o_ref[...] = x * lax.rsqrt(ms + NORM_EPS) * g_ref[...]


def final_norm(x, gain, tm=1024):
    m, d = x.shape
    return pl.pallas_call(
        _final_norm_body,
        grid=(m // tm,),
        in_specs=[pl.BlockSpec((tm, d), lambda i: (i, 0)), pl.BlockSpec((1, d), lambda i: (0, 0))],
        out_specs=pl.BlockSpec((tm, d), lambda i: (i, 0)),
        out_shape=jax.ShapeDtypeStruct((m, d), F32),
        compiler_params=_cparams("parallel"),
        name="final_norm",
    )(x, gain.reshape(1, d))


def _top16_desc(s):
    work = s
    rank = jnp.zeros_like(s)
    vals = []
    for r in range(PEER_TOPK):
        m = jnp.max(work, axis=0, keepdims=True)
        eq = work == m
        rank = jnp.where(eq, float(r + 1), rank)
        work = jnp.where(eq, -jnp.inf, work)
        vals.append(m)
    return jnp.concatenate(vals, axis=0), rank


def _peer_route_body(x_ref, g_ref, wqT_ref, keys_ref, hnT_ref, n_ref, e0_ref, r1_ref, e1_ref):
    x = x_ref[...]
    ms = jnp.mean(x * x, axis=-1, keepdims=True)
    hn = x * lax.rsqrt(ms + NORM_EPS) * g_ref[...]
    hnT = hn.T.astype(BF16)
    hnT_ref[...] = hnT
    qT = jnp.dot(wqT_ref[...], hnT, preferred_element_type=F32)
    for h in range(PEER_HEADS):
        s, a, rk = [], [], []
        for p in range(2):
            hp = 2 * h + p
            q_hp = qT[hp * PEER_HALF:(hp + 1) * PEER_HALF, :].astype(BF16)
            s_hp = jnp.dot(keys_ref[hp], q_hp, preferred_element_type=F32)
            a_hp, rk_hp = _top16_desc(s_hp)
            s.append(s_hp)
            a.append(a_hp)
            rk.append(rk_hp)
        cand = jnp.concatenate([a[0][p:p + 1, :] + a[1] for p in range(PEER_TOPK)], axis=0)
        best, _ = _top16_desc(cand)
        tau = best[PEER_TOPK - 1:PEER_TOPK, :]
        z = jnp.sum(jnp.exp(best - best[0:1, :]), axis=0, keepdims=True)
        sel = (cand >= tau).astype(F32)
        n_map = jnp.zeros_like(rk[0])
        for p in range(PEER_TOPK):
            n_p = jnp.sum(sel[p * PEER_TOPK:(p + 1) * PEER_TOPK, :], axis=0, keepdims=True)
            n_map = jnp.where(rk[0] == float(p + 1), n_p, n_map)
        n_ref[h] = n_map
        e0_ref[h] = jnp.exp(s[0] - a[0][0:1, :]) / z
        r1_ref[h] = jnp.where(rk[1] > 0.0, rk[1], 99.0)
        e1_ref[h] = jnp.exp(s[1] - a[1][0:1, :])


def peer_route(x, gain, wqT, keys, tm=256):
    m, d = x.shape
    nq = wqT.shape[0]
    tab = jax.ShapeDtypeStruct((PEER_HEADS, PEER_NKEYS, m), F32)
    tab_spec = pl.BlockSpec((PEER_HEADS, PEER_NKEYS, tm), lambda i: (0, 0, i))
    return pl.pallas_call(
        _peer_route_body,
        grid=(m // tm,),
        in_specs=[
            pl.BlockSpec((tm, d), lambda i: (i, 0)),
            pl.BlockSpec((1, d), lambda i: (0, 0)),
            pl.BlockSpec((nq, d), lambda i: (0, 0)),
            pl.BlockSpec(keys.shape, lambda i: (0, 0, 0)),
        ],
        out_specs=[pl.BlockSpec((d, tm), lambda i: (0, i)), tab_spec, tab_spec, tab_spec, tab_spec],
        out_shape=[jax.ShapeDtypeStruct((d, m), BF16), tab, tab, tab, tab],
        compiler_params=_cparams("parallel"),
        name="peer_route",
    )(x, gain.reshape(1, d), wqT, keys)


def _gelu(z):
    return 0.5 * z * (1.0 + lax.erf(z * (1.0 / math.sqrt(2.0))))


def _peer_expert_body(x_ref, hnT_ref, n_ref, e0_ref, r1_ref, e1_ref, u_ref, vT_ref, o_ref, acc_ref, *, ni):
    ib = pl.program_id(1)

    @pl.when(ib == 0)
    def _():
        acc_ref[...] = jnp.zeros_like(acc_ref)

    z = jnp.dot(u_ref[...], hnT_ref[...], preferred_element_type=F32)
    rows = []
    for ii in range(ni):
        i = ib * ni + ii
        act = _gelu(z[ii * PEER_NKEYS:(ii + 1) * PEER_NKEYS, :])
        gate = jnp.zeros_like(act)
        for h in range(PEER_HEADS):
            n_row = n_ref[h, pl.ds(i, 1), :]
            e0_row = e0_ref[h, pl.ds(i, 1), :]
            gate = gate + jnp.where(r1_ref[h] <= n_row, e1_ref[h] * e0_row, 0.0)
        rows.append((act * gate).astype(BF16))
    a = jnp.concatenate(rows, axis=0) if ni > 1 else rows[0]
    acc_ref[...] += jnp.dot(vT_ref[...], a, preferred_element_type=F32)

    @pl.when(ib == pl.num_programs(1) - 1)
    def _():
        o_ref[...] = x_ref[...] + acc_ref[...].T


def peer_experts(x, hnT, n_tab, e0_tab, r1_tab, e1_tab, u, vT, tm=512, ni=4):
    m, d = x.shape
    ne = u.shape[0]
    et = ni * PEER_NKEYS
    tab_spec = pl.BlockSpec((PEER_HEADS, PEER_NKEYS, tm), lambda i, j: (0, 0, i))
    return pl.pallas_call(
        functools.partial(_peer_expert_body, ni=ni),
        grid=(m // tm, ne // et),
        in_specs=[
            pl.BlockSpec((tm, d), lambda i, j: (i, 0)),
            pl.BlockSpec((d, tm), lambda i, j: (0, i)),
            tab_spec, tab_spec, tab_spec, tab_spec,
            pl.BlockSpec((et, d), lambda i, j: (j, 0)),
            pl.BlockSpec((d, et), lambda i, j: (0, j)),
        ],
        out_specs=pl.BlockSpec((tm, d), lambda i, j: (i, 0)),
        out_shape=jax.ShapeDtypeStruct((m, d), F32),
        scratch_shapes=[pltpu.VMEM((d, tm), F32)],
        compiler_params=_cparams("parallel", "arbitrary"),
        name="peer_experts",
    )(x, hnT, n_tab, e0_tab, r1_tab, e1_tab, u, vT)


def peer_layer(x, gain, w_q, keys, u_tab, v_tab):
    wqT = w_q.T.astype(BF16)
    keys2 = keys.reshape(PEER_HEADS * 2, PEER_NKEYS, PEER_HALF).astype(BF16)
    hnT, n_tab, e0_tab, r1_tab, e1_tab = peer_route(x, gain, wqT, keys2)
    return peer_experts(x, hnT, n_tab, e0_tab, r1_tab, e1_tab, u_tab.astype(BF16), v_tab.T.astype(BF16))


def _rope_lanes(t, cos, sin_signed):
    half = SWA_HEADDIM // 2
    lane = lax.broadcasted_iota(jnp.int32, t.shape, 1)
    first = (lane % SWA_HEADDIM) < half
    partner = jnp.where(first, pltpu.roll(t, 128 - half, 1), pltpu.roll(t, half, 1))
    return t * cos + partner * sin_signed


def _swa_body(q_ref, k_ref, v_ref, cos_ref, sin_ref, sink_ref, o_ref, kp_ref, vp_ref):
    w = SWA_WINDOW
    t_len = q_ref.shape[1]
    nkv = SWA_KV_HEADS * SWA_HEADDIM
    rep = SWA_HEADS // SWA_KV_HEADS
    zeros = jnp.zeros((w, nkv), BF16)
    kp_ref[pl.ds(0, w), :] = zeros
    kp_ref[pl.ds(w + t_len, w), :] = zeros
    vp_ref[pl.ds(0, w), :] = zeros
    vp_ref[pl.ds(w + t_len, w), :] = zeros
    vp_ref[pl.ds(w, t_len), :] = v_ref[0]
    cos_all = cos_ref[...]
    sin_all = sin_ref[...]
    for c in range(nkv // 128):
        kc = k_ref[0, :, c * 128:(c + 1) * 128].astype(F32)
        kp_ref[pl.ds(w, t_len), c * 128:(c + 1) * 128] = _rope_lanes(kc, cos_all, sin_all).astype(BF16)

    row = lax.broadcasted_iota(jnp.int32, (w, 3 * w), 0)
    col = lax.broadcasted_iota(jnp.int32, (w, 3 * w), 1)
    band = (col >= row) & (col <= row + 2 * w)

    def block(n, carry):
        base = pl.multiple_of(n * w, w)
        cos_b = cos_ref[pl.ds(base, w), :]
        sin_b = sin_ref[pl.ds(base, w), :]
        kpos = base - w + col
        valid = band & (kpos >= 0) & (kpos < t_len)
        outs = []
        for c in range(SWA_HEADS * SWA_HEADDIM // 128):
            qc = q_ref[0, pl.ds(base, w), c * 128:(c + 1) * 128].astype(F32)
            qc = (_rope_lanes(qc, cos_b, sin_b) * (SWA_HEADDIM ** -0.5)).astype(BF16)
            for hh in range(2):
                h = 2 * c + hh
                g = h // rep
                qh = qc[:, hh * SWA_HEADDIM:(hh + 1) * SWA_HEADDIM]
                kw = kp_ref[pl.ds(base, 3 * w), g * SWA_HEADDIM:(g + 1) * SWA_HEADDIM]
                vw = vp_ref[pl.ds(base, 3 * w), g * SWA_HEADDIM:(g + 1) * SWA_HEADDIM]
                s = lax.dot_general(qh, kw, (((1,), (1,)), ((), ())), preferred_element_type=F32)
                s = jnp.where(valid, s, NEG)
                sk = sink_ref[h]
                m = jnp.maximum(jnp.max(s, axis=-1, keepdims=True), sk)
                p = jnp.exp(s - m)
                denom = jnp.sum(p, axis=-1, keepdims=True) + jnp.exp(sk - m)
                p = (p / denom).astype(BF16)
                outs.append(jnp.dot(p, vw, preferred_element_type=F32))
        o_ref[0, pl.ds(base, w), :] = jnp.concatenate(outs, axis=-1).astype(o_ref.dtype)
        return carry

    lax.fori_loop(0, t_len // w, block, 0)


def swa_attention(proj, sink, bsz, t_len):
    half = SWA_HEADDIM // 2
    inv_freq = ROPE_THETA ** (-jnp.arange(half, dtype=F32) / half)
    ang = jnp.arange(t_len, dtype=F32)[:, None] * inv_freq[None, :]
    cos, sin = jnp.cos(ang), jnp.sin(ang)
    cos_t = jnp.tile(jnp.concatenate([cos, cos], axis=-1), (1, 2))
    sin_t = jnp.tile(jnp.concatenate([-sin, sin], axis=-1), (1, 2))
    nq = SWA_HEADS * SWA_HEADDIM
    nkv = SWA_KV_HEADS * SWA_HEADDIM
    return pl.pallas_call(
        _swa_body,
        grid=(bsz,),
        in_specs=[
            pl.BlockSpec((1, t_len, nq), lambda b: (b, 0, 0)),
            pl.BlockSpec((1, t_len, nkv), lambda b: (b, 0, nq // nkv)),
            pl.BlockSpec((1, t_len, nkv), lambda b: (b, 0, nq // nkv + 1)),
            pl.BlockSpec((t_len, 128), lambda b: (0, 0)),
            pl.BlockSpec((t_len, 128), lambda b: (0, 0)),
            pl.BlockSpec(memory_space=pltpu.SMEM),
        ],
        out_specs=pl.BlockSpec((1, t_len, nq), lambda b: (b, 0, 0)),
        out_shape=jax.ShapeDtypeStruct((bsz, t_len, nq), BF16),
        scratch_shapes=[pltpu.VMEM((t_len + 2 * SWA_WINDOW, nkv), BF16),
                        pltpu.VMEM((t_len + 2 * SWA_WINDOW, nkv), BF16)],
        compiler_params=_cparams("parallel"),
        name="swa_attention",
    )(proj, proj, proj, cos_t, sin_t, sink.astype(F32))


def _na_bias_table(rpb):
    shift = np.arange(NA_ROWS)[:, None]
    krow = np.arange(NA_ROWS)[None, :]
    ridx = krow - shift + NA_ROWS - 1
    qc = np.arange(GRID_W)[:, None]
    kc = np.arange(GRID_W)[None, :]
    cstart = np.clip(qc - NA_COLS // 2, 0, GRID_W - NA_COLS)
    valid = (kc >= cstart) & (kc < cstart + NA_COLS)
    cidx = np.clip(kc - qc + NA_COLS - 1, 0, 2 * NA_COLS - 2)
    tab = rpb.astype(F32)[:, ridx[:, :, None, None], cidx[None, None, :, :]]
    tab = jnp.where(valid[None, None, None], tab, NEG)
    tab = jnp.transpose(tab, (0, 1, 3, 2, 4))
    return tab.reshape(rpb.shape[0], NA_ROWS, GRID_W, NA_ROWS * GRID_W)


def _na_body(q_ref, k_ref, v_ref, bias_ref, o_ref):
    t_len = q_ref.shape[1]
    rows = t_len // GRID_W
    win = NA_ROWS * GRID_W

    def row(r, carry):
        rs = jnp.clip(r - NA_ROWS // 2, 0, rows - NA_ROWS)
        qbase = pl.multiple_of(r * GRID_W, GRID_W)
        kbase = pl.multiple_of(rs * GRID_W, GRID_W)
        qr = q_ref[0, pl.ds(qbase, GRID_W), :]
        kw = k_ref[0, pl.ds(kbase, win), :]
        vw = v_ref[0, pl.ds(kbase, win), :]
        outs = []
        for hh in range(2):
            sl = slice(hh * NA_HEADDIM, (hh + 1) * NA_HEADDIM)
            s = lax.dot_general(qr[:, sl], kw[:, sl], (((1,), (1,)), ((), ())), preferred_element_type=F32)
            s = s * (NA_HEADDIM ** -0.5) + bias_ref[hh, r - rs]
            m = jnp.max(s, axis=-1, keepdims=True)
            p = jnp.exp(s - m)
            p = (p / jnp.sum(p, axis=-1, keepdims=True)).astype(BF16)
            outs.append(jnp.dot(p, vw[:, sl], preferred_element_type=F32))
        o_ref[0, pl.ds(qbase, GRID_W), :] = jnp.concatenate(outs, axis=-1).astype(o_ref.dtype)
        return carry

    lax.fori_loop(0, rows, row, 0)


def na_attention(proj, rpb, bsz, t_len):
    bias = _na_bias_table(rpb)
    npair = NA_HEADS // 2
    win = NA_ROWS * GRID_W
    return pl.pallas_call(
        _na_body,
        grid=(npair, bsz),
        in_specs=[
            pl.BlockSpec((1, t_len, 128), lambda hp, b: (b, 0, hp)),
            pl.BlockSpec((1, t_len, 128), lambda hp, b: (b, 0, npair + hp)),
            pl.BlockSpec((1, t_len, 128), lambda hp, b: (b, 0, 2 * npair + hp)),
            pl.BlockSpec((2, NA_ROWS, GRID_W, win), lambda hp, b: (hp, 0, 0, 0)),
        ],
        out_specs=pl.BlockSpec((1, t_len, 128), lambda hp, b: (b, 0, hp)),
        out_shape=jax.ShapeDtypeStruct((bsz, t_len, NA_WIDTH), BF16),
        compiler_params=_cparams("parallel", "parallel"),
        name="na_attention",
    )(proj, proj, proj, bias)


def _conv_silu_body(x_ref, w_ref, b_ref, o_ref, *, n_l2):
    x = x_ref[0].astype(F32)
    t_len, tc = x.shape
    row = lax.broadcasted_iota(jnp.int32, x.shape, 0)
    acc = x * w_ref[CONV_K // 2:CONV_K // 2 + 1, :] + b_ref[...]
    for k in range(CONV_K):
        off = k - CONV_K // 2
        if off == 0:
            continue
        shifted = pltpu.roll(x, (-off) % t_len, 0)
        valid = (row + off >= 0) & (row + off < t_len)
        acc = acc + jnp.where(valid, shifted, 0.0) * w_ref[k:k + 1, :]
    y = acc * jax.nn.sigmoid(acc)

    if n_l2 == 0:
        o_ref[0] = y.astype(o_ref.dtype)
    else:
        j = pl.program_id(1)

        @pl.when(j < n_l2)
        def _():
            parts = []
            for c in range(tc // 128):
                yc = y[:, c * 128:(c + 1) * 128]
                ss = jnp.sum(yc * yc, axis=-1, keepdims=True)
                parts.append(yc * lax.rsqrt(ss + NORM_EPS))
            o_ref[0] = jnp.concatenate(parts, axis=-1).astype(o_ref.dtype)

        @pl.when(j >= n_l2)
        def _():
            o_ref[0] = y.astype(o_ref.dtype)


def conv_silu(proj, w, bias, col0, n_ch, n_l2=0, tc=512):
    bsz, t_len, _ = proj.shape
    assert col0 % tc == 0 and n_ch % tc == 0
    c0 = col0 // tc
    return pl.pallas_call(
        functools.partial(_conv_silu_body, n_l2=n_l2),
        grid=(bsz, n_ch // tc),
        in_specs=[
            pl.BlockSpec((1, t_len, tc), lambda b, j: (b, 0, c0 + j)),
            pl.BlockSpec((CONV_K, tc), lambda b, j: (0, j)),
            pl.BlockSpec((1, tc), lambda b, j: (0, j)),
        ],
        out_specs=pl.BlockSpec((1, t_len, tc), lambda b, j: (b, 0, j)),
        out_shape=jax.ShapeDtypeStruct((bsz, t_len, n_ch), BF16),
        compiler_params=_cparams("parallel", "parallel"),
        name="conv_silu",
    )(proj, w.astype(F32), bias.astype(F32).reshape(1, n_ch))


def _softplus(x):
    return jnp.maximum(x, 0.0) + jnp.log1p(jnp.exp(-jnp.abs(x)))


def _bmm(a, b, precision=None):
    return lax.dot_general(a, b, (((2,), (1,)), ((0,), (0,))), precision=precision,
                           preferred_element_type=F32)


def _bmm_nt(a, b):
    return lax.dot_general(a, b, (((2,), (2,)), ((0,), (0,))), preferred_element_type=F32)


def _gdn_body(q_ref, k_ref, v_ref, z_ref, gcol_ref, grow_ref, alog_ref, dtb_ref, nw_ref, o_ref,
              u_s, w_s, qk_s, qg_s, kg_s, el_s, o_s):
    c_len = GDN_CHUNK
    t_len = q_ref.shape[1]
    nc = t_len // c_len
    h = pl.program_id(1)
    hi = lax.Precision.HIGHEST
    ii = lax.broadcasted_iota(jnp.int32, (c_len, c_len), 0)
    jj = lax.broadcasted_iota(jnp.int32, (c_len, c_len), 1)
    eye = (ii == jj).astype(F32)
    cg = GDN_PHASE_A_CHUNKS

    def phase_a(gi, carry):
        c0 = pl.multiple_of(gi * cg, cg)
        rows = pl.ds(pl.multiple_of(gi * (cg * c_len), cg * c_len), cg * c_len)
        k3 = k_ref[0, rows, :].reshape(cg, c_len, GDN_DK)
        v3 = v_ref[0, rows, :].reshape(cg, c_len, GDN_DK).astype(F32)
        kf = k3.astype(F32)
        qs = q_ref[0, rows, :].reshape(cg, c_len, GDN_DK).astype(F32) * (GDN_DK ** -0.5)
        gcol = gcol_ref[0, 0, pl.ds(c0, cg)]
        grow = grow_ref[0, 0, pl.ds(c0, cg)]
        qk_raw = _bmm_nt(qs.astype(BF16), k3)
        for d in range(2):
            incl = (ii >= jj) if d == 0 else (ii <= jj)
            strict = (ii > jj) if d == 0 else (ii < jj)
            tri = incl.astype(F32)
            tri_t = ((ii <= jj) if d == 0 else (ii >= jj)).astype(F32)
            neg_a = -jnp.exp(jnp.full((1, 1, 1), alog_ref[d, h], F32))
            dtb = dtb_ref[d, h]
            g_c = neg_a * _softplus(gcol[:, :, d:d + 1] + dtb)
            g_r = neg_a * _softplus(grow[:, d:d + 1, :] + dtb)
            beta_c = jax.nn.sigmoid(gcol[:, :, 2 + d:3 + d])
            gc_c = jnp.sum(tri[None] * g_r, axis=2, keepdims=True)
            gc_r = jnp.sum(tri_t[None] * g_c, axis=1, keepdims=True)
            g_last = jnp.sum(g_r, axis=2, keepdims=True)
            decay = jnp.where(incl[None], jnp.exp(jnp.where(incl[None], gc_c - gc_r, 0.0)), 0.0)
            kb = kf * beta_c
            kk = _bmm_nt(kb.astype(BF16), k3)
            low = jnp.where(strict[None], kk * decay, 0.0)
            inv = eye[None] - low
            pw = low
            for _ in range(5):
                pw = _bmm(pw, pw, hi)
                inv = inv + _bmm(inv, pw, hi)
            rhs = jnp.concatenate([v3 * beta_c, kb * jnp.exp(gc_c)], axis=-1)
            sol = _bmm(inv, rhs, hi)
            u_s[d, pl.ds(c0, cg)] = sol[:, :, :GDN_DK]
            w_s[d, pl.ds(c0, cg)] = sol[:, :, GDN_DK:].astype(BF16)
            qk_s[d, pl.ds(c0, cg)] = (qk_raw * decay).astype(BF16)
            qg_s[d, pl.ds(c0, cg)] = (qs * jnp.exp(gc_c)).astype(BF16)
            kg_s[d, pl.ds(c0, cg)] = (kf * jnp.exp(g_last - gc_c)).astype(BF16)
            el_s[d, pl.ds(c0, cg)] = jnp.broadcast_to(jnp.exp(g_last), (cg, 1, GDN_DK))
        return carry

    lax.fori_loop(0, nc // cg, phase_a, 0)

    def chunk_step(d, c, s):
        sb = s.astype(BF16)
        v_new = u_s[d, c] - jnp.dot(w_s[d, c], sb, preferred_element_type=F32)
        vb = v_new.astype(BF16)
        o = (jnp.dot(qg_s[d, c], sb, preferred_element_type=F32)
             + jnp.dot(qk_s[d, c], vb, preferred_element_type=F32))
        s = s * el_s[d, c] + lax.dot_general(kg_s[d, c], vb, (((0,), (0,)), ((), ())),
                                             preferred_element_type=F32)
        o_s[d, c] = o
        return s

    def step(t, carry):
        s_f, s_b = carry
        return chunk_step(0, t, s_f), chunk_step(1, nc - 1 - t, s_b)

    s0 = jnp.zeros((GDN_DK, GDN_DK), F32)
    lax.fori_loop(0, nc, step, (s0, s0))

    o = (o_s[0] + o_s[1]).reshape(t_len, GDN_DK)
    ms = jnp.mean(o * o, axis=-1, keepdims=True)
    z = z_ref[0]
    y = o * lax.rsqrt(ms + NORM_EPS) * nw_ref[...] * (z * jax.nn.sigmoid(z))
    o_ref[0] = y.astype(o_ref.dtype)


def gdn_scan(qkv, proj, gates, a_log, dt_bias, norm_w, bsz, t_len):
    nc = t_len // GDN_CHUNK
    nh = GDN_HEADS
    g4 = gates[:, :4 * nh].reshape(bsz, nc, GDN_CHUNK, 4, nh)
    gcol = jnp.transpose(g4, (0, 4, 1, 2, 3))
    grow = jnp.transpose(g4, (0, 4, 1, 3, 2))
    dk = GDN_DK
    nq = GDN_QK // dk
    return pl.pallas_call(
        _gdn_body,
        grid=(bsz, nh),
        in_specs=[
            pl.BlockSpec((1, t_len, dk), lambda b, h: (b, 0, h)),
            pl.BlockSpec((1, t_len, dk), lambda b, h: (b, 0, nq + h)),
            pl.BlockSpec((1, t_len, dk), lambda b, h: (b, 0, 2 * nq + h)),
            pl.BlockSpec((1, t_len, dk), lambda b, h: (b, 0, 3 * nq + h)),
            pl.BlockSpec((1, 1, nc, GDN_CHUNK, 4), lambda b, h: (b, h, 0, 0, 0)),
            pl.BlockSpec((1, 1, nc, 4, GDN_CHUNK), lambda b, h: (b, h, 0, 0, 0)),
            pl.BlockSpec(memory_space=pltpu.SMEM),
            pl.BlockSpec(memory_space=pltpu.SMEM),
            pl.BlockSpec((1, dk), lambda b, h: (0, 0)),
        ],
        out_specs=pl.BlockSpec((1, t_len, dk), lambda b, h: (b, 0, h)),
        out_shape=jax.ShapeDtypeStruct((bsz, t_len, GDN_V), BF16),
        scratch_shapes=[
            pltpu.VMEM((2, nc, GDN_CHUNK, dk), F32),
            pltpu.VMEM((2, nc, GDN_CHUNK, dk), BF16),
            pltpu.VMEM((2, nc, GDN_CHUNK, GDN_CHUNK), BF16),
            pltpu.VMEM((2, nc, GDN_CHUNK, dk), BF16),
            pltpu.VMEM((2, nc, GDN_CHUNK, dk), BF16),
            pltpu.VMEM((2, nc, 1, dk), F32),
            pltpu.VMEM((2, nc, GDN_CHUNK, dk), F32),
        ],
        compiler_params=_cparams("parallel", "parallel"),
        name="gdn_scan",
    )(qkv, qkv, qkv, proj, gcol, grow, a_log.astype(F32), dt_bias.astype(F32),
      norm_w.astype(F32).reshape(1, dk))


def _pad_cols(w, n):
    return jnp.pad(w, ((0, 0), (0, n - w.shape[1])))


def gdn_layer(x, gain, w_in, conv_w, a_log, dt_bias, norm_w, w_out, bsz, t_len):
    n_main = GDN_CONV_CH + GDN_V
    proj = norm_matmul(x, gain, w_in[:, :n_main].astype(BF16), F32)
    gates = norm_matmul(x, gain, _pad_cols(w_in[:, n_main:], 128).astype(BF16), F32)
    proj3 = proj.reshape(bsz, t_len, n_main)
    qkv = conv_silu(proj3, conv_w, jnp.zeros((GDN_CONV_CH,), F32), 0, GDN_CONV_CH,
                    n_l2=2 * GDN_QK // 512)
    y = gdn_scan(qkv, proj3, gates, a_log, dt_bias, norm_w, bsz, t_len)
    return matmul_residual(x, y.reshape(bsz * t_len, GDN_V), w_out.astype(BF16))


def _ssd_body(xs_ref, b_ref, c_ref, z_ref, dcol_ref, drow_ref, alog_ref, dtb_ref, dskip_ref, nw_ref, o_ref,
              xw_s, ea_s, el_s, y_s, st_s):
    c_len = SSD_CHUNK
    t_len = xs_ref.shape[1]
    nc = t_len // c_len
    nr = SSD_HEADS // SSD_GROUPS
    hp = SSD_HEADDIM
    g = pl.program_id(1)
    ii = lax.broadcasted_iota(jnp.int32, (c_len, c_len), 0)
    jj = lax.broadcasted_iota(jnp.int32, (c_len, c_len), 1)
    cg = SSD_PHASE_A_CHUNKS

    def phase_a(gi, carry):
        c0 = pl.multiple_of(gi * cg, cg)
        rows = pl.ds(pl.multiple_of(gi * (cg * c_len), cg * c_len), cg * c_len)
        x3 = xs_ref[0, rows, :].reshape(cg, c_len, nr * hp)
        b3 = b_ref[0, rows, :].reshape(cg, c_len, SSD_STATE)
        c3 = c_ref[0, rows, :].reshape(cg, c_len, SSD_STATE)
        dcol = dcol_ref[0, 0, pl.ds(c0, cg)]
        drow = drow_ref[0, 0, pl.ds(c0, cg)]
        cb = _bmm_nt(c3, b3)
        ydiag = None
        for d in range(2):
            incl = (ii >= jj) if d == 0 else (ii <= jj)
            tri = incl.astype(F32)
            tri_t = ((ii <= jj) if d == 0 else (ii >= jj)).astype(F32)
            yd, xw, ea, el = [], [], [], []
            for r in range(nr):
                hidx = g * nr + r
                a = -jnp.exp(jnp.full((1, 1, 1), alog_ref[d, hidx], F32))
                dtb = dtb_ref[d, hidx]
                col = d * nr + r
                dt_c = _softplus(dcol[:, :, col:col + 1] + dtb)
                dt_r = _softplus(drow[:, col:col + 1, :] + dtb)
                ac_c = jnp.sum(tri[None] * (dt_r * a), axis=2, keepdims=True)
                ac_r = jnp.sum(tri_t[None] * (dt_c * a), axis=1, keepdims=True)
                a_last = jnp.sum(dt_r * a, axis=2, keepdims=True)
                seg = jnp.where(incl[None], jnp.exp(jnp.where(incl[None], ac_c - ac_r, 0.0)), 0.0)
                xr = x3[:, :, r * hp:(r + 1) * hp].astype(F32) * dt_c
                yd.append(_bmm((cb * seg).astype(BF16), xr.astype(BF16)))
                xw.append((xr * jnp.exp(a_last - ac_c)).astype(BF16))
                ea.append(jnp.broadcast_to(jnp.exp(ac_c), (cg, c_len, hp)))
                el.append(jnp.broadcast_to(jnp.exp(a_last), (cg, 1, hp)))
            yd = jnp.concatenate(yd, axis=-1)
            ydiag = yd if ydiag is None else ydiag + yd
            xw_s[d, pl.ds(c0, cg)] = jnp.concatenate(xw, axis=-1)
            ea_s[d, pl.ds(c0, cg)] = jnp.concatenate(ea, axis=-1)
            el_s[d, pl.ds(c0, cg)] = jnp.concatenate(el, axis=-1)
        y_s[pl.ds(c0, cg)] = ydiag
        return carry

    lax.fori_loop(0, nc // cg, phase_a, 0)

    st_s[...] = jnp.zeros_like(st_s)

    def step(t, carry):
        for d in range(2):
            c = t if d == 0 else nc - 1 - t
            base = pl.multiple_of(c * c_len, c_len)
            cc = c_ref[0, pl.ds(base, c_len), :]
            bb = b_ref[0, pl.ds(base, c_len), :]
            st = st_s[d]
            y_s[c] += jnp.dot(cc, st.astype(BF16), preferred_element_type=F32) * ea_s[d, c]
            st_s[d] = st * el_s[d, c] + lax.dot_general(bb, xw_s[d, c], (((0,), (0,)), ((), ())),
                                                        preferred_element_type=F32)
        return carry

    lax.fori_loop(0, nc, step, 0)

    xs = xs_ref[0].astype(F32)
    z = z_ref[0]
    y = (y_s[...].reshape(t_len, nr * hp) + dskip_ref[...] * xs) * (z * jax.nn.sigmoid(z))
    ms = jnp.mean(y * y, axis=-1, keepdims=True)
    o_ref[0] = (y * lax.rsqrt(ms + NORM_EPS) * nw_ref[...]).astype(o_ref.dtype)


def ssd_scan(xbc, proj, dts, a_log, dt_bias, d_skip, norm_w, bsz, t_len):
    nc = t_len // SSD_CHUNK
    ng = SSD_GROUPS
    nr = SSD_HEADS // ng
    gw = nr * SSD_HEADDIM
    d6 = dts[:, :2 * SSD_HEADS].reshape(bsz, nc, SSD_CHUNK, 2, ng, nr)
    dcol = jnp.transpose(d6, (0, 4, 1, 2, 3, 5)).reshape(bsz, ng, nc, SSD_CHUNK, 2 * nr)
    drow = jnp.transpose(d6, (0, 4, 1, 3, 5, 2)).reshape(bsz, ng, nc, 2 * nr, SSD_CHUNK)
    nb0 = SSD_INNER // SSD_STATE
    return pl.pallas_call(
        _ssd_body,
        grid=(bsz, ng),
        in_specs=[
            pl.BlockSpec((1, t_len, gw), lambda b, g: (b, 0, g)),
            pl.BlockSpec((1, t_len, SSD_STATE), lambda b, g: (b, 0, nb0 + g)),
            pl.BlockSpec((1, t_len, SSD_STATE), lambda b, g: (b, 0, nb0 + ng + g)),
            pl.BlockSpec((1, t_len, gw), lambda b, g: (b, 0, g)),
            pl.BlockSpec((1, 1, nc, SSD_CHUNK, 2 * nr), lambda b, g: (b, g, 0, 0, 0)),
            pl.BlockSpec((1, 1, nc, 2 * nr, SSD_CHUNK), lambda b, g: (b, g, 0, 0, 0)),
            pl.BlockSpec(memory_space=pltpu.SMEM),
            pl.BlockSpec(memory_space=pltpu.SMEM),
            pl.BlockSpec((1, gw), lambda b, g: (0, g)),
            pl.BlockSpec((1, gw), lambda b, g: (0, g)),
        ],
        out_specs=pl.BlockSpec((1, t_len, gw), lambda b, g: (b, 0, g)),
        out_shape=jax.ShapeDtypeStruct((bsz, t_len, SSD_INNER), BF16),
        scratch_shapes=[
            pltpu.VMEM((2, nc, SSD_CHUNK, gw), BF16),
            pltpu.VMEM((2, nc, SSD_CHUNK, gw), F32),
            pltpu.VMEM((2, nc, 1, gw), F32),
            pltpu.VMEM((nc, SSD_CHUNK, gw), F32),
            pltpu.VMEM((2, SSD_STATE, gw), F32),
        ],
        compiler_params=_cparams("parallel", "parallel"),
        name="ssd_scan",
    )(xbc, xbc, xbc, proj, dcol, drow, a_log.astype(F32), dt_bias.astype(F32),
      jnp.repeat(d_skip.astype(F32), SSD_HEADDIM).reshape(1, SSD_INNER),
      norm_w.astype(F32).reshape(1, SSD_INNER))


def ssd_layer(x, gain, w_in, conv_w, conv_b, a_log, dt_bias, d_skip, norm_w, w_out, bsz, t_len):
    n_main = SSD_INNER + SSD_CONV_CH
    proj = norm_matmul(x, gain, w_in[:, :n_main].astype(BF16), F32)
    dts = norm_matmul(x, gain, _pad_cols(w_in[:, n_main:], 128).astype(BF16), F32)
    proj3 = proj.reshape(bsz, t_len, n_main)
    xbc = conv_silu(proj3, conv_w, conv_b, SSD_INNER, SSD_CONV_CH)
    y = ssd_scan(xbc, proj3, dts, a_log, dt_bias, d_skip, norm_w, bsz, t_len)
    return matmul_residual(x, y.reshape(bsz * t_len, SSD_INNER), w_out.astype(BF16))


def swa_layer(x, gain, w_in, sink, w_out, bsz, t_len):
    proj = norm_matmul(x, gain, w_in.astype(BF16), BF16)
    o = swa_attention(proj.reshape(bsz, t_len, -1), sink, bsz, t_len)
    return matmul_residual(x, o.reshape(bsz * t_len, -1), w_out.astype(BF16))


def na_layer(x, gain, w_in, rpb, w_out, bsz, t_len):
    proj = norm_matmul(x, gain, w_in.astype(BF16), BF16)
    o = na_attention(proj.reshape(bsz, t_len, -1), rpb, bsz, t_len)
    return matmul_residual(x, o.reshape(bsz * t_len, -1), w_out.astype(BF16))


def kernel(x, norm_mix, norm_ffn, norm_final, gdn_w_in, gdn_conv, gdn_a_log, gdn_dt_bias, gdn_norm, gdn_w_out, ssd_w_in, ssd_conv, ssd_conv_b, ssd_a_log, ssd_dt_bias, ssd_d, ssd_norm, ssd_w_out, swa_w_in, swa_sink, swa_w_out, na_w_in, na_rpb, na_w_out, peer_w_q, peer_keys, peer_u, peer_v):
    bsz, t_len, d = x.shape
    depth = norm_mix.shape[0]
    xf = x.reshape(bsz * t_len, d)
    for i in range(depth):
        mixer, j = i % 4, i // 4
        if mixer == 0:
            xf = gdn_layer(xf, norm_mix[i], gdn_w_in[j], gdn_conv[j], gdn_a_log[j], gdn_dt_bias[j],
                           gdn_norm[j], gdn_w_out[j], bsz, t_len)
        elif mixer == 1:
            xf = ssd_layer(xf, norm_mix[i], ssd_w_in[j], ssd_conv[j], ssd_conv_b[j], ssd_a_log[j],
                           ssd_dt_bias[j], ssd_d[j], ssd_norm[j], ssd_w_out[j], bsz, t_len)
        elif mixer == 2:
            xf = swa_layer(xf, norm_mix[i], swa_w_in[j], swa_sink[j], swa_w_out[j], bsz, t_len)
        else:
            xf = na_layer(xf, norm_mix[i], na_w_in[j], na_rpb[j], na_w_out[j], bsz, t_len)
        xf = peer_layer(xf, norm_ffn[i], peer_w_q[i], peer_keys[i], peer_u[i], peer_v[i])
    return final_norm(xf, norm_final).reshape(bsz, t_len, d)
```

```python
import functools
import math

import jax
import jax.numpy as jnp
import numpy as np
from jax import lax
from jax.experimental import pallas as pl
from jax.experimental.pallas import tpu as pltpu

F32 = jnp.float32
BF16 = jnp.bfloat16
NEG = -1e30

D_MODEL = 1024
SEQ = 2048
GRID_W = 64
CONV_K = 5
NORM_EPS = 1e-6

GDN_HEADS = 8
GDN_DK = 128
GDN_CHUNK = 64
GDN_QK = 1024
GDN_V = 1024
GDN_CONV_CH = 3072

SSD_INNER = 2048
SSD_HEADDIM = 64
SSD_HEADS = 32
SSD_GROUPS = 4
SSD_STATE = 128
SSD_CHUNK = 64
SSD_BC = 512
SSD_CONV_CH = 3072

SWA_HEADS = 16
SWA_KV_HEADS = 4
SWA_HEADDIM = 64
SWA_WINDOW = 128
ROPE_THETA = 10000.0

NA_HEADS = 16
NA_HEADDIM = 64
NA_ROWS = 8
NA_COLS = 16
NA_WIDTH = 1024

PEER_HEADS = 8
PEER_NKEYS = 128
PEER_QDIM = 256
PEER_HALF = 128
PEER_TOPK = 16

GDN_PHASE_A_CHUNKS = 8
SSD_PHASE_A_CHUNKS = 4

NA_ROW_UNROLL = 4

BF16_SUBLANES = 16

VMEM_LIMIT_BYTES = 52 * 1024 * 1024


def _cparams(*sem, flags=None):
    return pltpu.CompilerParams(dimension_semantics=sem, vmem_limit_bytes=VMEM_LIMIT_BYTES, flags=flags)


def _norm_matmul_body(x_ref, g_ref, w_ref, o_ref, hn_ref):
    @pl.when(pl.program_id(1) == 0)
    def _():
        x = x_ref[...]
        ms = jnp.mean(x * x, axis=-1, keepdims=True)
        hn_ref[...] = (x * lax.rsqrt(ms + NORM_EPS) * g_ref[...]).astype(BF16)

    o_ref[...] = jnp.dot(hn_ref[...], w_ref[...], preferred_element_type=F32).astype(o_ref.dtype)


def norm_matmul(x, gain, w, out_dtype, tm=512, tn=512):
    m, d = x.shape
    n = w.shape[1]
    tn = min(tn, n)
    assert m % tm == 0 and n % tn == 0
    return pl.pallas_call(
        _norm_matmul_body,
        grid=(m // tm, n // tn),
        in_specs=[
            pl.BlockSpec((tm, d), lambda i, j: (i, 0)),
            pl.BlockSpec((1, d), lambda i, j: (0, 0)),
            pl.BlockSpec((d, tn), lambda i, j: (0, j)),
        ],
        out_specs=pl.BlockSpec((tm, tn), lambda i, j: (i, j)),
        out_shape=jax.ShapeDtypeStruct((m, n), out_dtype),
        scratch_shapes=[pltpu.VMEM((tm, d), BF16)],
        compiler_params=_cparams("parallel", "arbitrary"),
        name="norm_matmul",
    )(x, gain.reshape(1, d), w)


def _matmul_residual_body(x_ref, y_ref, w_ref, o_ref):
    o_ref[...] = x_ref[...] + jnp.dot(y_ref[...], w_ref[...], preferred_element_type=F32)


def matmul_residual(x, y, w, tm=512):
    m, d = x.shape
    k = y.shape[1]
    return pl.pallas_call(
        _matmul_residual_body,
        grid=(m // tm,),
        in_specs=[
            pl.BlockSpec((tm, d), lambda i: (i, 0)),
            pl.BlockSpec((tm, k), lambda i: (i, 0)),
            pl.BlockSpec((k, d), lambda i: (0, 0)),
        ],
        out_specs=pl.BlockSpec((tm, d), lambda i: (i, 0)),
        out_shape=jax.ShapeDtypeStruct((m, d), F32),
        compiler_params=_cparams("parallel"),
        name="matmul_residual",
    )(x, y, w)


def _final_norm_body(x_ref, g_ref, o_ref):
    x = x_ref[...]
    ms = jnp.mean(x * x, axis=-1, keepdims=True)
    o_ref[...] = x * lax.rsqrt(ms + NORM_EPS) * g_ref[...]


def final_norm(x, gain, tm=1024):
    m, d = x.shape
    return pl.pallas_call(
        _final_norm_body,
        grid=(m // tm,),
        in_specs=[pl.BlockSpec((tm, d), lambda i: (i, 0)), pl.BlockSpec((1, d), lambda i: (0, 0))],
        out_specs=pl.BlockSpec((tm, d), lambda i: (i, 0)),
        out_shape=jax.ShapeDtypeStruct((m, d), F32),
        compiler_params=_cparams("parallel"),
        name="final_norm",
    )(x, gain.reshape(1, d))


def _top16_desc(s, with_rank):
    work = s
    rank = jnp.full(s.shape, 99.0, F32) if with_rank else None
    vals = []
    for r in range(PEER_TOPK):
        m = jnp.max(work, axis=0, keepdims=True)
        eq = work == m
        if with_rank:
            rank = jnp.where(eq, float(r + 1), rank)
        work = jnp.where(eq, -jnp.inf, work)
        vals.append(m)
    return jnp.concatenate(vals, axis=0), rank


def _peer_route_body(x_ref, g_ref, wqT_ref, keys_ref, hnT_ref, n_ref, e0_ref, r1_ref, e1_ref):
    x = x_ref[...]
    ms = jnp.mean(x * x, axis=-1, keepdims=True)
    hn = x * lax.rsqrt(ms + NORM_EPS) * g_ref[...]
    hnT = hn.T.astype(BF16)
    hnT_ref[...] = hnT
    qT = jnp.dot(wqT_ref[...], hnT, preferred_element_type=F32)
    row8 = lax.broadcasted_iota(jnp.int32, (8, x.shape[0]), 0)
    for h in range(PEER_HEADS):
        s = []
        for p in range(2):
            hp = 2 * h + p
            q_hp = qT[hp * PEER_HALF:(hp + 1) * PEER_HALF, :].astype(BF16)
            s.append(jnp.dot(keys_ref[hp], q_hp, preferred_element_type=F32))
        a0, _ = _top16_desc(s[0], False)
        b, rank1 = _top16_desc(s[1], True)
        cand = [a0[0:1, :] + b]
        for p in range(1, PEER_TOPK):
            cnt = PEER_TOPK // (p + 1)
            c = a0[p:p + 1, :] + b[0:8, :]
            cand.append(c if cnt >= 8 else jnp.where(row8 < cnt, c, -jnp.inf))
        best, _ = _top16_desc(jnp.concatenate(cand, axis=0), False)
        tau = best[PEER_TOPK - 1:PEER_TOPK, :]
        z = jnp.sum(jnp.exp(best - best[0:1, :]), axis=0, keepdims=True)
        n_map = jnp.zeros_like(s[0])
        for q in range(PEER_TOPK):
            n_map = n_map + jnp.where(s[0] + b[q:q + 1, :] >= tau, 1.0, 0.0)
        n_ref[h] = jnp.where(s[0] >= a0[PEER_TOPK - 1:PEER_TOPK, :], n_map, 0.0)
        e0_ref[h] = jnp.exp(s[0] - a0[0:1, :]) * (0.5 / z)
        r1_ref[h] = rank1.astype(BF16)
        e1_ref[h] = jnp.exp(s[1] - b[0:1, :]).astype(BF16)


def peer_route(x, gain, wqT, keys, tm=256):
    m, d = x.shape
    nq = wqT.shape[0]
    tab = jax.ShapeDtypeStruct((PEER_HEADS, PEER_NKEYS, m), F32)
    tab16 = jax.ShapeDtypeStruct((PEER_HEADS, PEER_NKEYS, m), BF16)
    tab_spec = pl.BlockSpec((PEER_HEADS, PEER_NKEYS, tm), lambda i: (0, 0, i))
    return pl.pallas_call(
        _peer_route_body,
        grid=(m // tm,),
        in_specs=[
            pl.BlockSpec((tm, d), lambda i: (i, 0)),
            pl.BlockSpec((1, d), lambda i: (0, 0)),
            pl.BlockSpec((nq, d), lambda i: (0, 0)),
            pl.BlockSpec(keys.shape, lambda i: (0, 0, 0)),
        ],
        out_specs=[pl.BlockSpec((d, tm), lambda i: (0, i)), tab_spec, tab_spec, tab_spec, tab_spec],
        out_shape=[jax.ShapeDtypeStruct((d, m), BF16), tab, tab, tab16, tab16],
        compiler_params=_cparams("parallel"),
        name="peer_route",
    )(x, gain.reshape(1, d), wqT, keys)


def _peer_expert_body(x_ref, hnT_ref, n_ref, e0_ref, r1_ref, e1_ref, u_ref, vT_ref, o_ref,
                      z_s, a_s, acc_ref, *, ni, nb):
    j = pl.program_id(1)

    @pl.when(j == 0)
    def _():
        z_s[...] = jnp.zeros_like(z_s)
        a_s[...] = jnp.zeros_like(a_s)
        acc_ref[...] = jnp.zeros_like(acc_ref)

    ib = jnp.clip(j - 1, 0, nb - 1)
    tm = z_s.shape[2]
    pk = BF16_SUBLANES
    nk = PEER_NKEYS
    cur = j % 2
    prev = 1 - cur

    z_s[cur] = jnp.dot(u_ref[...], hnT_ref[...], preferred_element_type=F32)
    acc_ref[...] += jnp.dot(vT_ref[...], a_s[cur], preferred_element_type=F32)
    for ii in range(ni):
        i = ib * ni + ii
        rows = pl.ds(ii * nk, nk)
        z = z_s[prev, rows, :]
        act = (z * (1.0 + lax.erf(z * (1.0 / math.sqrt(2.0))))).astype(BF16)
        gate = None
        for h in range(PEER_HEADS):
            n_b = jnp.broadcast_to(n_ref[h, pl.ds(i, 1), :], (pk, tm)).astype(BF16)[None]
            e0_b = jnp.broadcast_to(e0_ref[h, pl.ds(i, 1), :], (pk, tm)).astype(BF16)[None]
            r1 = r1_ref[h].reshape(nk // pk, pk, tm)
            e1 = e1_ref[h].reshape(nk // pk, pk, tm)
            term = jnp.where(r1 <= n_b, e1 * e0_b, jnp.zeros((), BF16))
            gate = term if gate is None else gate + term
        a_s[prev, rows, :] = (act.reshape(nk // pk, pk, tm) * gate).reshape(nk, tm)

    @pl.when(j == nb + 1)
    def _():
        o_ref[...] = x_ref[...] + acc_ref[...].T


def peer_experts(x, hnT, n_tab, e0_tab, r1_tab, e1_tab, u, vT, tm=512, ni=8):
    m, d = x.shape
    ne = u.shape[0]
    et = ni * PEER_NKEYS
    nb = ne // et
    tab_spec = pl.BlockSpec((PEER_HEADS, PEER_NKEYS, tm), lambda i, j: (0, 0, i))
    return pl.pallas_call(
        functools.partial(_peer_expert_body, ni=ni, nb=nb),
        grid=(m // tm, nb + 2),
        in_specs=[
            pl.BlockSpec((tm, d), lambda i, j: (i, 0)),
            pl.BlockSpec((d, tm), lambda i, j: (0, i)),
            tab_spec, tab_spec, tab_spec, tab_spec,
            pl.BlockSpec((et, d), lambda i, j: (jnp.minimum(j, nb - 1), 0)),
            pl.BlockSpec((d, et), lambda i, j: (0, jnp.clip(j - 2, 0, nb - 1))),
        ],
        out_specs=pl.BlockSpec((tm, d), lambda i, j: (i, 0)),
        out_shape=jax.ShapeDtypeStruct((m, d), F32),
        scratch_shapes=[pltpu.VMEM((2, et, tm), F32), pltpu.VMEM((2, et, tm), BF16), pltpu.VMEM((d, tm), F32)],
        compiler_params=_cparams("parallel", "arbitrary"),
        name="peer_experts",
    )(x, hnT, n_tab, e0_tab, r1_tab, e1_tab, u, vT)


def peer_layer(x, gain, w_q, keys, u_tab, v_tab):
    wqT = w_q.T.astype(BF16)
    keys2 = keys.reshape(PEER_HEADS * 2, PEER_NKEYS, PEER_HALF).astype(BF16)
    hnT, n_tab, e0_tab, r1_tab, e1_tab = peer_route(x, gain, wqT, keys2)
    return peer_experts(x, hnT, n_tab, e0_tab, r1_tab, e1_tab, u_tab.astype(BF16), v_tab.T.astype(BF16))


def _rope_lanes(t, cos, sin_signed):
    half = SWA_HEADDIM // 2
    lane = lax.broadcasted_iota(jnp.int32, t.shape, 1)
    first = (lane % SWA_HEADDIM) < half
    partner = jnp.where(first, pltpu.roll(t, 128 - half, 1), pltpu.roll(t, half, 1))
    return t * cos + partner * sin_signed


def _swa_body(q_ref, k_ref, v_ref, cos_ref, sin_ref, sink_ref, o_ref, kp_ref, vp_ref):
    w = SWA_WINDOW
    t_len = q_ref.shape[1]
    nkv = SWA_KV_HEADS * SWA_HEADDIM
    rep = SWA_HEADS // SWA_KV_HEADS
    zeros = jnp.zeros((w, nkv), BF16)
    kp_ref[pl.ds(0, w), :] = zeros
    kp_ref[pl.ds(w + t_len, w), :] = zeros
    vp_ref[pl.ds(0, w), :] = zeros
    vp_ref[pl.ds(w + t_len, w), :] = zeros
    vp_ref[pl.ds(w, t_len), :] = v_ref[0]
    cos_all = cos_ref[...]
    sin_all = sin_ref[...]
    for c in range(nkv // 128):
        kc = k_ref[0, :, c * 128:(c + 1) * 128].astype(F32)
        kp_ref[pl.ds(w, t_len), c * 128:(c + 1) * 128] = _rope_lanes(kc, cos_all, sin_all).astype(BF16)

    row = lax.broadcasted_iota(jnp.int32, (w, 3 * w), 0)
    col = lax.broadcasted_iota(jnp.int32, (w, 3 * w), 1)
    band = (col >= row) & (col <= row + 2 * w)

    def block(n, carry):
        base = pl.multiple_of(n * w, w)
        cos_b = cos_ref[pl.ds(base, w), :]
        sin_b = sin_ref[pl.ds(base, w), :]
        kpos = base - w + col
        valid = band & (kpos >= 0) & (kpos < t_len)
        outs = []
        for g in range(SWA_KV_HEADS):
            kw = kp_ref[pl.ds(base, 3 * w), g * SWA_HEADDIM:(g + 1) * SWA_HEADDIM]
            vw = vp_ref[pl.ds(base, 3 * w), g * SWA_HEADDIM:(g + 1) * SWA_HEADDIM]
            scores = []
            for c in range(g * rep // 2, (g + 1) * rep // 2):
                qc = q_ref[0, pl.ds(base, w), c * 128:(c + 1) * 128].astype(F32)
                qc = (_rope_lanes(qc, cos_b, sin_b) * (SWA_HEADDIM ** -0.5)).astype(BF16)
                for hh in range(2):
                    qh = qc[:, hh * SWA_HEADDIM:(hh + 1) * SWA_HEADDIM]
                    scores.append(lax.dot_general(qh, kw, (((1,), (1,)), ((), ())), preferred_element_type=F32))
            probs = []
            for k, s in enumerate(scores):
                s = jnp.where(valid, s, NEG)
                sk = sink_ref[g * rep + k]
                m = jnp.maximum(jnp.max(s, axis=-1, keepdims=True), sk)
                p = jnp.exp(s - m)
                inv = 1.0 / (jnp.sum(p, axis=-1, keepdims=True) + jnp.exp(sk - m))
                probs.append((p.astype(BF16), inv))
            for p, inv in probs:
                outs.append(jnp.dot(p, vw, preferred_element_type=F32) * inv)
        o_ref[0, pl.ds(base, w), :] = jnp.concatenate(outs, axis=-1).astype(o_ref.dtype)
        return carry

    lax.fori_loop(0, t_len // w, block, 0)


def swa_attention(proj, sink, bsz, t_len):
    half = SWA_HEADDIM // 2
    inv_freq = ROPE_THETA ** (-jnp.arange(half, dtype=F32) / half)
    ang = jnp.arange(t_len, dtype=F32)[:, None] * inv_freq[None, :]
    cos, sin = jnp.cos(ang), jnp.sin(ang)
    cos_t = jnp.tile(jnp.concatenate([cos, cos], axis=-1), (1, 2))
    sin_t = jnp.tile(jnp.concatenate([-sin, sin], axis=-1), (1, 2))
    nq = SWA_HEADS * SWA_HEADDIM
    nkv = SWA_KV_HEADS * SWA_HEADDIM
    return pl.pallas_call(
        _swa_body,
        grid=(bsz,),
        in_specs=[
            pl.BlockSpec((1, t_len, nq), lambda b: (b, 0, 0)),
            pl.BlockSpec((1, t_len, nkv), lambda b: (b, 0, nq // nkv)),
            pl.BlockSpec((1, t_len, nkv), lambda b: (b, 0, nq // nkv + 1)),
            pl.BlockSpec((t_len, 128), lambda b: (0, 0)),
            pl.BlockSpec((t_len, 128), lambda b: (0, 0)),
            pl.BlockSpec(memory_space=pltpu.SMEM),
        ],
        out_specs=pl.BlockSpec((1, t_len, nq), lambda b: (b, 0, 0)),
        out_shape=jax.ShapeDtypeStruct((bsz, t_len, nq), BF16),
        scratch_shapes=[pltpu.VMEM((t_len + 2 * SWA_WINDOW, nkv), BF16),
                        pltpu.VMEM((t_len + 2 * SWA_WINDOW, nkv), BF16)],
        compiler_params=_cparams("parallel"),
        name="swa_attention",
    )(proj, proj, proj, cos_t, sin_t, sink.astype(F32))


def _na_bias_table(rpb):
    qc = np.arange(GRID_W)[:, None]
    kc = np.arange(GRID_W)[None, :]
    cstart = np.clip(qc - NA_COLS // 2, 0, GRID_W - NA_COLS)
    valid = (kc >= cstart) & (kc < cstart + NA_COLS)
    cidx = np.clip(kc - qc + NA_COLS - 1, 0, 2 * NA_COLS - 2)
    onehot = (np.arange(2 * NA_COLS - 1)[:, None, None] == cidx[None]).astype(np.float32)
    toep = jnp.einsum('hrc,cqk->hrqk', rpb.astype(F32), onehot, precision=lax.Precision.HIGHEST)
    toep = jnp.where(valid[None, None], toep, NEG)
    return jnp.concatenate([toep[:, :-1], toep[:, 1:]], axis=-1)


def _na_body(q_ref, k_ref, v_ref, bias_ref, o_ref):
    t_len = q_ref.shape[1]
    rows = t_len // GRID_W
    win = NA_ROWS * GRID_W

    def row_group(gi, carry):
        chains = []
        for rr in range(NA_ROW_UNROLL):
            r = gi * NA_ROW_UNROLL + rr
            rs = jnp.clip(r - NA_ROWS // 2, 0, rows - NA_ROWS)
            d0 = rs - r + NA_ROWS - 1
            qbase = pl.multiple_of(r * GRID_W, GRID_W)
            kbase = pl.multiple_of(rs * GRID_W, GRID_W)
            qr = q_ref[0, pl.ds(qbase, GRID_W), :]
            kw = k_ref[0, pl.ds(kbase, win), :]
            for hh in range(2):
                sl = slice(hh * NA_HEADDIM, (hh + 1) * NA_HEADDIM)
                s = lax.dot_general(qr[:, sl], kw[:, sl], (((1,), (1,)), ((), ())), preferred_element_type=F32)
                chains.append((hh, d0, kbase, s))
        probs = []
        for hh, d0, kbase, s in chains:
            bias = jnp.concatenate([bias_ref[hh, d0 + 2 * c] for c in range(NA_ROWS // 2)], axis=-1)
            s = s * (NA_HEADDIM ** -0.5) + bias
            m = jnp.max(s, axis=-1, keepdims=True)
            p = jnp.exp(s - m)
            probs.append((p.astype(BF16), 1.0 / jnp.sum(p, axis=-1, keepdims=True)))
        outs = []
        for (hh, d0, kbase, s), (p, inv) in zip(chains, probs):
            vw = v_ref[0, pl.ds(kbase, win), hh * NA_HEADDIM:(hh + 1) * NA_HEADDIM]
            outs.append(jnp.dot(p, vw, preferred_element_type=F32) * inv)
        for rr in range(NA_ROW_UNROLL):
            qbase = pl.multiple_of((gi * NA_ROW_UNROLL + rr) * GRID_W, GRID_W)
            o_ref[0, pl.ds(qbase, GRID_W), :] = jnp.concatenate(outs[2 * rr:2 * rr + 2], axis=-1).astype(o_ref.dtype)
        return carry

    lax.fori_loop(0, rows // NA_ROW_UNROLL, row_group, 0)


def na_attention(proj, rpb, bsz, t_len):
    bias = _na_bias_table(rpb)
    npair = NA_HEADS // 2
    return pl.pallas_call(
        _na_body,
        grid=(npair, bsz),
        in_specs=[
            pl.BlockSpec((1, t_len, 128), lambda hp, b: (b, 0, hp)),
            pl.BlockSpec((1, t_len, 128), lambda hp, b: (b, 0, npair + hp)),
            pl.BlockSpec((1, t_len, 128), lambda hp, b: (b, 0, 2 * npair + hp)),
            pl.BlockSpec((2, 2 * NA_ROWS - 2, GRID_W, 2 * GRID_W), lambda hp, b: (hp, 0, 0, 0)),
        ],
        out_specs=pl.BlockSpec((1, t_len, 128), lambda hp, b: (b, 0, hp)),
        out_shape=jax.ShapeDtypeStruct((bsz, t_len, NA_WIDTH), BF16),
        compiler_params=_cparams("parallel", "parallel"),
        name="na_attention",
    )(proj, proj, proj, bias)


def _conv_silu_body(x_ref, w_ref, b_ref, o_ref, *, n_l2):
    x = x_ref[0].astype(F32)
    t_len, tc = x.shape
    row = lax.broadcasted_iota(jnp.int32, x.shape, 0)
    acc = x * w_ref[CONV_K // 2:CONV_K // 2 + 1, :] + b_ref[...]
    for k in range(CONV_K):
        off = k - CONV_K // 2
        if off == 0:
            continue
        shifted = pltpu.roll(x, (-off) % t_len, 0)
        valid = (row + off >= 0) & (row + off < t_len)
        acc = acc + jnp.where(valid, shifted, 0.0) * w_ref[k:k + 1, :]
    y = acc * jax.nn.sigmoid(acc)

    if n_l2 == 0:
        o_ref[0] = y.astype(o_ref.dtype)
    else:
        j = pl.program_id(1)

        @pl.when(j < n_l2)
        def _():
            parts = []
            for c in range(tc // 128):
                yc = y[:, c * 128:(c + 1) * 128]
                ss = jnp.sum(yc * yc, axis=-1, keepdims=True)
                parts.append(yc * lax.rsqrt(ss + NORM_EPS))
            o_ref[0] = jnp.concatenate(parts, axis=-1).astype(o_ref.dtype)

        @pl.when(j >= n_l2)
        def _():
            o_ref[0] = y.astype(o_ref.dtype)


def conv_silu(proj, w, bias, col0, n_ch, n_l2=0, tc=512):
    bsz, t_len, _ = proj.shape
    assert col0 % tc == 0 and n_ch % tc == 0
    c0 = col0 // tc
    return pl.pallas_call(
        functools.partial(_conv_silu_body, n_l2=n_l2),
        grid=(bsz, n_ch // tc),
        in_specs=[
            pl.BlockSpec((1, t_len, tc), lambda b, j: (b, 0, c0 + j)),
            pl.BlockSpec((CONV_K, tc), lambda b, j: (0, j)),
            pl.BlockSpec((1, tc), lambda b, j: (0, j)),
        ],
        out_specs=pl.BlockSpec((1, t_len, tc), lambda b, j: (b, 0, j)),
        out_shape=jax.ShapeDtypeStruct((bsz, t_len, n_ch), BF16),
        compiler_params=_cparams("parallel", "parallel"),
        name="conv_silu",
    )(proj, w.astype(F32), bias.astype(F32).reshape(1, n_ch))


def _softplus(x):
    return jnp.maximum(x, 0.0) + jnp.log1p(jnp.exp(-jnp.abs(x)))


def _bmm(a, b, precision=None):
    return lax.dot_general(a, b, (((2,), (1,)), ((0,), (0,))), precision=precision,
                           preferred_element_type=F32)


def _split_bf16(a):
    hi = a.astype(BF16)
    return hi, (a - hi.astype(F32)).astype(BF16)


def _bmm_split(a, b):
    return _bmm(a[0], b[0]) + _bmm(a[0], b[1]) + _bmm(a[1], b[0])


def _bmm_nt(a, b):
    return lax.dot_general(a, b, (((2,), (2,)), ((0,), (0,))), preferred_element_type=F32)


def _gdn_body(q_ref, k_ref, v_ref, z_ref, gcol_ref, grow_ref, alog_ref, dtb_ref, nw_ref, o_ref,
              u_s, w_s, qk_s, qg_s, kg_s, el_s, o_s):
    c_len = GDN_CHUNK
    t_len = q_ref.shape[1]
    nc = t_len // c_len
    h = pl.program_id(1)
    ii = lax.broadcasted_iota(jnp.int32, (c_len, c_len), 0)
    jj = lax.broadcasted_iota(jnp.int32, (c_len, c_len), 1)
    eye = (ii == jj).astype(F32)
    cg = GDN_PHASE_A_CHUNKS

    def phase_a(gi, carry):
        c0 = pl.multiple_of(gi * cg, cg)
        rows = pl.ds(pl.multiple_of(gi * (cg * c_len), cg * c_len), cg * c_len)
        k3 = k_ref[0, rows, :].reshape(cg, c_len, GDN_DK)
        v3 = v_ref[0, rows, :].reshape(cg, c_len, GDN_DK).astype(F32)
        kf = k3.astype(F32)
        qs = q_ref[0, rows, :].reshape(cg, c_len, GDN_DK).astype(F32) * (GDN_DK ** -0.5)
        gcol = gcol_ref[0, 0, pl.ds(c0, cg)]
        grow = grow_ref[0, 0, pl.ds(c0, cg)]
        qk_raw = _bmm_nt(qs.astype(BF16), k3)
        for d in range(2):
            incl = (ii >= jj) if d == 0 else (ii <= jj)
            strict = (ii > jj) if d == 0 else (ii < jj)
            tri = incl.astype(F32)
            tri_t = ((ii <= jj) if d == 0 else (ii >= jj)).astype(F32)
            neg_a = -jnp.exp(jnp.full((1, 1, 1), alog_ref[d, h], F32))
            dtb = dtb_ref[d, h]
            g_c = neg_a * _softplus(gcol[:, :, d:d + 1] + dtb)
            g_r = neg_a * _softplus(grow[:, d:d + 1, :] + dtb)
            beta_c = jax.nn.sigmoid(gcol[:, :, 2 + d:3 + d])
            gc_c = jnp.sum(tri[None] * g_r, axis=2, keepdims=True)
            gc_r = jnp.sum(tri_t[None] * g_c, axis=1, keepdims=True)
            g_last = jnp.sum(g_r, axis=2, keepdims=True)
            decay = jnp.where(incl[None], jnp.exp(jnp.where(incl[None], gc_c - gc_r, 0.0)), 0.0)
            kb = kf * beta_c
            kk = _bmm_nt(kb.astype(BF16), k3)
            low = jnp.where(strict[None], kk * decay, 0.0)
            inv = eye[None] - low
            pw = _split_bf16(low)
            for _ in range(5):
                pw = _split_bf16(_bmm_split(pw, pw))
                inv = inv + _bmm_split(_split_bf16(inv), pw)
            rhs = jnp.concatenate([v3 * beta_c, kb * jnp.exp(gc_c)], axis=-1)
            sol = _bmm_split(_split_bf16(inv), _split_bf16(rhs))
            u_s[d, pl.ds(c0, cg)] = sol[:, :, :GDN_DK]
            w_s[d, pl.ds(c0, cg)] = sol[:, :, GDN_DK:].astype(BF16)
            qk_s[d, pl.ds(c0, cg)] = (qk_raw * decay).astype(BF16)
            qg_s[d, pl.ds(c0, cg)] = (qs * jnp.exp(gc_c)).astype(BF16)
            kg_s[d, pl.ds(c0, cg)] = (kf * jnp.exp(g_last - gc_c)).astype(BF16)
            el_s[d, pl.ds(c0, cg)] = jnp.broadcast_to(jnp.exp(g_last), (cg, 1, GDN_DK))
        return carry

    lax.fori_loop(0, nc // cg, phase_a, 0)

    def step(t, carry):
        cs = (t, nc - 1 - t)
        sb = [carry[d].astype(BF16) for d in range(2)]
        ws = [jnp.dot(w_s[d, cs[d]], sb[d], preferred_element_type=F32) for d in range(2)]
        qs_ = [jnp.dot(qg_s[d, cs[d]], sb[d], preferred_element_type=F32) for d in range(2)]
        vb = [(u_s[d, cs[d]] - ws[d]).astype(BF16) for d in range(2)]
        os_ = [qs_[d] + jnp.dot(qk_s[d, cs[d]], vb[d], preferred_element_type=F32) for d in range(2)]
        kv = [lax.dot_general(kg_s[d, cs[d]], vb[d], (((0,), (0,)), ((), ())), preferred_element_type=F32)
              for d in range(2)]
        for d in range(2):
            o_s[d, cs[d]] = os_[d]
        return tuple(carry[d] * el_s[d, cs[d]] + kv[d] for d in range(2))

    s0 = jnp.zeros((GDN_DK, GDN_DK), F32)
    lax.fori_loop(0, nc, step, (s0, s0))

    o = (o_s[0] + o_s[1]).reshape(t_len, GDN_DK)
    ms = jnp.mean(o * o, axis=-1, keepdims=True)
    z = z_ref[0]
    y = o * lax.rsqrt(ms + NORM_EPS) * nw_ref[...] * (z * jax.nn.sigmoid(z))
    o_ref[0] = y.astype(o_ref.dtype)


def gdn_scan(qkv, proj, gates, a_log, dt_bias, norm_w, bsz, t_len):
    nc = t_len // GDN_CHUNK
    nh = GDN_HEADS
    g4 = gates[:, :4 * nh].reshape(bsz, nc, GDN_CHUNK, 4, nh)
    gcol = jnp.transpose(g4, (0, 4, 1, 2, 3))
    grow = jnp.transpose(g4, (0, 4, 1, 3, 2))
    dk = GDN_DK
    nq = GDN_QK // dk
    return pl.pallas_call(
        _gdn_body,
        grid=(bsz, nh),
        in_specs=[
            pl.BlockSpec((1, t_len, dk), lambda b, h: (b, 0, h)),
            pl.BlockSpec((1, t_len, dk), lambda b, h: (b, 0, nq + h)),
            pl.BlockSpec((1, t_len, dk), lambda b, h: (b, 0, 2 * nq + h)),
            pl.BlockSpec((1, t_len, dk), lambda b, h: (b, 0, 3 * nq + h)),
            pl.BlockSpec((1, 1, nc, GDN_CHUNK, 4), lambda b, h: (b, h, 0, 0, 0)),
            pl.BlockSpec((1, 1, nc, 4, GDN_CHUNK), lambda b, h: (b, h, 0, 0, 0)),
            pl.BlockSpec(memory_space=pltpu.SMEM),
            pl.BlockSpec(memory_space=pltpu.SMEM),
            pl.BlockSpec((1, dk), lambda b, h: (0, 0)),
        ],
        out_specs=pl.BlockSpec((1, t_len, dk), lambda b, h: (b, 0, h)),
        out_shape=jax.ShapeDtypeStruct((bsz, t_len, GDN_V), BF16),
        scratch_shapes=[
            pltpu.VMEM((2, nc, GDN_CHUNK, dk), F32),
            pltpu.VMEM((2, nc, GDN_CHUNK, dk), BF16),
            pltpu.VMEM((2, nc, GDN_CHUNK, GDN_CHUNK), BF16),
            pltpu.VMEM((2, nc, GDN_CHUNK, dk), BF16),
            pltpu.VMEM((2, nc, GDN_CHUNK, dk), BF16),
            pltpu.VMEM((2, nc, 1, dk), F32),
            pltpu.VMEM((2, nc, GDN_CHUNK, dk), F32),
        ],
        compiler_params=_cparams("parallel", "parallel"),
        name="gdn_scan",
    )(qkv, qkv, qkv, proj, gcol, grow, a_log.astype(F32), dt_bias.astype(F32),
      norm_w.astype(F32).reshape(1, dk))


def _pad_cols(w, n):
    return jnp.pad(w, ((0, 0), (0, n - w.shape[1])))


def gdn_layer(x, gain, w_in, conv_w, a_log, dt_bias, norm_w, w_out, bsz, t_len):
    n_main = GDN_CONV_CH + GDN_V
    proj = norm_matmul(x, gain, w_in[:, :n_main].astype(BF16), F32)
    gates = norm_matmul(x, gain, _pad_cols(w_in[:, n_main:], 128).astype(BF16), F32)
    proj3 = proj.reshape(bsz, t_len, n_main)
    qkv = conv_silu(proj3, conv_w, jnp.zeros((GDN_CONV_CH,), F32), 0, GDN_CONV_CH,
                    n_l2=2 * GDN_QK // 512)
    y = gdn_scan(qkv, proj3, gates, a_log, dt_bias, norm_w, bsz, t_len)
    return matmul_residual(x, y.reshape(bsz * t_len, GDN_V), w_out.astype(BF16))


def _ssd_body(xs_ref, b_ref, c_ref, z_ref, dcol_ref, drow_ref, alog_ref, dtb_ref, dskip_ref, nw_ref, o_ref,
              xw_s, ea_s, el_s, y_s, st_s):
    c_len = SSD_CHUNK
    t_len = xs_ref.shape[1]
    nc = t_len // c_len
    nr = SSD_HEADS // SSD_GROUPS
    hp = SSD_HEADDIM
    g = pl.program_id(1)
    ii = lax.broadcasted_iota(jnp.int32, (c_len, c_len), 0)
    jj = lax.broadcasted_iota(jnp.int32, (c_len, c_len), 1)
    cg = SSD_PHASE_A_CHUNKS

    def phase_a(gi, carry):
        c0 = pl.multiple_of(gi * cg, cg)
        rows = pl.ds(pl.multiple_of(gi * (cg * c_len), cg * c_len), cg * c_len)
        x3 = xs_ref[0, rows, :].reshape(cg, c_len, nr * hp)
        b3 = b_ref[0, rows, :].reshape(cg, c_len, SSD_STATE)
        c3 = c_ref[0, rows, :].reshape(cg, c_len, SSD_STATE)
        dcol = dcol_ref[0, 0, pl.ds(c0, cg)]
        drow = drow_ref[0, 0, pl.ds(c0, cg)]
        cb = _bmm_nt(c3, b3)
        ydiag = None
        for d in range(2):
            incl = (ii >= jj) if d == 0 else (ii <= jj)
            tri = incl.astype(F32)
            tri_t = ((ii <= jj) if d == 0 else (ii >= jj)).astype(F32)
            yd, xw, ea, el = [], [], [], []
            for r in range(nr):
                hidx = g * nr + r
                a = -jnp.exp(jnp.full((1, 1, 1), alog_ref[d, hidx], F32))
                dtb = dtb_ref[d, hidx]
                col = d * nr + r
                dt_c = _softplus(dcol[:, :, col:col + 1] + dtb)
                dt_r = _softplus(drow[:, col:col + 1, :] + dtb)
                ac_c = jnp.sum(tri[None] * (dt_r * a), axis=2, keepdims=True)
                ac_r = jnp.sum(tri_t[None] * (dt_c * a), axis=1, keepdims=True)
                a_last = jnp.sum(dt_r * a, axis=2, keepdims=True)
                seg = jnp.where(incl[None], jnp.exp(jnp.where(incl[None], ac_c - ac_r, 0.0)), 0.0)
                xr = x3[:, :, r * hp:(r + 1) * hp].astype(F32) * dt_c
                yd.append(_bmm((cb * seg).astype(BF16), xr.astype(BF16)))
                xw.append((xr * jnp.exp(a_last - ac_c)).astype(BF16))
                ea.append(jnp.broadcast_to(jnp.exp(ac_c), (cg, c_len, hp)))
                el.append(jnp.broadcast_to(jnp.exp(a_last), (cg, 1, hp)))
            yd = jnp.concatenate(yd, axis=-1)
            ydiag = yd if ydiag is None else ydiag + yd
            xw_s[d, pl.ds(c0, cg)] = jnp.concatenate(xw, axis=-1)
            ea_s[d, pl.ds(c0, cg)] = jnp.concatenate(ea, axis=-1)
            el_s[d, pl.ds(c0, cg)] = jnp.concatenate(el, axis=-1)
        y_s[pl.ds(c0, cg)] = ydiag
        return carry

    lax.fori_loop(0, nc // cg, phase_a, 0)

    st_s[...] = jnp.zeros_like(st_s)

    def step(t, carry):
        for d in range(2):
            c = t if d == 0 else nc - 1 - t
            base = pl.multiple_of(c * c_len, c_len)
            cc = c_ref[0, pl.ds(base, c_len), :]
            bb = b_ref[0, pl.ds(base, c_len), :]
            st = st_s[d]
            y_s[c] += jnp.dot(cc, st.astype(BF16), preferred_element_type=F32) * ea_s[d, c]
            st_s[d] = st * el_s[d, c] + lax.dot_general(bb, xw_s[d, c], (((0,), (0,)), ((), ())),
                                                        preferred_element_type=F32)
        return carry

    lax.fori_loop(0, nc, step, 0)

    xs = xs_ref[0].astype(F32)
    z = z_ref[0]
    y = (y_s[...].reshape(t_len, nr * hp) + dskip_ref[...] * xs) * (z * jax.nn.sigmoid(z))
    ms = jnp.mean(y * y, axis=-1, keepdims=True)
    o_ref[0] = (y * lax.rsqrt(ms + NORM_EPS) * nw_ref[...]).astype(o_ref.dtype)


def ssd_scan(xbc, proj, dts, a_log, dt_bias, d_skip, norm_w, bsz, t_len):
    nc = t_len // SSD_CHUNK
    ng = SSD_GROUPS
    nr = SSD_HEADS // ng
    gw = nr * SSD_HEADDIM
    d6 = dts[:, :2 * SSD_HEADS].reshape(bsz, nc, SSD_CHUNK, 2, ng, nr)
    dcol = jnp.transpose(d6, (0, 4, 1, 2, 3, 5)).reshape(bsz, ng, nc, SSD_CHUNK, 2 * nr)
    drow = jnp.transpose(d6, (0, 4, 1, 3, 5, 2)).reshape(bsz, ng, nc, 2 * nr, SSD_CHUNK)
    nb0 = SSD_INNER // SSD_STATE
    return pl.pallas_call(
        _ssd_body,
        grid=(bsz, ng),
        in_specs=[
            pl.BlockSpec((1, t_len, gw), lambda b, g: (b, 0, g)),
            pl.BlockSpec((1, t_len, SSD_STATE), lambda b, g: (b, 0, nb0 + g)),
            pl.BlockSpec((1, t_len, SSD_STATE), lambda b, g: (b, 0, nb0 + ng + g)),
            pl.BlockSpec((1, t_len, gw), lambda b, g: (b, 0, g)),
            pl.BlockSpec((1, 1, nc, SSD_CHUNK, 2 * nr), lambda b, g: (b, g, 0, 0, 0)),
            pl.BlockSpec((1, 1, nc, 2 * nr, SSD_CHUNK), lambda b, g: (b, g, 0, 0, 0)),
            pl.BlockSpec(memory_space=pltpu.SMEM),
            pl.BlockSpec(memory_space=pltpu.SMEM),
            pl.BlockSpec((1, gw), lambda b, g: (0, g)),
            pl.BlockSpec((1, gw), lambda b, g: (0, g)),
        ],
        out_specs=pl.BlockSpec((1, t_len, gw), lambda b, g: (b, 0, g)),
        out_shape=jax.ShapeDtypeStruct((bsz, t_len, SSD_INNER), BF16),
        scratch_shapes=[
            pltpu.VMEM((2, nc, SSD_CHUNK, gw), BF16),
            pltpu.VMEM((2, nc, SSD_CHUNK, gw), F32),
            pltpu.VMEM((2, nc, 1, gw), F32),
            pltpu.VMEM((nc, SSD_CHUNK, gw), F32),
            pltpu.VMEM((2, SSD_STATE, gw), F32),
        ],
        compiler_params=_cparams("parallel", "parallel"),
        name="ssd_scan",
    )(xbc, xbc, xbc, proj, dcol, drow, a_log.astype(F32), dt_bias.astype(F32),
      jnp.repeat(d_skip.astype(F32), SSD_HEADDIM).reshape(1, SSD_INNER),
      norm_w.astype(F32).reshape(1, SSD_INNER))


def ssd_layer(x, gain, w_in, conv_w, conv_b, a_log, dt_bias, d_skip, norm_w, w_out, bsz, t_len):
    n_main = SSD_INNER + SSD_CONV_CH
    proj = norm_matmul(x, gain, w_in[:, :n_main].astype(BF16), F32)
    dts = norm_matmul(x, gain, _pad_cols(w_in[:, n_main:], 128).astype(BF16), F32)
    proj3 = proj.reshape(bsz, t_len, n_main)
    xbc = conv_silu(proj3, conv_w, conv_b, SSD_INNER, SSD_CONV_CH)
    y = ssd_scan(xbc, proj3, dts, a_log, dt_bias, d_skip, norm_w, bsz, t_len)
    return matmul_residual(x, y.reshape(bsz * t_len, SSD_INNER), w_out.astype(BF16))


def swa_layer(x, gain, w_in, sink, w_out, bsz, t_len):
    proj = norm_matmul(x, gain, w_in.astype(BF16), BF16)
    o = swa_attention(proj.reshape(bsz, t_len, -1), sink, bsz, t_len)
    return matmul_residual(x, o.reshape(bsz * t_len, -1), w_out.astype(BF16))


def na_layer(x, gain, w_in, rpb, w_out, bsz, t_len):
    proj = norm_matmul(x, gain, w_in.astype(BF16), BF16)
    o = na_attention(proj.reshape(bsz, t_len, -1), rpb, bsz, t_len)
    return matmul_residual(x, o.reshape(bsz * t_len, -1), w_out.astype(BF16))


def kernel(x, norm_mix, norm_ffn, norm_final, gdn_w_in, gdn_conv, gdn_a_log, gdn_dt_bias, gdn_norm, gdn_w_out, ssd_w_in, ssd_conv, ssd_conv_b, ssd_a_log, ssd_dt_bias, ssd_d, ssd_norm, ssd_w_out, swa_w_in, swa_sink, swa_w_out, na_w_in, na_rpb, na_w_out, peer_w_q, peer_keys, peer_u, peer_v):
    bsz, t_len, d = x.shape
    depth = norm_mix.shape[0]
    xf = x.reshape(bsz * t_len, d)
    for i in range(depth):
        mixer, j = i % 4, i // 4
        if mixer == 0:
            xf = gdn_layer(xf, norm_mix[i], gdn_w_in[j], gdn_conv[j], gdn_a_log[j], gdn_dt_bias[j],
                           gdn_norm[j], gdn_w_out[j], bsz, t_len)
        elif mixer == 1:
            xf = ssd_layer(xf, norm_mix[i], ssd_w_in[j], ssd_conv[j], ssd_conv_b[j], ssd_a_log[j],
                           ssd_dt_bias[j], ssd_d[j], ssd_norm[j], ssd_w_out[j], bsz, t_len)
        elif mixer == 2:
            xf = swa_layer(xf, norm_mix[i], swa_w_in[j], swa_sink[j], swa_w_out[j], bsz, t_len)
        else:
            xf = na_layer(xf, norm_mix[i], na_w_in[j], na_rpb[j], na_w_out[j], bsz, t_len)
        xf = peer_layer(xf, norm_ffn[i], peer_w_q[i], peer_keys[i], peer_u[i], peer_v[i])
    return final_norm(xf, norm_final).reshape(bsz, t_len, d)
```

```python
import functools
import math

import jax
import jax.numpy as jnp
import numpy as np
from jax import lax
from jax.experimental import pallas as pl
from jax.experimental.pallas import tpu as pltpu

F32 = jnp.float32
BF16 = jnp.bfloat16
NEG = -1e30

D_MODEL = 1024
SEQ = 2048
GRID_W = 64
CONV_K = 5
NORM_EPS = 1e-6

GDN_HEADS = 8
GDN_DK = 128
GDN_CHUNK = 64
GDN_QK = 1024
GDN_V = 1024
GDN_CONV_CH = 3072

SSD_INNER = 2048
SSD_HEADDIM = 64
SSD_HEADS = 32
SSD_GROUPS = 4
SSD_STATE = 128
SSD_CHUNK = 64
SSD_BC = 512
SSD_CONV_CH = 3072

SWA_HEADS = 16
SWA_KV_HEADS = 4
SWA_HEADDIM = 64
SWA_WINDOW = 128
ROPE_THETA = 10000.0

NA_HEADS = 16
NA_HEADDIM = 64
NA_ROWS = 8
NA_COLS = 16
NA_WIDTH = 1024

PEER_HEADS = 8
PEER_NKEYS = 128
PEER_QDIM = 256
PEER_HALF = 128
PEER_TOPK = 16

GDN_PHASE_A_CHUNKS = 8
SSD_PHASE_A_CHUNKS = 4

PEER_EXPERT_ROWS = 8

NA_ROW_UNROLL = 4

BF16_SUBLANES = 16
F32_SUBLANES = 8

CONV_ROW_CHUNK = 256

VMEM_LIMIT_BYTES = 52 * 1024 * 1024


def _cparams(*sem, flags=None):
    return pltpu.CompilerParams(dimension_semantics=sem, vmem_limit_bytes=VMEM_LIMIT_BYTES, flags=flags)


def _norm_matmul_body(x_ref, g_ref, w_ref, o_ref, hn_ref):
    @pl.when(pl.program_id(1) == 0)
    def _():
        x = x_ref[...]
        ms = jnp.mean(x * x, axis=-1, keepdims=True)
        hn_ref[...] = (x * lax.rsqrt(ms + NORM_EPS) * g_ref[...]).astype(BF16)

    o_ref[...] = jnp.dot(hn_ref[...], w_ref[...], preferred_element_type=F32).astype(o_ref.dtype)


def norm_matmul(x, gain, w, out_dtype, tm=1024, tn=1024):
    m, d = x.shape
    n = w.shape[1]
    tn = min(tn, n)
    while n % tn:
        tn //= 2
    assert m % tm == 0 and tn % 128 == 0
    return pl.pallas_call(
        _norm_matmul_body,
        grid=(m // tm, n // tn),
        in_specs=[
            pl.BlockSpec((tm, d), lambda i, j: (i, 0)),
            pl.BlockSpec((1, d), lambda i, j: (0, 0)),
            pl.BlockSpec((d, tn), lambda i, j: (0, j)),
        ],
        out_specs=pl.BlockSpec((tm, tn), lambda i, j: (i, j)),
        out_shape=jax.ShapeDtypeStruct((m, n), out_dtype),
        scratch_shapes=[pltpu.VMEM((tm, d), BF16)],
        compiler_params=_cparams("parallel", "arbitrary"),
        name="norm_matmul",
    )(x, gain.reshape(1, d), w)


def _matmul_residual_body(x_ref, y_ref, w_ref, o_ref):
    o_ref[...] = x_ref[...] + jnp.dot(y_ref[...], w_ref[...], preferred_element_type=F32)


def matmul_residual(x, y, w, tm=1024):
    m, d = x.shape
    k = y.shape[1]
    return pl.pallas_call(
        _matmul_residual_body,
        grid=(m // tm,),
        in_specs=[
            pl.BlockSpec((tm, d), lambda i: (i, 0)),
            pl.BlockSpec((tm, k), lambda i: (i, 0)),
            pl.BlockSpec((k, d), lambda i: (0, 0)),
        ],
        out_specs=pl.BlockSpec((tm, d), lambda i: (i, 0)),
        out_shape=jax.ShapeDtypeStruct((m, d), F32),
        compiler_params=_cparams("parallel"),
        name="matmul_residual",
    )(x, y, w)


def _final_norm_body(x_ref, g_ref, o_ref):
    x = x_ref[...]
    ms = jnp.mean(x * x, axis=-1, keepdims=True)
    o_ref[...] = x * lax.rsqrt(ms + NORM_EPS) * g_ref[...]


def final_norm(x, gain, tm=1024):
    m, d = x.shape
    return pl.pallas_call(
        _final_norm_body,
        grid=(m // tm,),
        in_specs=[pl.BlockSpec((tm, d), lambda i: (i, 0)), pl.BlockSpec((1, d), lambda i: (0, 0))],
        out_specs=pl.BlockSpec((tm, d), lambda i: (i, 0)),
        out_shape=jax.ShapeDtypeStruct((m, d), F32),
        compiler_params=_cparams("parallel"),
        name="final_norm",
    )(x, gain.reshape(1, d))


def _top16_desc(s, with_rank):
    work = s
    rank = jnp.full(s.shape, 99.0, F32) if with_rank else None
    vals = []
    for r in range(PEER_TOPK):
        m = jnp.max(work, axis=0, keepdims=True)
        eq = work == m
        if with_rank:
            rank = jnp.where(eq, float(r + 1), rank)
        work = jnp.where(eq, -jnp.inf, work)
        vals.append(m)
    return jnp.concatenate(vals, axis=0), rank


def _peer_route_body(x_ref, g_ref, wqT_ref, keys_ref, hnT_ref, n_ref, e0_ref, r1_ref, e1_ref):
    x = x_ref[...]
    ms = jnp.mean(x * x, axis=-1, keepdims=True)
    hn = x * lax.rsqrt(ms + NORM_EPS) * g_ref[...]
    hnT = hn.T.astype(BF16)
    hnT_ref[...] = hnT
    qT = jnp.dot(wqT_ref[...], hnT, preferred_element_type=F32)
    row8 = lax.broadcasted_iota(jnp.int32, (8, x.shape[0]), 0)
    for h in range(PEER_HEADS):
        s = []
        for p in range(2):
            hp = 2 * h + p
            q_hp = qT[hp * PEER_HALF:(hp + 1) * PEER_HALF, :].astype(BF16)
            s.append(jnp.dot(keys_ref[hp], q_hp, preferred_element_type=F32))
        a0, _ = _top16_desc(s[0], False)
        b, rank1 = _top16_desc(s[1], True)
        cand = [a0[0:1, :] + b]
        for p in range(1, PEER_TOPK):
            cnt = PEER_TOPK // (p + 1)
            c = a0[p:p + 1, :] + b[0:8, :]
            cand.append(c if cnt >= 8 else jnp.where(row8 < cnt, c, -jnp.inf))
        best, _ = _top16_desc(jnp.concatenate(cand, axis=0), False)
        tau = best[PEER_TOPK - 1:PEER_TOPK, :]
        z = jnp.sum(jnp.exp(best - best[0:1, :]), axis=0, keepdims=True)
        n_map = jnp.zeros_like(s[0])
        for q in range(PEER_TOPK):
            n_map = n_map + jnp.where(s[0] + b[q:q + 1, :] >= tau, 1.0, 0.0)
        n_ref[h] = jnp.where(s[0] >= a0[PEER_TOPK - 1:PEER_TOPK, :], n_map, 0.0)
        e0_ref[h] = jnp.exp(s[0] - a0[0:1, :]) * (0.5 / z)
        r1_ref[h] = rank1.astype(BF16)
        e1_ref[h] = jnp.exp(s[1] - b[0:1, :]).astype(BF16)


def peer_route(x, gain, wqT, keys, tm=256):
    m, d = x.shape
    nq = wqT.shape[0]
    tab = jax.ShapeDtypeStruct((PEER_HEADS, PEER_NKEYS, m), F32)
    tab16 = jax.ShapeDtypeStruct((PEER_HEADS, PEER_NKEYS, m), BF16)
    tab_spec = pl.BlockSpec((PEER_HEADS, PEER_NKEYS, tm), lambda i: (0, 0, i))
    return pl.pallas_call(
        _peer_route_body,
        grid=(m // tm,),
        in_specs=[
            pl.BlockSpec((tm, d), lambda i: (i, 0)),
            pl.BlockSpec((1, d), lambda i: (0, 0)),
            pl.BlockSpec((nq, d), lambda i: (0, 0)),
            pl.BlockSpec(keys.shape, lambda i: (0, 0, 0)),
        ],
        out_specs=[pl.BlockSpec((d, tm), lambda i: (0, i)), tab_spec, tab_spec, tab_spec, tab_spec],
        out_shape=[jax.ShapeDtypeStruct((d, m), BF16), tab, tab, tab16, tab16],
        compiler_params=_cparams("parallel"),
        name="peer_route",
    )(x, gain.reshape(1, d), wqT, keys)


def _peer_expert_body(x_ref, hnT_ref, n_ref, e0_ref, r1_ref, e1_ref, u_ref, vT_ref, o_ref,
                      z_s, a_s, acc_ref, *, ni, nb):
    j = pl.program_id(1)

    @pl.when(j == 0)
    def _():
        z_s[...] = jnp.zeros_like(z_s)
        a_s[...] = jnp.zeros_like(a_s)
        acc_ref[...] = jnp.zeros_like(acc_ref)

    ib = jnp.clip(j - 1, 0, nb - 1)
    tm = z_s.shape[2]
    pk = BF16_SUBLANES
    nk = PEER_NKEYS
    cur = j % 2
    prev = 1 - cur

    z_s[cur] = jnp.dot(u_ref[...], hnT_ref[...], preferred_element_type=F32)
    acc_ref[...] += jnp.dot(vT_ref[0], a_s[cur], preferred_element_type=F32)
    for ii in range(ni):
        i = ib * ni + ii
        rows = pl.ds(ii * nk, nk)
        z = z_s[prev, rows, :]
        act = (z * (1.0 + lax.erf(z * (1.0 / math.sqrt(2.0))))).astype(BF16)
        gate = None
        for h in range(PEER_HEADS):
            n_b = jnp.broadcast_to(n_ref[h, pl.ds(i, 1), :], (pk, tm)).astype(BF16)[None]
            e0_b = jnp.broadcast_to(e0_ref[h, pl.ds(i, 1), :], (pk, tm)).astype(BF16)[None]
            r1 = r1_ref[h].reshape(nk // pk, pk, tm)
            e1 = e1_ref[h].reshape(nk // pk, pk, tm)
            term = jnp.where(r1 <= n_b, e1 * e0_b, jnp.zeros((), BF16))
            gate = term if gate is None else gate + term
        a_s[prev, rows, :] = (act.reshape(nk // pk, pk, tm) * gate).reshape(nk, tm)

    @pl.when(j == nb + 1)
    def _():
        o_ref[...] = x_ref[...] + acc_ref[...].T


def peer_experts(x, hnT, n_tab, e0_tab, r1_tab, e1_tab, u, v_tab, tm=512, ni=PEER_EXPERT_ROWS):
    m, d = x.shape
    ne = u.shape[0]
    et = ni * PEER_NKEYS
    nb = ne // et
    vT = jnp.transpose(v_tab.astype(BF16).reshape(nb, et, d), (0, 2, 1))
    tab_spec = pl.BlockSpec((PEER_HEADS, PEER_NKEYS, tm), lambda i, j: (0, 0, i))
    return pl.pallas_call(
        functools.partial(_peer_expert_body, ni=ni, nb=nb),
        grid=(m // tm, nb + 2),
        in_specs=[
            pl.BlockSpec((tm, d), lambda i, j: (i, 0)),
            pl.BlockSpec((d, tm), lambda i, j: (0, i)),
            tab_spec, tab_spec, tab_spec, tab_spec,
            pl.BlockSpec((et, d), lambda i, j: (jnp.minimum(j, nb - 1), 0)),
            pl.BlockSpec((1, d, et), lambda i, j: (jnp.clip(j - 2, 0, nb - 1), 0, 0)),
        ],
        out_specs=pl.BlockSpec((tm, d), lambda i, j: (i, 0)),
        out_shape=jax.ShapeDtypeStruct((m, d), F32),
        scratch_shapes=[pltpu.VMEM((2, et, tm), F32), pltpu.VMEM((2, et, tm), BF16), pltpu.VMEM((d, tm), F32)],
        compiler_params=_cparams("parallel", "arbitrary"),
        name="peer_experts",
    )(x, hnT, n_tab, e0_tab, r1_tab, e1_tab, u, vT)


def peer_layer(x, gain, w_q, keys, u_tab, v_tab):
    wqT = w_q.T.astype(BF16)
    keys2 = keys.reshape(PEER_HEADS * 2, PEER_NKEYS, PEER_HALF).astype(BF16)
    hnT, n_tab, e0_tab, r1_tab, e1_tab = peer_route(x, gain, wqT, keys2)
    return peer_experts(x, hnT, n_tab, e0_tab, r1_tab, e1_tab, u_tab.astype(BF16), v_tab)


def _rope_lanes(t, cos, sin_signed):
    half = SWA_HEADDIM // 2
    lane = lax.broadcasted_iota(jnp.int32, t.shape, 1)
    first = (lane % SWA_HEADDIM) < half
    partner = jnp.where(first, pltpu.roll(t, 128 - half, 1), pltpu.roll(t, half, 1))
    return t * cos + partner * sin_signed


def _swa_body(q_ref, k_ref, v_ref, cos_ref, sin_ref, sink_ref, o_ref, kp_ref, vp_ref):
    w = SWA_WINDOW
    t_len = q_ref.shape[1]
    nkv = SWA_KV_HEADS * SWA_HEADDIM
    rep = SWA_HEADS // SWA_KV_HEADS
    zeros = jnp.zeros((w, nkv), BF16)
    kp_ref[pl.ds(0, w), :] = zeros
    kp_ref[pl.ds(w + t_len, w), :] = zeros
    vp_ref[pl.ds(0, w), :] = zeros
    vp_ref[pl.ds(w + t_len, w), :] = zeros
    vp_ref[pl.ds(w, t_len), :] = v_ref[0]
    cos_all = cos_ref[...]
    sin_all = sin_ref[...]
    for c in range(nkv // 128):
        kc = k_ref[0, :, c * 128:(c + 1) * 128].astype(F32)
        kp_ref[pl.ds(w, t_len), c * 128:(c + 1) * 128] = _rope_lanes(kc, cos_all, sin_all).astype(BF16)

    row = lax.broadcasted_iota(jnp.int32, (w, 3 * w), 0)
    col = lax.broadcasted_iota(jnp.int32, (w, 3 * w), 1)
    band = (col >= row) & (col <= row + 2 * w)

    def block(n, carry):
        base = pl.multiple_of(n * w, w)
        cos_b = cos_ref[pl.ds(base, w), :]
        sin_b = sin_ref[pl.ds(base, w), :]
        kpos = base - w + col
        valid = band & (kpos >= 0) & (kpos < t_len)
        outs = []
        for g in range(SWA_KV_HEADS):
            kw = kp_ref[pl.ds(base, 3 * w), g * SWA_HEADDIM:(g + 1) * SWA_HEADDIM]
            vw = vp_ref[pl.ds(base, 3 * w), g * SWA_HEADDIM:(g + 1) * SWA_HEADDIM]
            scores = []
            for c in range(g * rep // 2, (g + 1) * rep // 2):
                qc = q_ref[0, pl.ds(base, w), c * 128:(c + 1) * 128].astype(F32)
                qc = (_rope_lanes(qc, cos_b, sin_b) * (SWA_HEADDIM ** -0.5)).astype(BF16)
                for hh in range(2):
                    qh = qc[:, hh * SWA_HEADDIM:(hh + 1) * SWA_HEADDIM]
                    scores.append(lax.dot_general(qh, kw, (((1,), (1,)), ((), ())), preferred_element_type=F32))
            probs = []
            for k, s in enumerate(scores):
                s = jnp.where(valid, s, NEG)
                sk = sink_ref[g * rep + k]
                m = jnp.maximum(jnp.max(s, axis=-1, keepdims=True), sk)
                p = jnp.exp(s - m)
                inv = 1.0 / (jnp.sum(p, axis=-1, keepdims=True) + jnp.exp(sk - m))
                probs.append((p.astype(BF16), inv))
            for p, inv in probs:
                outs.append(jnp.dot(p, vw, preferred_element_type=F32) * inv)
        o_ref[0, pl.ds(base, w), :] = jnp.concatenate(outs, axis=-1).astype(o_ref.dtype)
        return carry

    lax.fori_loop(0, t_len // w, block, 0)


def swa_attention(proj, sink, bsz, t_len):
    half = SWA_HEADDIM // 2
    inv_freq = ROPE_THETA ** (-jnp.arange(half, dtype=F32) / half)
    ang = jnp.arange(t_len, dtype=F32)[:, None] * inv_freq[None, :]
    cos, sin = jnp.cos(ang), jnp.sin(ang)
    cos_t = jnp.tile(jnp.concatenate([cos, cos], axis=-1), (1, 2))
    sin_t = jnp.tile(jnp.concatenate([-sin, sin], axis=-1), (1, 2))
    nq = SWA_HEADS * SWA_HEADDIM
    nkv = SWA_KV_HEADS * SWA_HEADDIM
    return pl.pallas_call(
        _swa_body,
        grid=(bsz,),
        in_specs=[
            pl.BlockSpec((1, t_len, nq), lambda b: (b, 0, 0)),
            pl.BlockSpec((1, t_len, nkv), lambda b: (b, 0, nq // nkv)),
            pl.BlockSpec((1, t_len, nkv), lambda b: (b, 0, nq // nkv + 1)),
            pl.BlockSpec((t_len, 128), lambda b: (0, 0)),
            pl.BlockSpec((t_len, 128), lambda b: (0, 0)),
            pl.BlockSpec(memory_space=pltpu.SMEM),
        ],
        out_specs=pl.BlockSpec((1, t_len, nq), lambda b: (b, 0, 0)),
        out_shape=jax.ShapeDtypeStruct((bsz, t_len, nq), BF16),
        scratch_shapes=[pltpu.VMEM((t_len + 2 * SWA_WINDOW, nkv), BF16),
                        pltpu.VMEM((t_len + 2 * SWA_WINDOW, nkv), BF16)],
        compiler_params=_cparams("parallel"),
        name="swa_attention",
    )(proj, proj, proj, cos_t, sin_t, sink.astype(F32))


def _na_bias_table(rpb):
    qc = np.arange(GRID_W)[:, None]
    kc = np.arange(GRID_W)[None, :]
    cstart = np.clip(qc - NA_COLS // 2, 0, GRID_W - NA_COLS)
    valid = (kc >= cstart) & (kc < cstart + NA_COLS)
    cidx = np.clip(kc - qc + NA_COLS - 1, 0, 2 * NA_COLS - 2)
    onehot = (np.arange(2 * NA_COLS - 1)[:, None, None] == cidx[None]).astype(np.float32)
    toep = jnp.einsum('hrc,cqk->hrqk', rpb.astype(F32), onehot, precision=lax.Precision.HIGHEST)
    toep = jnp.where(valid[None, None], toep, NEG)
    return jnp.concatenate([toep[:, :-1], toep[:, 1:]], axis=-1)


def _na_body(q_ref, k_ref, v_ref, bias_ref, o_ref):
    t_len = q_ref.shape[1]
    rows = t_len // GRID_W
    win = NA_ROWS * GRID_W

    def row_group(gi, carry):
        chains = []
        for rr in range(NA_ROW_UNROLL):
            r = gi * NA_ROW_UNROLL + rr
            rs = jnp.clip(r - NA_ROWS // 2, 0, rows - NA_ROWS)
            d0 = rs - r + NA_ROWS - 1
            qbase = pl.multiple_of(r * GRID_W, GRID_W)
            kbase = pl.multiple_of(rs * GRID_W, GRID_W)
            qr = q_ref[0, pl.ds(qbase, GRID_W), :]
            kw = k_ref[0, pl.ds(kbase, win), :]
            for hh in range(2):
                sl = slice(hh * NA_HEADDIM, (hh + 1) * NA_HEADDIM)
                s = lax.dot_general(qr[:, sl], kw[:, sl], (((1,), (1,)), ((), ())), preferred_element_type=F32)
                chains.append((hh, d0, kbase, s))
        probs = []
        for hh, d0, kbase, s in chains:
            bias = jnp.concatenate([bias_ref[hh, d0 + 2 * c] for c in range(NA_ROWS // 2)], axis=-1)
            s = s * (NA_HEADDIM ** -0.5) + bias
            m = jnp.max(s, axis=-1, keepdims=True)
            p = jnp.exp(s - m)
            probs.append((p.astype(BF16), 1.0 / jnp.sum(p, axis=-1, keepdims=True)))
        outs = []
        for (hh, d0, kbase, s), (p, inv) in zip(chains, probs):
            vw = v_ref[0, pl.ds(kbase, win), hh * NA_HEADDIM:(hh + 1) * NA_HEADDIM]
            outs.append(jnp.dot(p, vw, preferred_element_type=F32) * inv)
        for rr in range(NA_ROW_UNROLL):
            qbase = pl.multiple_of((gi * NA_ROW_UNROLL + rr) * GRID_W, GRID_W)
            o_ref[0, pl.ds(qbase, GRID_W), :] = jnp.concatenate(outs[2 * rr:2 * rr + 2], axis=-1).astype(o_ref.dtype)
        return carry

    lax.fori_loop(0, rows // NA_ROW_UNROLL, row_group, 0)


def na_attention(proj, rpb, bsz, t_len):
    bias = _na_bias_table(rpb)
    npair = NA_HEADS // 2
    return pl.pallas_call(
        _na_body,
        grid=(npair, bsz),
        in_specs=[
            pl.BlockSpec((1, t_len, 128), lambda hp, b: (b, 0, hp)),
            pl.BlockSpec((1, t_len, 128), lambda hp, b: (b, 0, npair + hp)),
            pl.BlockSpec((1, t_len, 128), lambda hp, b: (b, 0, 2 * npair + hp)),
            pl.BlockSpec((2, 2 * NA_ROWS - 2, GRID_W, 2 * GRID_W), lambda hp, b: (hp, 0, 0, 0)),
        ],
        out_specs=pl.BlockSpec((1, t_len, 128), lambda hp, b: (b, 0, hp)),
        out_shape=jax.ShapeDtypeStruct((bsz, t_len, NA_WIDTH), BF16),
        compiler_params=_cparams("parallel", "parallel"),
        name="na_attention",
    )(proj, proj, proj, bias)


def _conv_silu_body(x_ref, w_ref, b_ref, o_ref, *, n_l2):
    t_len, tc = x_ref.shape[1], x_ref.shape[2]
    rc = CONV_ROW_CHUNK
    halo = F32_SUBLANES
    normalise = pl.program_id(1) < n_l2

    def chunk(ci, carry):
        r0 = pl.multiple_of(ci * rc, rc)
        cur = x_ref[0, pl.ds(r0, rc), :].astype(F32)
        lo = pl.multiple_of(jnp.maximum(r0 - halo, 0), halo)
        hi = pl.multiple_of(jnp.minimum(r0 + rc, t_len - halo), halo)
        before = jnp.where(r0 > 0, x_ref[0, pl.ds(lo, halo), :].astype(F32), 0.0)
        after = jnp.where(r0 + rc < t_len, x_ref[0, pl.ds(hi, halo), :].astype(F32), 0.0)
        xx = jnp.concatenate([before, cur, after], axis=0)
        acc = cur * w_ref[CONV_K // 2:CONV_K // 2 + 1, :] + b_ref[...]
        for k in range(CONV_K):
            off = k - CONV_K // 2
            if off == 0:
                continue
            acc = acc + xx[halo + off:halo + off + rc, :] * w_ref[k:k + 1, :]
        y = acc * jax.nn.sigmoid(acc)
        if n_l2 > 0:
            parts = []
            for c in range(tc // 128):
                yc = y[:, c * 128:(c + 1) * 128]
                ss = jnp.sum(yc * yc, axis=-1, keepdims=True)
                parts.append(yc * jnp.where(normalise, lax.rsqrt(ss + NORM_EPS), 1.0))
            y = jnp.concatenate(parts, axis=-1)
        o_ref[0, pl.ds(r0, rc), :] = y.astype(o_ref.dtype)
        return carry

    lax.fori_loop(0, t_len // rc, chunk, 0)


def conv_silu(proj, w, bias, col0, n_ch, n_l2=0, tc=512):
    bsz, t_len, _ = proj.shape
    assert col0 % tc == 0 and n_ch % tc == 0
    c0 = col0 // tc
    return pl.pallas_call(
        functools.partial(_conv_silu_body, n_l2=n_l2),
        grid=(bsz, n_ch // tc),
        in_specs=[
            pl.BlockSpec((1, t_len, tc), lambda b, j: (b, 0, c0 + j)),
            pl.BlockSpec((CONV_K, tc), lambda b, j: (0, j)),
            pl.BlockSpec((1, tc), lambda b, j: (0, j)),
        ],
        out_specs=pl.BlockSpec((1, t_len, tc), lambda b, j: (b, 0, j)),
        out_shape=jax.ShapeDtypeStruct((bsz, t_len, n_ch), BF16),
        compiler_params=_cparams("parallel", "parallel"),
        name="conv_silu",
    )(proj, w.astype(F32), bias.astype(F32).reshape(1, n_ch))


def _softplus(x):
    return jnp.maximum(x, 0.0) + jnp.log1p(jnp.exp(-jnp.abs(x)))


def _bmm(a, b, precision=None):
    return lax.dot_general(a, b, (((2,), (1,)), ((0,), (0,))), precision=precision,
                           preferred_element_type=F32)


def _split_bf16(a):
    hi = a.astype(BF16)
    return hi, (a - hi.astype(F32)).astype(BF16)


def _bmm_split(a, b):
    return _bmm(a[0], b[0]) + _bmm(a[0], b[1]) + _bmm(a[1], b[0])


def _bmm_nt(a, b):
    return lax.dot_general(a, b, (((2,), (2,)), ((0,), (0,))), preferred_element_type=F32)


def _gdn_body(q_ref, k_ref, v_ref, z_ref, gcol_ref, grow_ref, alog_ref, dtb_ref, nw_ref, o_ref,
              u_s, w_s, qk_s, qg_s, kg_s, el_s, o_s):
    c_len = GDN_CHUNK
    t_len = q_ref.shape[1]
    nc = t_len // c_len
    h = pl.program_id(1)
    ii = lax.broadcasted_iota(jnp.int32, (c_len, c_len), 0)
    jj = lax.broadcasted_iota(jnp.int32, (c_len, c_len), 1)
    eye = (ii == jj).astype(F32)
    cg = GDN_PHASE_A_CHUNKS

    def phase_a(gi, carry):
        c0 = pl.multiple_of(gi * cg, cg)
        rows = pl.ds(pl.multiple_of(gi * (cg * c_len), cg * c_len), cg * c_len)
        k3 = k_ref[0, rows, :].reshape(cg, c_len, GDN_DK)
        v3 = v_ref[0, rows, :].reshape(cg, c_len, GDN_DK).astype(F32)
        kf = k3.astype(F32)
        qs = q_ref[0, rows, :].reshape(cg, c_len, GDN_DK).astype(F32) * (GDN_DK ** -0.5)
        gcol = gcol_ref[0, 0, pl.ds(c0, cg)]
        grow = grow_ref[0, 0, pl.ds(c0, cg)]
        qk_raw = _bmm_nt(qs.astype(BF16), k3)
        for d in range(2):
            incl = (ii >= jj) if d == 0 else (ii <= jj)
            strict = (ii > jj) if d == 0 else (ii < jj)
            tri = incl.astype(F32)
            tri_t = ((ii <= jj) if d == 0 else (ii >= jj)).astype(F32)
            neg_a = -jnp.exp(jnp.full((1, 1, 1), alog_ref[d, h], F32))
            dtb = dtb_ref[d, h]
            g_c = neg_a * _softplus(gcol[:, :, d:d + 1] + dtb)
            g_r = neg_a * _softplus(grow[:, d:d + 1, :] + dtb)
            beta_c = jax.nn.sigmoid(gcol[:, :, 2 + d:3 + d])
            gc_c = jnp.sum(tri[None] * g_r, axis=2, keepdims=True)
            gc_r = jnp.sum(tri_t[None] * g_c, axis=1, keepdims=True)
            g_last = jnp.sum(g_r, axis=2, keepdims=True)
            decay = jnp.where(incl[None], jnp.exp(jnp.where(incl[None], gc_c - gc_r, 0.0)), 0.0)
            kb = kf * beta_c
            kk = _bmm_nt(kb.astype(BF16), k3)
            low = jnp.where(strict[None], kk * decay, 0.0)
            inv = eye[None] - low
            pw = _split_bf16(low)
            for _ in range(5):
                pw = _split_bf16(_bmm_split(pw, pw))
                inv = inv + _bmm_split(_split_bf16(inv), pw)
            rhs = jnp.concatenate([v3 * beta_c, kb * jnp.exp(gc_c)], axis=-1)
            sol = _bmm_split(_split_bf16(inv), _split_bf16(rhs))
            u_s[d, pl.ds(c0, cg)] = sol[:, :, :GDN_DK]
            w_s[d, pl.ds(c0, cg)] = sol[:, :, GDN_DK:].astype(BF16)
            qk_s[d, pl.ds(c0, cg)] = (qk_raw * decay).astype(BF16)
            qg_s[d, pl.ds(c0, cg)] = (qs * jnp.exp(gc_c)).astype(BF16)
            kg_s[d, pl.ds(c0, cg)] = (kf * jnp.exp(g_last - gc_c)).astype(BF16)
            el_s[d, pl.ds(c0, cg)] = jnp.broadcast_to(jnp.exp(g_last), (cg, 1, GDN_DK))
        return carry

    lax.fori_loop(0, nc // cg, phase_a, 0)

    def step(t, carry):
        cs = (t, nc - 1 - t)
        sb = [carry[d].astype(BF16) for d in range(2)]
        ws = [jnp.dot(w_s[d, cs[d]], sb[d], preferred_element_type=F32) for d in range(2)]
        qs_ = [jnp.dot(qg_s[d, cs[d]], sb[d], preferred_element_type=F32) for d in range(2)]
        vb = [(u_s[d, cs[d]] - ws[d]).astype(BF16) for d in range(2)]
        os_ = [qs_[d] + jnp.dot(qk_s[d, cs[d]], vb[d], preferred_element_type=F32) for d in range(2)]
        kv = [lax.dot_general(kg_s[d, cs[d]], vb[d], (((0,), (0,)), ((), ())), preferred_element_type=F32)
              for d in range(2)]
        for d in range(2):
            o_s[d, cs[d]] = os_[d]
        return tuple(carry[d] * el_s[d, cs[d]] + kv[d] for d in range(2))

    s0 = jnp.zeros((GDN_DK, GDN_DK), F32)
    lax.fori_loop(0, nc, step, (s0, s0))

    o = (o_s[0] + o_s[1]).reshape(t_len, GDN_DK)
    ms = jnp.mean(o * o, axis=-1, keepdims=True)
    z = z_ref[0]
    y = o * lax.rsqrt(ms + NORM_EPS) * nw_ref[...] * (z * jax.nn.sigmoid(z))
    o_ref[0] = y.astype(o_ref.dtype)


def gdn_scan(qkv, proj, gates, a_log, dt_bias, norm_w, bsz, t_len):
    nc = t_len // GDN_CHUNK
    nh = GDN_HEADS
    g4 = gates[:, :4 * nh].reshape(bsz, nc, GDN_CHUNK, 4, nh)
    gcol = jnp.transpose(g4, (0, 4, 1, 2, 3))
    grow = jnp.transpose(g4, (0, 4, 1, 3, 2))
    dk = GDN_DK
    nq = GDN_QK // dk
    return pl.pallas_call(
        _gdn_body,
        grid=(bsz, nh),
        in_specs=[
            pl.BlockSpec((1, t_len, dk), lambda b, h: (b, 0, h)),
            pl.BlockSpec((1, t_len, dk), lambda b, h: (b, 0, nq + h)),
            pl.BlockSpec((1, t_len, dk), lambda b, h: (b, 0, 2 * nq + h)),
            pl.BlockSpec((1, t_len, dk), lambda b, h: (b, 0, 3 * nq + h)),
            pl.BlockSpec((1, 1, nc, GDN_CHUNK, 4), lambda b, h: (b, h, 0, 0, 0)),
            pl.BlockSpec((1, 1, nc, 4, GDN_CHUNK), lambda b, h: (b, h, 0, 0, 0)),
            pl.BlockSpec(memory_space=pltpu.SMEM),
            pl.BlockSpec(memory_space=pltpu.SMEM),
            pl.BlockSpec((1, dk), lambda b, h: (0, 0)),
        ],
        out_specs=pl.BlockSpec((1, t_len, dk), lambda b, h: (b, 0, h)),
        out_shape=jax.ShapeDtypeStruct((bsz, t_len, GDN_V), BF16),
        scratch_shapes=[
            pltpu.VMEM((2, nc, GDN_CHUNK, dk), F32),
            pltpu.VMEM((2, nc, GDN_CHUNK, dk), BF16),
            pltpu.VMEM((2, nc, GDN_CHUNK, GDN_CHUNK), BF16),
            pltpu.VMEM((2, nc, GDN_CHUNK, dk), BF16),
            pltpu.VMEM((2, nc, GDN_CHUNK, dk), BF16),
            pltpu.VMEM((2, nc, 1, dk), F32),
            pltpu.VMEM((2, nc, GDN_CHUNK, dk), F32),
        ],
        compiler_params=_cparams("parallel", "parallel"),
        name="gdn_scan",
    )(qkv, qkv, qkv, proj, gcol, grow, a_log.astype(F32), dt_bias.astype(F32),
      norm_w.astype(F32).reshape(1, dk))


def _pad_cols(w, n):
    return jnp.pad(w, ((0, 0), (0, n - w.shape[1])))


def gdn_layer(x, gain, w_in, conv_w, a_log, dt_bias, norm_w, w_out, bsz, t_len):
    n_main = GDN_CONV_CH + GDN_V
    proj = norm_matmul(x, gain, w_in[:, :n_main].astype(BF16), F32)
    gates = norm_matmul(x, gain, _pad_cols(w_in[:, n_main:], 128).astype(BF16), F32)
    proj3 = proj.reshape(bsz, t_len, n_main)
    qkv = conv_silu(proj3, conv_w, jnp.zeros((GDN_CONV_CH,), F32), 0, GDN_CONV_CH,
                    n_l2=2 * GDN_QK // 512)
    y = gdn_scan(qkv, proj3, gates, a_log, dt_bias, norm_w, bsz, t_len)
    return matmul_residual(x, y.reshape(bsz * t_len, GDN_V), w_out.astype(BF16))


def _ssd_body(xs_ref, b_ref, c_ref, z_ref, dcol_ref, drow_ref, alog_ref, dtbc_ref, dtbr_ref, dskip_ref, nw_ref, o_ref,
              xw_s, ea_s, el_s, y_s, st_s):
    c_len = SSD_CHUNK
    t_len = xs_ref.shape[1]
    nc = t_len // c_len
    nr = SSD_HEADS // SSD_GROUPS
    hp = SSD_HEADDIM
    g = pl.program_id(1)
    ii = lax.broadcasted_iota(jnp.int32, (c_len, c_len), 0)
    jj = lax.broadcasted_iota(jnp.int32, (c_len, c_len), 1)
    cg = SSD_PHASE_A_CHUNKS

    def phase_a(gi, carry):
        c0 = pl.multiple_of(gi * cg, cg)
        rows = pl.ds(pl.multiple_of(gi * (cg * c_len), cg * c_len), cg * c_len)
        x3 = xs_ref[0, rows, :].reshape(cg, c_len, nr * hp)
        b3 = b_ref[0, rows, :].reshape(cg, c_len, SSD_STATE)
        c3 = c_ref[0, rows, :].reshape(cg, c_len, SSD_STATE)
        dcol = dcol_ref[0, 0, pl.ds(c0, cg)]
        drow = drow_ref[0, 0, pl.ds(c0, cg)]
        cb = _bmm_nt(c3, b3)
        dt_c_all = _softplus(dcol + dtbc_ref[0])
        dt_r_all = _softplus(drow + dtbr_ref[0])
        ydiag = None
        for d in range(2):
            incl = (ii >= jj) if d == 0 else (ii <= jj)
            tri = incl.astype(F32)
            tri_t = ((ii <= jj) if d == 0 else (ii >= jj)).astype(F32)
            yd, xw, ea, el = [], [], [], []
            for r in range(nr):
                hidx = g * nr + r
                a = -jnp.exp(jnp.full((1, 1, 1), alog_ref[d, hidx], F32))
                col = d * nr + r
                dt_c = dt_c_all[:, :, col:col + 1]
                dt_r = dt_r_all[:, col:col + 1, :]
                ac_c = jnp.sum(tri[None] * (dt_r * a), axis=2, keepdims=True)
                ac_r = jnp.sum(tri_t[None] * (dt_c * a), axis=1, keepdims=True)
                a_last = jnp.sum(dt_r * a, axis=2, keepdims=True)
                seg = jnp.where(incl[None], jnp.exp(jnp.where(incl[None], ac_c - ac_r, 0.0)), 0.0)
                xr = x3[:, :, r * hp:(r + 1) * hp].astype(F32) * dt_c
                yd.append(_bmm((cb * seg).astype(BF16), xr.astype(BF16)))
                xw.append((xr * jnp.exp(a_last - ac_c)).astype(BF16))
                ea.append(jnp.broadcast_to(jnp.exp(ac_c), (cg, c_len, hp)))
                el.append(jnp.broadcast_to(jnp.exp(a_last), (cg, 1, hp)))
            yd = jnp.concatenate(yd, axis=-1)
            ydiag = yd if ydiag is None else ydiag + yd
            xw_s[d, pl.ds(c0, cg)] = jnp.concatenate(xw, axis=-1)
            ea_s[d, pl.ds(c0, cg)] = jnp.concatenate(ea, axis=-1)
            el_s[d, pl.ds(c0, cg)] = jnp.concatenate(el, axis=-1)
        y_s[pl.ds(c0, cg)] = ydiag
        return carry

    lax.fori_loop(0, nc // cg, phase_a, 0)

    st_s[...] = jnp.zeros_like(st_s)

    def step(t, carry):
        for d in range(2):
            c = t if d == 0 else nc - 1 - t
            base = pl.multiple_of(c * c_len, c_len)
            cc = c_ref[0, pl.ds(base, c_len), :]
            bb = b_ref[0, pl.ds(base, c_len), :]
            st = st_s[d]
            y_s[c] += jnp.dot(cc, st.astype(BF16), preferred_element_type=F32) * ea_s[d, c]
            st_s[d] = st * el_s[d, c] + lax.dot_general(bb, xw_s[d, c], (((0,), (0,)), ((), ())),
                                                        preferred_element_type=F32)
        return carry

    lax.fori_loop(0, nc, step, 0)

    xs = xs_ref[0].astype(F32)
    z = z_ref[0]
    y = (y_s[...].reshape(t_len, nr * hp) + dskip_ref[...] * xs) * (z * jax.nn.sigmoid(z))
    ms = jnp.mean(y * y, axis=-1, keepdims=True)
    o_ref[0] = (y * lax.rsqrt(ms + NORM_EPS) * nw_ref[...]).astype(o_ref.dtype)


def ssd_scan(xbc, proj, dts, a_log, dt_bias, d_skip, norm_w, bsz, t_len):
    nc = t_len // SSD_CHUNK
    ng = SSD_GROUPS
    nr = SSD_HEADS // ng
    gw = nr * SSD_HEADDIM
    d6 = dts[:, :2 * SSD_HEADS].reshape(bsz, nc, SSD_CHUNK, 2, ng, nr)
    dcol = jnp.transpose(d6, (0, 4, 1, 2, 3, 5)).reshape(bsz, ng, nc, SSD_CHUNK, 2 * nr)
    drow = jnp.transpose(d6, (0, 4, 1, 3, 5, 2)).reshape(bsz, ng, nc, 2 * nr, SSD_CHUNK)
    nb0 = SSD_INNER // SSD_STATE
    dtb_g = jnp.transpose(dt_bias.astype(F32).reshape(2, ng, nr), (1, 0, 2)).reshape(ng, 2 * nr)
    return pl.pallas_call(
        _ssd_body,
        grid=(bsz, ng),
        in_specs=[
            pl.BlockSpec((1, t_len, gw), lambda b, g: (b, 0, g)),
            pl.BlockSpec((1, t_len, SSD_STATE), lambda b, g: (b, 0, nb0 + g)),
            pl.BlockSpec((1, t_len, SSD_STATE), lambda b, g: (b, 0, nb0 + ng + g)),
            pl.BlockSpec((1, t_len, gw), lambda b, g: (b, 0, g)),
            pl.BlockSpec((1, 1, nc, SSD_CHUNK, 2 * nr), lambda b, g: (b, g, 0, 0, 0)),
            pl.BlockSpec((1, 1, nc, 2 * nr, SSD_CHUNK), lambda b, g: (b, g, 0, 0, 0)),
            pl.BlockSpec(memory_space=pltpu.SMEM),
            pl.BlockSpec((1, 1, 2 * nr), lambda b, g: (g, 0, 0)),
            pl.BlockSpec((1, 2 * nr, 1), lambda b, g: (g, 0, 0)),
            pl.BlockSpec((1, gw), lambda b, g: (0, g)),
            pl.BlockSpec((1, gw), lambda b, g: (0, g)),
        ],
        out_specs=pl.BlockSpec((1, t_len, gw), lambda b, g: (b, 0, g)),
        out_shape=jax.ShapeDtypeStruct((bsz, t_len, SSD_INNER), BF16),
        scratch_shapes=[
            pltpu.VMEM((2, nc, SSD_CHUNK, gw), BF16),
            pltpu.VMEM((2, nc, SSD_CHUNK, gw), F32),
            pltpu.VMEM((2, nc, 1, gw), F32),
            pltpu.VMEM((nc, SSD_CHUNK, gw), F32),
            pltpu.VMEM((2, SSD_STATE, gw), F32),
        ],
        compiler_params=_cparams("parallel", "parallel"),
        name="ssd_scan",
    )(xbc, xbc, xbc, proj, dcol, drow, a_log.astype(F32), dtb_g.reshape(ng, 1, 2 * nr),
      dtb_g.reshape(ng, 2 * nr, 1),
      jnp.repeat(d_skip.astype(F32), SSD_HEADDIM).reshape(1, SSD_INNER),
      norm_w.astype(F32).reshape(1, SSD_INNER))


def ssd_layer(x, gain, w_in, conv_w, conv_b, a_log, dt_bias, d_skip, norm_w, w_out, bsz, t_len):
    n_main = SSD_INNER + SSD_CONV_CH
    proj = norm_matmul(x, gain, w_in[:, :n_main].astype(BF16), F32)
    dts = norm_matmul(x, gain, _pad_cols(w_in[:, n_main:], 128).astype(BF16), F32)
    proj3 = proj.reshape(bsz, t_len, n_main)
    xbc = conv_silu(proj3, conv_w, conv_b, SSD_INNER, SSD_CONV_CH)
    y = ssd_scan(xbc, proj3, dts, a_log, dt_bias, d_skip, norm_w, bsz, t_len)
    return matmul_residual(x, y.reshape(bsz * t_len, SSD_INNER), w_out.astype(BF16))


def swa_layer(x, gain, w_in, sink, w_out, bsz, t_len):
    proj = norm_matmul(x, gain, w_in.astype(BF16), BF16)
    o = swa_attention(proj.reshape(bsz, t_len, -1), sink, bsz, t_len)
    return matmul_residual(x, o.reshape(bsz * t_len, -1), w_out.astype(BF16))


def na_layer(x, gain, w_in, rpb, w_out, bsz, t_len):
    proj = norm_matmul(x, gain, w_in.astype(BF16), BF16)
    o = na_attention(proj.reshape(bsz, t_len, -1), rpb, bsz, t_len)
    return matmul_residual(x, o.reshape(bsz * t_len, -1), w_out.astype(BF16))


def kernel(x, norm_mix, norm_ffn, norm_final, gdn_w_in, gdn_conv, gdn_a_log, gdn_dt_bias, gdn_norm, gdn_w_out, ssd_w_in, ssd_conv, ssd_conv_b, ssd_a_log, ssd_dt_bias, ssd_d, ssd_norm, ssd_w_out, swa_w_in, swa_sink, swa_w_out, na_w_in, na_rpb, na_w_out, peer_w_q, peer_keys, peer_u, peer_v):
    bsz, t_len, d = x.shape
    depth = norm_mix.shape[0]
    xf = x.reshape(bsz * t_len, d)
    for i in range(depth):
        mixer, j = i % 4, i // 4
        if mixer == 0:
            xf = gdn_layer(xf, norm_mix[i], gdn_w_in[j], gdn_conv[j], gdn_a_log[j], gdn_dt_bias[j],
                           gdn_norm[j], gdn_w_out[j], bsz, t_len)
        elif mixer == 1:
            xf = ssd_layer(xf, norm_mix[i], ssd_w_in[j], ssd_conv[j], ssd_conv_b[j], ssd_a_log[j],
                           ssd_dt_bias[j], ssd_d[j], ssd_norm[j], ssd_w_out[j], bsz, t_len)
        elif mixer == 2:
            xf = swa_layer(xf, norm_mix[i], swa_w_in[j], swa_sink[j], swa_w_out[j], bsz, t_len)
        else:
            xf = na_layer(xf, norm_mix[i], na_w_in[j], na_rpb[j], na_w_out[j], bsz, t_len)
        xf = peer_layer(xf, norm_ffn[i], peer_w_q[i], peer_keys[i], peer_u[i], peer_v[i])
    return final_norm(xf, norm_final).reshape(bsz, t_len, d)
```

```python
import functools
import math

import jax
import jax.numpy as jnp
import numpy as np
from jax import lax
from jax.experimental import pallas as pl
from jax.experimental.pallas import tpu as pltpu

F32 = jnp.float32
BF16 = jnp.bfloat16
NEG = -1e30

D_MODEL = 1024
SEQ = 2048
GRID_W = 64
CONV_K = 5
NORM_EPS = 1e-6

GDN_HEADS = 8
GDN_DK = 128
GDN_CHUNK = 64
GDN_QK = 1024
GDN_V = 1024
GDN_CONV_CH = 3072

SSD_INNER = 2048
SSD_HEADDIM = 64
SSD_HEADS = 32
SSD_GROUPS = 4
SSD_STATE = 128
SSD_CHUNK = 64
SSD_BC = 512
SSD_CONV_CH = 3072

SWA_HEADS = 16
SWA_KV_HEADS = 4
SWA_HEADDIM = 64
SWA_WINDOW = 128
ROPE_THETA = 10000.0

NA_HEADS = 16
NA_HEADDIM = 64
NA_ROWS = 8
NA_COLS = 16
NA_WIDTH = 1024

PEER_HEADS = 8
PEER_NKEYS = 128
PEER_QDIM = 256
PEER_HALF = 128
PEER_TOPK = 16

GDN_PHASE_A_CHUNKS = 8
SSD_PHASE_A_CHUNKS = 4

PEER_EXPERT_ROWS = 8
PEER_Z_CHUNKS = 4

NA_ROW_UNROLL = 4

BF16_SUBLANES = 16
F32_SUBLANES = 8

CONV_ROW_CHUNK = 256

VMEM_LIMIT_BYTES = 52 * 1024 * 1024


def _cparams(*sem, flags=None):
    return pltpu.CompilerParams(dimension_semantics=sem, vmem_limit_bytes=VMEM_LIMIT_BYTES, flags=flags)


def _norm_matmul_body(x_ref, g_ref, w_ref, o_ref, hn_ref):
    @pl.when(pl.program_id(1) == 0)
    def _():
        x = x_ref[...]
        ms = jnp.mean(x * x, axis=-1, keepdims=True)
        hn_ref[...] = (x * lax.rsqrt(ms + NORM_EPS) * g_ref[...]).astype(BF16)

    o_ref[...] = jnp.dot(hn_ref[...], w_ref[...], preferred_element_type=F32).astype(o_ref.dtype)


def norm_matmul(x, gain, w, out_dtype, tm=1024, tn=1024):
    m, d = x.shape
    n = w.shape[1]
    tn = min(tn, n)
    while n % tn:
        tn //= 2
    assert m % tm == 0 and tn % 128 == 0
    return pl.pallas_call(
        _norm_matmul_body,
        grid=(m // tm, n // tn),
        in_specs=[
            pl.BlockSpec((tm, d), lambda i, j: (i, 0)),
            pl.BlockSpec((1, d), lambda i, j: (0, 0)),
            pl.BlockSpec((d, tn), lambda i, j: (0, j)),
        ],
        out_specs=pl.BlockSpec((tm, tn), lambda i, j: (i, j)),
        out_shape=jax.ShapeDtypeStruct((m, n), out_dtype),
        scratch_shapes=[pltpu.VMEM((tm, d), BF16)],
        compiler_params=_cparams("parallel", "arbitrary"),
        name="norm_matmul",
    )(x, gain.reshape(1, d), w)


def _matmul_residual_body(x_ref, y_ref, w_ref, o_ref):
    o_ref[...] = x_ref[...] + jnp.dot(y_ref[...], w_ref[...], preferred_element_type=F32)


def matmul_residual(x, y, w, tm=1024):
    m, d = x.shape
    k = y.shape[1]
    return pl.pallas_call(
        _matmul_residual_body,
        grid=(m // tm,),
        in_specs=[
            pl.BlockSpec((tm, d), lambda i: (i, 0)),
            pl.BlockSpec((tm, k), lambda i: (i, 0)),
            pl.BlockSpec((k, d), lambda i: (0, 0)),
        ],
        out_specs=pl.BlockSpec((tm, d), lambda i: (i, 0)),
        out_shape=jax.ShapeDtypeStruct((m, d), F32),
        compiler_params=_cparams("parallel"),
        name="matmul_residual",
    )(x, y, w)


def _final_norm_body(x_ref, g_ref, o_ref):
    x = x_ref[...]
    ms = jnp.mean(x * x, axis=-1, keepdims=True)
    o_ref[...] = x * lax.rsqrt(ms + NORM_EPS) * g_ref[...]


def final_norm(x, gain, tm=1024):
    m, d = x.shape
    return pl.pallas_call(
        _final_norm_body,
        grid=(m // tm,),
        in_specs=[pl.BlockSpec((tm, d), lambda i: (i, 0)), pl.BlockSpec((1, d), lambda i: (0, 0))],
        out_specs=pl.BlockSpec((tm, d), lambda i: (i, 0)),
        out_shape=jax.ShapeDtypeStruct((m, d), F32),
        compiler_params=_cparams("parallel"),
        name="final_norm",
    )(x, gain.reshape(1, d))


def _top16_desc(s, with_rank):
    work = s
    rank = jnp.full(s.shape, 99.0, F32) if with_rank else None
    vals = []
    for r in range(PEER_TOPK):
        m = jnp.max(work, axis=0, keepdims=True)
        eq = work == m
        if with_rank:
            rank = jnp.where(eq, float(r + 1), rank)
        work = jnp.where(eq, -jnp.inf, work)
        vals.append(m)
    return jnp.concatenate(vals, axis=0), rank


def _peer_route_body(x_ref, g_ref, wqT_ref, keys_ref, hnT_ref, n_ref, e0_ref, r1_ref, e1_ref):
    x = x_ref[...]
    ms = jnp.mean(x * x, axis=-1, keepdims=True)
    hn = x * lax.rsqrt(ms + NORM_EPS) * g_ref[...]
    hnT = hn.T.astype(BF16)
    hnT_ref[...] = hnT
    qT = jnp.dot(wqT_ref[...], hnT, preferred_element_type=F32)
    row8 = lax.broadcasted_iota(jnp.int32, (8, x.shape[0]), 0)
    for h in range(PEER_HEADS):
        s = []
        for p in range(2):
            hp = 2 * h + p
            q_hp = qT[hp * PEER_HALF:(hp + 1) * PEER_HALF, :].astype(BF16)
            s.append(jnp.dot(keys_ref[hp], q_hp, preferred_element_type=F32))
        a0, _ = _top16_desc(s[0], False)
        b, rank1 = _top16_desc(s[1], True)
        cand = [a0[0:1, :] + b]
        for p in range(1, PEER_TOPK):
            cnt = PEER_TOPK // (p + 1)
            c = a0[p:p + 1, :] + b[0:8, :]
            cand.append(c if cnt >= 8 else jnp.where(row8 < cnt, c, -jnp.inf))
        best, _ = _top16_desc(jnp.concatenate(cand, axis=0), False)
        tau = best[PEER_TOPK - 1:PEER_TOPK, :]
        z = jnp.sum(jnp.exp(best - best[0:1, :]), axis=0, keepdims=True)
        n_map = jnp.zeros_like(s[0])
        for q in range(PEER_TOPK):
            n_map = n_map + jnp.where(s[0] + b[q:q + 1, :] >= tau, 1.0, 0.0)
        n_ref[h] = jnp.where(s[0] >= a0[PEER_TOPK - 1:PEER_TOPK, :], n_map, 0.0)
        e0_ref[h] = jnp.exp(s[0] - a0[0:1, :]) * (0.5 / z)
        r1_ref[h] = rank1.astype(BF16)
        e1_ref[h] = jnp.exp(s[1] - b[0:1, :]).astype(BF16)


def peer_route(x, gain, wqT, keys, tm=256):
    m, d = x.shape
    nq = wqT.shape[0]
    tab = jax.ShapeDtypeStruct((PEER_HEADS, PEER_NKEYS, m), F32)
    tab16 = jax.ShapeDtypeStruct((PEER_HEADS, PEER_NKEYS, m), BF16)
    tab_spec = pl.BlockSpec((PEER_HEADS, PEER_NKEYS, tm), lambda i: (0, 0, i))
    return pl.pallas_call(
        _peer_route_body,
        grid=(m // tm,),
        in_specs=[
            pl.BlockSpec((tm, d), lambda i: (i, 0)),
            pl.BlockSpec((1, d), lambda i: (0, 0)),
            pl.BlockSpec((nq, d), lambda i: (0, 0)),
            pl.BlockSpec(keys.shape, lambda i: (0, 0, 0)),
        ],
        out_specs=[pl.BlockSpec((d, tm), lambda i: (0, i)), tab_spec, tab_spec, tab_spec, tab_spec],
        out_shape=[jax.ShapeDtypeStruct((d, m), BF16), tab, tab, tab16, tab16],
        compiler_params=_cparams("parallel"),
        name="peer_route",
    )(x, gain.reshape(1, d), wqT, keys)


def _peer_expert_body(x_ref, hnT_ref, n_ref, e0_ref, r1_ref, e1_ref, u_ref, vT_ref, o_ref,
                      a_s, acc_ref, *, ni, nb):
    j = pl.program_id(1)

    @pl.when(j == 0)
    def _():
        a_s[...] = jnp.zeros_like(a_s)
        acc_ref[...] = jnp.zeros_like(acc_ref)

    ib = jnp.minimum(j, nb - 1)
    tm = a_s.shape[2]
    pk = BF16_SUBLANES
    nk = PEER_NKEYS
    cur = j % 2
    prev = 1 - cur

    zc = ni * nk // PEER_Z_CHUNKS
    z_chunks = [jnp.dot(u_ref[pl.ds(c * zc, zc), :], hnT_ref[...], preferred_element_type=F32)
                for c in range(PEER_Z_CHUNKS)]
    acc_ref[...] += jnp.dot(vT_ref[0], a_s[prev], preferred_element_type=F32)
    for ii in range(ni):
        i = ib * ni + ii
        r0 = ii * nk - (ii * nk // zc) * zc
        z = z_chunks[ii * nk // zc][r0:r0 + nk, :]
        act = (z * (1.0 + lax.erf(z * (1.0 / math.sqrt(2.0))))).astype(BF16)
        gate = None
        for h in range(PEER_HEADS):
            n_b = jnp.broadcast_to(n_ref[h, pl.ds(i, 1), :], (pk, tm)).astype(BF16)[None]
            e0_b = jnp.broadcast_to(e0_ref[h, pl.ds(i, 1), :], (pk, tm)).astype(BF16)[None]
            r1 = r1_ref[h].reshape(nk // pk, pk, tm)
            e1 = e1_ref[h].reshape(nk // pk, pk, tm)
            term = jnp.where(r1 <= n_b, e1 * e0_b, jnp.zeros((), BF16))
            gate = term if gate is None else gate + term
        a_s[cur, pl.ds(ii * nk, nk), :] = (act.reshape(nk // pk, pk, tm) * gate).reshape(nk, tm)

    @pl.when(j == nb)
    def _():
        o_ref[...] = x_ref[...] + acc_ref[...].T


def peer_experts(x, hnT, n_tab, e0_tab, r1_tab, e1_tab, u, v_tab, tm=512, ni=PEER_EXPERT_ROWS):
    m, d = x.shape
    ne = u.shape[0]
    et = ni * PEER_NKEYS
    nb = ne // et
    vT = jnp.transpose(v_tab.astype(BF16).reshape(nb, et, d), (0, 2, 1))
    tab_spec = pl.BlockSpec((PEER_HEADS, PEER_NKEYS, tm), lambda i, j: (0, 0, i))
    return pl.pallas_call(
        functools.partial(_peer_expert_body, ni=ni, nb=nb),
        grid=(m // tm, nb + 1),
        in_specs=[
            pl.BlockSpec((tm, d), lambda i, j: (i, 0)),
            pl.BlockSpec((d, tm), lambda i, j: (0, i)),
            tab_spec, tab_spec, tab_spec, tab_spec,
            pl.BlockSpec((et, d), lambda i, j: (jnp.minimum(j, nb - 1), 0)),
            pl.BlockSpec((1, d, et), lambda i, j: (jnp.clip(j - 1, 0, nb - 1), 0, 0)),
        ],
        out_specs=pl.BlockSpec((tm, d), lambda i, j: (i, 0)),
        out_shape=jax.ShapeDtypeStruct((m, d), F32),
        scratch_shapes=[pltpu.VMEM((2, et, tm), BF16), pltpu.VMEM((d, tm), F32)],
        compiler_params=_cparams("parallel", "arbitrary"),
        name="peer_experts",
    )(x, hnT, n_tab, e0_tab, r1_tab, e1_tab, u, vT)


def peer_layer(x, gain, w_q, keys, u_tab, v_tab):
    wqT = w_q.T.astype(BF16)
    keys2 = keys.reshape(PEER_HEADS * 2, PEER_NKEYS, PEER_HALF).astype(BF16)
    hnT, n_tab, e0_tab, r1_tab, e1_tab = peer_route(x, gain, wqT, keys2)
    return peer_experts(x, hnT, n_tab, e0_tab, r1_tab, e1_tab, u_tab.astype(BF16), v_tab)


def _rope_lanes(t, cos, sin_signed):
    half = SWA_HEADDIM // 2
    lane = lax.broadcasted_iota(jnp.int32, t.shape, 1)
    first = (lane % SWA_HEADDIM) < half
    partner = jnp.where(first, pltpu.roll(t, 128 - half, 1), pltpu.roll(t, half, 1))
    return t * cos + partner * sin_signed


def _swa_body(q_ref, k_ref, v_ref, cos_ref, sin_ref, sink_ref, o_ref, kp_ref, vp_ref):
    w = SWA_WINDOW
    t_len = q_ref.shape[1]
    nkv = SWA_KV_HEADS * SWA_HEADDIM
    rep = SWA_HEADS // SWA_KV_HEADS
    zeros = jnp.zeros((w, nkv), BF16)
    kp_ref[pl.ds(0, w), :] = zeros
    kp_ref[pl.ds(w + t_len, w), :] = zeros
    vp_ref[pl.ds(0, w), :] = zeros
    vp_ref[pl.ds(w + t_len, w), :] = zeros
    vp_ref[pl.ds(w, t_len), :] = v_ref[0]
    cos_all = cos_ref[...]
    sin_all = sin_ref[...]
    for c in range(nkv // 128):
        kc = k_ref[0, :, c * 128:(c + 1) * 128].astype(F32)
        kp_ref[pl.ds(w, t_len), c * 128:(c + 1) * 128] = _rope_lanes(kc, cos_all, sin_all).astype(BF16)

    row = lax.broadcasted_iota(jnp.int32, (w, 3 * w), 0)
    col = lax.broadcasted_iota(jnp.int32, (w, 3 * w), 1)
    band = (col >= row) & (col <= row + 2 * w)

    def block(n, carry):
        base = pl.multiple_of(n * w, w)
        cos_b = cos_ref[pl.ds(base, w), :]
        sin_b = sin_ref[pl.ds(base, w), :]
        kpos = base - w + col
        valid = band & (kpos >= 0) & (kpos < t_len)
        outs = []
        for g in range(SWA_KV_HEADS):
            kw = kp_ref[pl.ds(base, 3 * w), g * SWA_HEADDIM:(g + 1) * SWA_HEADDIM]
            vw = vp_ref[pl.ds(base, 3 * w), g * SWA_HEADDIM:(g + 1) * SWA_HEADDIM]
            scores = []
            for c in range(g * rep // 2, (g + 1) * rep // 2):
                qc = q_ref[0, pl.ds(base, w), c * 128:(c + 1) * 128].astype(F32)
                qc = (_rope_lanes(qc, cos_b, sin_b) * (SWA_HEADDIM ** -0.5)).astype(BF16)
                for hh in range(2):
                    qh = qc[:, hh * SWA_HEADDIM:(hh + 1) * SWA_HEADDIM]
                    scores.append(lax.dot_general(qh, kw, (((1,), (1,)), ((), ())), preferred_element_type=F32))
            probs = []
            for k, s in enumerate(scores):
                s = jnp.where(valid, s, NEG)
                sk = sink_ref[g * rep + k]
                m = jnp.maximum(jnp.max(s, axis=-1, keepdims=True), sk)
                p = jnp.exp(s - m)
                inv = 1.0 / (jnp.sum(p, axis=-1, keepdims=True) + jnp.exp(sk - m))
                probs.append((p.astype(BF16), inv))
            for p, inv in probs:
                outs.append(jnp.dot(p, vw, preferred_element_type=F32) * inv)
        o_ref[0, pl.ds(base, w), :] = jnp.concatenate(outs, axis=-1).astype(o_ref.dtype)
        return carry

    lax.fori_loop(0, t_len // w, block, 0)


def swa_attention(proj, sink, bsz, t_len):
    half = SWA_HEADDIM // 2
    inv_freq = ROPE_THETA ** (-jnp.arange(half, dtype=F32) / half)
    ang = jnp.arange(t_len, dtype=F32)[:, None] * inv_freq[None, :]
    cos, sin = jnp.cos(ang), jnp.sin(ang)
    cos_t = jnp.tile(jnp.concatenate([cos, cos], axis=-1), (1, 2))
    sin_t = jnp.tile(jnp.concatenate([-sin, sin], axis=-1), (1, 2))
    nq = SWA_HEADS * SWA_HEADDIM
    nkv = SWA_KV_HEADS * SWA_HEADDIM
    return pl.pallas_call(
        _swa_body,
        grid=(bsz,),
        in_specs=[
            pl.BlockSpec((1, t_len, nq), lambda b: (b, 0, 0)),
            pl.BlockSpec((1, t_len, nkv), lambda b: (b, 0, nq // nkv)),
            pl.BlockSpec((1, t_len, nkv), lambda b: (b, 0, nq // nkv + 1)),
            pl.BlockSpec((t_len, 128), lambda b: (0, 0)),
            pl.BlockSpec((t_len, 128), lambda b: (0, 0)),
            pl.BlockSpec(memory_space=pltpu.SMEM),
        ],
        out_specs=pl.BlockSpec((1, t_len, nq), lambda b: (b, 0, 0)),
        out_shape=jax.ShapeDtypeStruct((bsz, t_len, nq), BF16),
        scratch_shapes=[pltpu.VMEM((t_len + 2 * SWA_WINDOW, nkv), BF16),
                        pltpu.VMEM((t_len + 2 * SWA_WINDOW, nkv), BF16)],
        compiler_params=_cparams("parallel"),
        name="swa_attention",
    )(proj, proj, proj, cos_t, sin_t, sink.astype(F32))


def _na_bias_table(rpb):
    qc = np.arange(GRID_W)[:, None]
    kc = np.arange(GRID_W)[None, :]
    cstart = np.clip(qc - NA_COLS // 2, 0, GRID_W - NA_COLS)
    valid = (kc >= cstart) & (kc < cstart + NA_COLS)
    cidx = np.clip(kc - qc + NA_COLS - 1, 0, 2 * NA_COLS - 2)
    onehot = (np.arange(2 * NA_COLS - 1)[:, None, None] == cidx[None]).astype(np.float32)
    toep = jnp.einsum('hrc,cqk->hrqk', rpb.astype(F32), onehot, precision=lax.Precision.HIGHEST)
    toep = jnp.where(valid[None, None], toep, NEG)
    return jnp.concatenate([toep[:, :-1], toep[:, 1:]], axis=-1)


def _na_body(q_ref, k_ref, v_ref, bias_ref, o_ref):
    t_len = q_ref.shape[1]
    rows = t_len // GRID_W
    win = NA_ROWS * GRID_W

    def row_group(gi, carry):
        chains = []
        for rr in range(NA_ROW_UNROLL):
            r = gi * NA_ROW_UNROLL + rr
            rs = jnp.clip(r - NA_ROWS // 2, 0, rows - NA_ROWS)
            d0 = rs - r + NA_ROWS - 1
            qbase = pl.multiple_of(r * GRID_W, GRID_W)
            kbase = pl.multiple_of(rs * GRID_W, GRID_W)
            qr = q_ref[0, pl.ds(qbase, GRID_W), :]
            kw = k_ref[0, pl.ds(kbase, win), :]
            for hh in range(2):
                sl = slice(hh * NA_HEADDIM, (hh + 1) * NA_HEADDIM)
                s = lax.dot_general(qr[:, sl], kw[:, sl], (((1,), (1,)), ((), ())), preferred_element_type=F32)
                chains.append((hh, d0, kbase, s))
        probs = []
        for hh, d0, kbase, s in chains:
            bias = jnp.concatenate([bias_ref[hh, d0 + 2 * c] for c in range(NA_ROWS // 2)], axis=-1)
            s = s * (NA_HEADDIM ** -0.5) + bias
            m = jnp.max(s, axis=-1, keepdims=True)
            p = jnp.exp(s - m)
            probs.append((p.astype(BF16), 1.0 / jnp.sum(p, axis=-1, keepdims=True)))
        outs = []
        for (hh, d0, kbase, s), (p, inv) in zip(chains, probs):
            vw = v_ref[0, pl.ds(kbase, win), hh * NA_HEADDIM:(hh + 1) * NA_HEADDIM]
            outs.append(jnp.dot(p, vw, preferred_element_type=F32) * inv)
        for rr in range(NA_ROW_UNROLL):
            qbase = pl.multiple_of((gi * NA_ROW_UNROLL + rr) * GRID_W, GRID_W)
            o_ref[0, pl.ds(qbase, GRID_W), :] = jnp.concatenate(outs[2 * rr:2 * rr + 2], axis=-1).astype(o_ref.dtype)
        return carry

    lax.fori_loop(0, rows // NA_ROW_UNROLL, row_group, 0)


def na_attention(proj, rpb, bsz, t_len):
    bias = _na_bias_table(rpb)
    npair = NA_HEADS // 2
    return pl.pallas_call(
        _na_body,
        grid=(npair, bsz),
        in_specs=[
            pl.BlockSpec((1, t_len, 128), lambda hp, b: (b, 0, hp)),
            pl.BlockSpec((1, t_len, 128), lambda hp, b: (b, 0, npair + hp)),
            pl.BlockSpec((1, t_len, 128), lambda hp, b: (b, 0, 2 * npair + hp)),
            pl.BlockSpec((2, 2 * NA_ROWS - 2, GRID_W, 2 * GRID_W), lambda hp, b: (hp, 0, 0, 0)),
        ],
        out_specs=pl.BlockSpec((1, t_len, 128), lambda hp, b: (b, 0, hp)),
        out_shape=jax.ShapeDtypeStruct((bsz, t_len, NA_WIDTH), BF16),
        compiler_params=_cparams("parallel", "parallel"),
        name="na_attention",
    )(proj, proj, proj, bias)


def _conv_silu_body(x_ref, w_ref, b_ref, o_ref, *, n_l2):
    t_len, tc = x_ref.shape[1], x_ref.shape[2]
    rc = CONV_ROW_CHUNK
    halo = F32_SUBLANES
    normalise = pl.program_id(1) < n_l2

    def chunk(ci, carry):
        r0 = pl.multiple_of(ci * rc, rc)
        cur = x_ref[0, pl.ds(r0, rc), :].astype(F32)
        lo = pl.multiple_of(jnp.maximum(r0 - halo, 0), halo)
        hi = pl.multiple_of(jnp.minimum(r0 + rc, t_len - halo), halo)
        before = jnp.where(r0 > 0, x_ref[0, pl.ds(lo, halo), :].astype(F32), 0.0)
        after = jnp.where(r0 + rc < t_len, x_ref[0, pl.ds(hi, halo), :].astype(F32), 0.0)
        xx = jnp.concatenate([before, cur, after], axis=0)
        acc = cur * w_ref[CONV_K // 2:CONV_K // 2 + 1, :] + b_ref[...]
        for k in range(CONV_K):
            off = k - CONV_K // 2
            if off == 0:
                continue
            acc = acc + xx[halo + off:halo + off + rc, :] * w_ref[k:k + 1, :]
        y = acc * jax.nn.sigmoid(acc)
        if n_l2 > 0:
            parts = []
            for c in range(tc // 128):
                yc = y[:, c * 128:(c + 1) * 128]
                ss = jnp.sum(yc * yc, axis=-1, keepdims=True)
                parts.append(yc * jnp.where(normalise, lax.rsqrt(ss + NORM_EPS), 1.0))
            y = jnp.concatenate(parts, axis=-1)
        o_ref[0, pl.ds(r0, rc), :] = y.astype(o_ref.dtype)
        return carry

    lax.fori_loop(0, t_len // rc, chunk, 0)


def conv_silu(proj, w, bias, col0, n_ch, n_l2=0, tc=512):
    bsz, t_len, _ = proj.shape
    assert col0 % tc == 0 and n_ch % tc == 0
    c0 = col0 // tc
    return pl.pallas_call(
        functools.partial(_conv_silu_body, n_l2=n_l2),
        grid=(bsz, n_ch // tc),
        in_specs=[
            pl.BlockSpec((1, t_len, tc), lambda b, j: (b, 0, c0 + j)),
            pl.BlockSpec((CONV_K, tc), lambda b, j: (0, j)),
            pl.BlockSpec((1, tc), lambda b, j: (0, j)),
        ],
        out_specs=pl.BlockSpec((1, t_len, tc), lambda b, j: (b, 0, j)),
        out_shape=jax.ShapeDtypeStruct((bsz, t_len, n_ch), BF16),
        compiler_params=_cparams("parallel", "parallel"),
        name="conv_silu",
    )(proj, w.astype(F32), bias.astype(F32).reshape(1, n_ch))


def _softplus(x):
    return jnp.maximum(x, 0.0) + jnp.log1p(jnp.exp(-jnp.abs(x)))


def _bmm(a, b, precision=None):
    return lax.dot_general(a, b, (((2,), (1,)), ((0,), (0,))), precision=precision,
                           preferred_element_type=F32)


def _split_bf16(a):
    hi = a.astype(BF16)
    return hi, (a - hi.astype(F32)).astype(BF16)


def _bmm_split(a, b):
    return _bmm(a[0], b[0]) + _bmm(a[0], b[1]) + _bmm(a[1], b[0])


def _bmm_nt(a, b):
    return lax.dot_general(a, b, (((2,), (2,)), ((0,), (0,))), preferred_element_type=F32)


def _gdn_body(q_ref, k_ref, v_ref, z_ref, gcol_ref, grow_ref, alog_ref, dtb_ref, nw_ref, o_ref,
              u_s, w_s, qk_s, qg_s, kg_s, el_s, o_s):
    c_len = GDN_CHUNK
    t_len = q_ref.shape[1]
    nc = t_len // c_len
    h = pl.program_id(1)
    ii = lax.broadcasted_iota(jnp.int32, (c_len, c_len), 0)
    jj = lax.broadcasted_iota(jnp.int32, (c_len, c_len), 1)
    eye = (ii == jj).astype(F32)
    cg = GDN_PHASE_A_CHUNKS

    def phase_a(gi, carry):
        c0 = pl.multiple_of(gi * cg, cg)
        rows = pl.ds(pl.multiple_of(gi * (cg * c_len), cg * c_len), cg * c_len)
        k3 = k_ref[0, rows, :].reshape(cg, c_len, GDN_DK)
        v3 = v_ref[0, rows, :].reshape(cg, c_len, GDN_DK).astype(F32)
        kf = k3.astype(F32)
        qs = q_ref[0, rows, :].reshape(cg, c_len, GDN_DK).astype(F32) * (GDN_DK ** -0.5)
        gcol = gcol_ref[0, 0, pl.ds(c0, cg)]
        grow = grow_ref[0, 0, pl.ds(c0, cg)]
        qk_raw = _bmm_nt(qs.astype(BF16), k3)
        kb, gc_c, g_last, decay, strict = [], [], [], [], []
        for d in range(2):
            incl = (ii >= jj) if d == 0 else (ii <= jj)
            strict.append((ii > jj) if d == 0 else (ii < jj))
            tri = incl.astype(F32)
            tri_t = ((ii <= jj) if d == 0 else (ii >= jj)).astype(F32)
            neg_a = -jnp.exp(jnp.full((1, 1, 1), alog_ref[d, h], F32))
            dtb = dtb_ref[d, h]
            g_c = neg_a * _softplus(gcol[:, :, d:d + 1] + dtb)
            g_r = neg_a * _softplus(grow[:, d:d + 1, :] + dtb)
            beta_c = jax.nn.sigmoid(gcol[:, :, 2 + d:3 + d])
            gc_c.append(jnp.sum(tri[None] * g_r, axis=2, keepdims=True))
            gc_r = jnp.sum(tri_t[None] * g_c, axis=1, keepdims=True)
            g_last.append(jnp.sum(g_r, axis=2, keepdims=True))
            decay.append(jnp.where(incl[None], jnp.exp(jnp.where(incl[None], gc_c[d] - gc_r, 0.0)), 0.0))
            kb.append((kf * beta_c, v3 * beta_c))
        kk = [_bmm_nt(kb[d][0].astype(BF16), k3) for d in range(2)]
        low = [jnp.where(strict[d][None], kk[d] * decay[d], 0.0) for d in range(2)]
        inv = [eye[None] - low[d] for d in range(2)]
        pw = [_split_bf16(low[d]) for d in range(2)]
        for _ in range(5):
            pw = [_split_bf16(_bmm_split(pw[d], pw[d])) for d in range(2)]
            inv = [inv[d] + _bmm_split(_split_bf16(inv[d]), pw[d]) for d in range(2)]
        rhs = [jnp.concatenate([kb[d][1], kb[d][0] * jnp.exp(gc_c[d])], axis=-1) for d in range(2)]
        sol = [_bmm_split(_split_bf16(inv[d]), _split_bf16(rhs[d])) for d in range(2)]
        for d in range(2):
            u_s[d, pl.ds(c0, cg)] = sol[d][:, :, :GDN_DK]
            w_s[d, pl.ds(c0, cg)] = sol[d][:, :, GDN_DK:].astype(BF16)
            qk_s[d, pl.ds(c0, cg)] = (qk_raw * decay[d]).astype(BF16)
            qg_s[d, pl.ds(c0, cg)] = (qs * jnp.exp(gc_c[d])).astype(BF16)
            kg_s[d, pl.ds(c0, cg)] = (kf * jnp.exp(g_last[d] - gc_c[d])).astype(BF16)
            el_s[d, pl.ds(c0, cg)] = jnp.broadcast_to(jnp.exp(g_last[d]), (cg, 1, GDN_DK))
        return carry

    lax.fori_loop(0, nc // cg, phase_a, 0)

    def step(t, carry):
        cs = (t, nc - 1 - t)
        sb = [carry[d].astype(BF16) for d in range(2)]
        ws = [jnp.dot(w_s[d, cs[d]], sb[d], preferred_element_type=F32) for d in range(2)]
        qs_ = [jnp.dot(qg_s[d, cs[d]], sb[d], preferred_element_type=F32) for d in range(2)]
        vb = [(u_s[d, cs[d]] - ws[d]).astype(BF16) for d in range(2)]
        os_ = [qs_[d] + jnp.dot(qk_s[d, cs[d]], vb[d], preferred_element_type=F32) for d in range(2)]
        kv = [lax.dot_general(kg_s[d, cs[d]], vb[d], (((0,), (0,)), ((), ())), preferred_element_type=F32)
              for d in range(2)]
        for d in range(2):
            o_s[d, cs[d]] = os_[d]
        return tuple(carry[d] * el_s[d, cs[d]] + kv[d] for d in range(2))

    s0 = jnp.zeros((GDN_DK, GDN_DK), F32)
    lax.fori_loop(0, nc, step, (s0, s0))

    o = (o_s[0] + o_s[1]).reshape(t_len, GDN_DK)
    ms = jnp.mean(o * o, axis=-1, keepdims=True)
    z = z_ref[0]
    y = o * lax.rsqrt(ms + NORM_EPS) * nw_ref[...] * (z * jax.nn.sigmoid(z))
    o_ref[0] = y.astype(o_ref.dtype)


def gdn_scan(qkv, proj, gates, a_log, dt_bias, norm_w, bsz, t_len):
    nc = t_len // GDN_CHUNK
    nh = GDN_HEADS
    g4 = gates[:, :4 * nh].reshape(bsz, nc, GDN_CHUNK, 4, nh)
    gcol = jnp.transpose(g4, (0, 4, 1, 2, 3))
    grow = jnp.transpose(g4, (0, 4, 1, 3, 2))
    dk = GDN_DK
    nq = GDN_QK // dk
    return pl.pallas_call(
        _gdn_body,
        grid=(bsz, nh),
        in_specs=[
            pl.BlockSpec((1, t_len, dk), lambda b, h: (b, 0, h)),
            pl.BlockSpec((1, t_len, dk), lambda b, h: (b, 0, nq + h)),
            pl.BlockSpec((1, t_len, dk), lambda b, h: (b, 0, 2 * nq + h)),
            pl.BlockSpec((1, t_len, dk), lambda b, h: (b, 0, 3 * nq + h)),
            pl.BlockSpec((1, 1, nc, GDN_CHUNK, 4), lambda b, h: (b, h, 0, 0, 0)),
            pl.BlockSpec((1, 1, nc, 4, GDN_CHUNK), lambda b, h: (b, h, 0, 0, 0)),
            pl.BlockSpec(memory_space=pltpu.SMEM),
            pl.BlockSpec(memory_space=pltpu.SMEM),
            pl.BlockSpec((1, dk), lambda b, h: (0, 0)),
        ],
        out_specs=pl.BlockSpec((1, t_len, dk), lambda b, h: (b, 0, h)),
        out_shape=jax.ShapeDtypeStruct((bsz, t_len, GDN_V), BF16),
        scratch_shapes=[
            pltpu.VMEM((2, nc, GDN_CHUNK, dk), F32),
            pltpu.VMEM((2, nc, GDN_CHUNK, dk), BF16),
            pltpu.VMEM((2, nc, GDN_CHUNK, GDN_CHUNK), BF16),
            pltpu.VMEM((2, nc, GDN_CHUNK, dk), BF16),
            pltpu.VMEM((2, nc, GDN_CHUNK, dk), BF16),
            pltpu.VMEM((2, nc, 1, dk), F32),
            pltpu.VMEM((2, nc, GDN_CHUNK, dk), F32),
        ],
        compiler_params=_cparams("parallel", "parallel"),
        name="gdn_scan",
    )(qkv, qkv, qkv, proj, gcol, grow, a_log.astype(F32), dt_bias.astype(F32),
      norm_w.astype(F32).reshape(1, dk))


def _pad_cols(w, n):
    return jnp.pad(w, ((0, 0), (0, n - w.shape[1])))


def gdn_layer(x, gain, w_in, conv_w, a_log, dt_bias, norm_w, w_out, bsz, t_len):
    n_main = GDN_CONV_CH + GDN_V
    proj = norm_matmul(x, gain, w_in[:, :n_main].astype(BF16), F32)
    gates = norm_matmul(x, gain, _pad_cols(w_in[:, n_main:], 128).astype(BF16), F32)
    proj3 = proj.reshape(bsz, t_len, n_main)
    qkv = conv_silu(proj3, conv_w, jnp.zeros((GDN_CONV_CH,), F32), 0, GDN_CONV_CH,
                    n_l2=2 * GDN_QK // 512)
    y = gdn_scan(qkv, proj3, gates, a_log, dt_bias, norm_w, bsz, t_len)
    return matmul_residual(x, y.reshape(bsz * t_len, GDN_V), w_out.astype(BF16))


def _ssd_body(xs_ref, b_ref, c_ref, z_ref, dcol_ref, drow_ref, alog_ref, dtbc_ref, dtbr_ref, dskip_ref, nw_ref, o_ref,
              xw_s, ea_s, el_s, y_s, st_s):
    c_len = SSD_CHUNK
    t_len = xs_ref.shape[1]
    nc = t_len // c_len
    nr = SSD_HEADS // SSD_GROUPS
    hp = SSD_HEADDIM
    g = pl.program_id(1)
    ii = lax.broadcasted_iota(jnp.int32, (c_len, c_len), 0)
    jj = lax.broadcasted_iota(jnp.int32, (c_len, c_len), 1)
    cg = SSD_PHASE_A_CHUNKS

    def phase_a(gi, carry):
        c0 = pl.multiple_of(gi * cg, cg)
        rows = pl.ds(pl.multiple_of(gi * (cg * c_len), cg * c_len), cg * c_len)
        x3 = xs_ref[0, rows, :].reshape(cg, c_len, nr * hp)
        b3 = b_ref[0, rows, :].reshape(cg, c_len, SSD_STATE)
        c3 = c_ref[0, rows, :].reshape(cg, c_len, SSD_STATE)
        dcol = dcol_ref[0, 0, pl.ds(c0, cg)]
        drow = drow_ref[0, 0, pl.ds(c0, cg)]
        cb = _bmm_nt(c3, b3)
        dt_c_all = _softplus(dcol + dtbc_ref[0])
        dt_r_all = _softplus(drow + dtbr_ref[0])
        ydiag = None
        for d in range(2):
            incl = (ii >= jj) if d == 0 else (ii <= jj)
            tri = incl.astype(F32)
            tri_t = ((ii <= jj) if d == 0 else (ii >= jj)).astype(F32)
            yd, xw, ea, el = [], [], [], []
            for r in range(nr):
                hidx = g * nr + r
                a = -jnp.exp(jnp.full((1, 1, 1), alog_ref[d, hidx], F32))
                col = d * nr + r
                dt_c = dt_c_all[:, :, col:col + 1]
                dt_r = dt_r_all[:, col:col + 1, :]
                ac_c = jnp.sum(tri[None] * (dt_r * a), axis=2, keepdims=True)
                ac_r = jnp.sum(tri_t[None] * (dt_c * a), axis=1, keepdims=True)
                a_last = jnp.sum(dt_r * a, axis=2, keepdims=True)
                seg = jnp.where(incl[None], jnp.exp(jnp.where(incl[None], ac_c - ac_r, 0.0)), 0.0)
                xr = x3[:, :, r * hp:(r + 1) * hp].astype(F32) * dt_c
                yd.append(_bmm((cb * seg).astype(BF16), xr.astype(BF16)))
                xw.append((xr * jnp.exp(a_last - ac_c)).astype(BF16))
                ea.append(jnp.broadcast_to(jnp.exp(ac_c), (cg, c_len, hp)))
                el.append(jnp.broadcast_to(jnp.exp(a_last), (cg, 1, hp)))
            yd = jnp.concatenate(yd, axis=-1)
            ydiag = yd if ydiag is None else ydiag + yd
            xw_s[d, pl.ds(c0, cg)] = jnp.concatenate(xw, axis=-1)
            ea_s[d, pl.ds(c0, cg)] = jnp.concatenate(ea, axis=-1)
            el_s[d, pl.ds(c0, cg)] = jnp.concatenate(el, axis=-1)
        y_s[pl.ds(c0, cg)] = ydiag
        return carry

    lax.fori_loop(0, nc // cg, phase_a, 0)

    st_s[...] = jnp.zeros_like(st_s)

    def step(t, carry):
        for d in range(2):
            c = t if d == 0 else nc - 1 - t
            base = pl.multiple_of(c * c_len, c_len)
            cc = c_ref[0, pl.ds(base, c_len), :]
            bb = b_ref[0, pl.ds(base, c_len), :]
            st = st_s[d]
            y_s[c] += jnp.dot(cc, st.astype(BF16), preferred_element_type=F32) * ea_s[d, c]
            st_s[d] = st * el_s[d, c] + lax.dot_general(bb, xw_s[d, c], (((0,), (0,)), ((), ())),
                                                        preferred_element_type=F32)
        return carry

    lax.fori_loop(0, nc, step, 0)

    xs = xs_ref[0].astype(F32)
    z = z_ref[0]
    y = (y_s[...].reshape(t_len, nr * hp) + dskip_ref[...] * xs) * (z * jax.nn.sigmoid(z))
    ms = jnp.mean(y * y, axis=-1, keepdims=True)
    o_ref[0] = (y * lax.rsqrt(ms + NORM_EPS) * nw_ref[...]).astype(o_ref.dtype)


def ssd_scan(xbc, proj, dts, a_log, dt_bias, d_skip, norm_w, bsz, t_len):
    nc = t_len // SSD_CHUNK
    ng = SSD_GROUPS
    nr = SSD_HEADS // ng
    gw = nr * SSD_HEADDIM
    d6 = dts[:, :2 * SSD_HEADS].reshape(bsz, nc, SSD_CHUNK, 2, ng, nr)
    dcol = jnp.transpose(d6, (0, 4, 1, 2, 3, 5)).reshape(bsz, ng, nc, SSD_CHUNK, 2 * nr)
    drow = jnp.transpose(d6, (0, 4, 1, 3, 5, 2)).reshape(bsz, ng, nc, 2 * nr, SSD_CHUNK)
    nb0 = SSD_INNER // SSD_STATE
    dtb_g = jnp.transpose(dt_bias.astype(F32).reshape(2, ng, nr), (1, 0, 2)).reshape(ng, 2 * nr)
    return pl.pallas_call(
        _ssd_body,
        grid=(bsz, ng),
        in_specs=[
            pl.BlockSpec((1, t_len, gw), lambda b, g: (b, 0, g)),
            pl.BlockSpec((1, t_len, SSD_STATE), lambda b, g: (b, 0, nb0 + g)),
            pl.BlockSpec((1, t_len, SSD_STATE), lambda b, g: (b, 0, nb0 + ng + g)),
            pl.BlockSpec((1, t_len, gw), lambda b, g: (b, 0, g)),
            pl.BlockSpec((1, 1, nc, SSD_CHUNK, 2 * nr), lambda b, g: (b, g, 0, 0, 0)),
            pl.BlockSpec((1, 1, nc, 2 * nr, SSD_CHUNK), lambda b, g: (b, g, 0, 0, 0)),
            pl.BlockSpec(memory_space=pltpu.SMEM),
            pl.BlockSpec((1, 1, 2 * nr), lambda b, g: (g, 0, 0)),
            pl.BlockSpec((1, 2 * nr, 1), lambda b, g: (g, 0, 0)),
            pl.BlockSpec((1, gw), lambda b, g: (0, g)),
            pl.BlockSpec((1, gw), lambda b, g: (0, g)),
        ],
        out_specs=pl.BlockSpec((1, t_len, gw), lambda b, g: (b, 0, g)),
        out_shape=jax.ShapeDtypeStruct((bsz, t_len, SSD_INNER), BF16),
        scratch_shapes=[
            pltpu.VMEM((2, nc, SSD_CHUNK, gw), BF16),
            pltpu.VMEM((2, nc, SSD_CHUNK, gw), F32),
            pltpu.VMEM((2, nc, 1, gw), F32),
            pltpu.VMEM((nc, SSD_CHUNK, gw), F32),
            pltpu.VMEM((2, SSD_STATE, gw), F32),
        ],
        compiler_params=_cparams("parallel", "parallel"),
        name="ssd_scan",
    )(xbc, xbc, xbc, proj, dcol, drow, a_log.astype(F32), dtb_g.reshape(ng, 1, 2 * nr),
      dtb_g.reshape(ng, 2 * nr, 1),
      jnp.repeat(d_skip.astype(F32), SSD_HEADDIM).reshape(1, SSD_INNER),
      norm_w.astype(F32).reshape(1, SSD_INNER))


def ssd_layer(x, gain, w_in, conv_w, conv_b, a_log, dt_bias, d_skip, norm_w, w_out, bsz, t_len):
    n_main = SSD_INNER + SSD_CONV_CH
    proj = norm_matmul(x, gain, w_in[:, :n_main].astype(BF16), F32)
    dts = norm_matmul(x, gain, _pad_cols(w_in[:, n_main:], 128).astype(BF16), F32)
    proj3 = proj.reshape(bsz, t_len, n_main)
    xbc = conv_silu(proj3, conv_w, conv_b, SSD_INNER, SSD_CONV_CH)
    y = ssd_scan(xbc, proj3, dts, a_log, dt_bias, d_skip, norm_w, bsz, t_len)
    return matmul_residual(x, y.reshape(bsz * t_len, SSD_INNER), w_out.astype(BF16))


def swa_layer(x, gain, w_in, sink, w_out, bsz, t_len):
    proj = norm_matmul(x, gain, w_in.astype(BF16), BF16)
    o = swa_attention(proj.reshape(bsz, t_len, -1), sink, bsz, t_len)
    return matmul_residual(x, o.reshape(bsz * t_len, -1), w_out.astype(BF16))


def na_layer(x, gain, w_in, rpb, w_out, bsz, t_len):
    proj = norm_matmul(x, gain, w_in.astype(BF16), BF16)
    o = na_attention(proj.reshape(bsz, t_len, -1), rpb, bsz, t_len)
    return matmul_residual(x, o.reshape(bsz * t_len, -1), w_out.astype(BF16))


def kernel(x, norm_mix, norm_ffn, norm_final, gdn_w_in, gdn_conv, gdn_a_log, gdn_dt_bias, gdn_norm, gdn_w_out, ssd_w_in, ssd_conv, ssd_conv_b, ssd_a_log, ssd_dt_bias, ssd_d, ssd_norm, ssd_w_out, swa_w_in, swa_sink, swa_w_out, na_w_in, na_rpb, na_w_out, peer_w_q, peer_keys, peer_u, peer_v):
    bsz, t_len, d = x.shape
    depth = norm_mix.shape[0]
    xf = x.reshape(bsz * t_len, d)
    for i in range(depth):
        mixer, j = i % 4, i // 4
        if mixer == 0:
            xf = gdn_layer(xf, norm_mix[i], gdn_w_in[j], gdn_conv[j], gdn_a_log[j], gdn_dt_bias[j],
                           gdn_norm[j], gdn_w_out[j], bsz, t_len)
        elif mixer == 1:
            xf = ssd_layer(xf, norm_mix[i], ssd_w_in[j], ssd_conv[j], ssd_conv_b[j], ssd_a_log[j],
                           ssd_dt_bias[j], ssd_d[j], ssd_norm[j], ssd_w_out[j], bsz, t_len)
        elif mixer == 2:
            xf = swa_layer(xf, norm_mix[i], swa_w_in[j], swa_sink[j], swa_w_out[j], bsz, t_len)
        else:
            xf = na_layer(xf, norm_mix[i], na_w_in[j], na_rpb[j], na_w_out[j], bsz, t_len)
        xf = peer_layer(xf, norm_ffn[i], peer_w_q[i], peer_keys[i], peer_u[i], peer_v[i])
    return final_norm(xf, norm_final).reshape(bsz, t_len, d)
```

```python
import functools
import math

import jax
import jax.numpy as jnp
import numpy as np
from jax import lax
from jax.experimental import pallas as pl
from jax.experimental.pallas import tpu as pltpu

F32 = jnp.float32
BF16 = jnp.bfloat16
NEG = -1e30

D_MODEL = 1024
SEQ = 2048
GRID_W = 64
CONV_K = 5
NORM_EPS = 1e-6

GDN_HEADS = 8
GDN_DK = 128
GDN_CHUNK = 64
GDN_QK = 1024
GDN_V = 1024
GDN_CONV_CH = 3072

SSD_INNER = 2048
SSD_HEADDIM = 64
SSD_HEADS = 32
SSD_GROUPS = 4
SSD_STATE = 128
SSD_CHUNK = 64
SSD_BC = 512
SSD_CONV_CH = 3072

SWA_HEADS = 16
SWA_KV_HEADS = 4
SWA_HEADDIM = 64
SWA_WINDOW = 128
ROPE_THETA = 10000.0

NA_HEADS = 16
NA_HEADDIM = 64
NA_ROWS = 8
NA_COLS = 16
NA_WIDTH = 1024

PEER_HEADS = 8
PEER_NKEYS = 128
PEER_QDIM = 256
PEER_HALF = 128
PEER_TOPK = 16

GDN_PHASE_A_CHUNKS = 8
SSD_PHASE_A_CHUNKS = 4

PEER_EXPERT_ROWS = 8
PEER_Z_CHUNKS = 4

NA_ROW_UNROLL = 4

BF16_SUBLANES = 16
F32_SUBLANES = 8

CONV_ROW_CHUNK = 256

VMEM_LIMIT_BYTES = 52 * 1024 * 1024


def _cparams(*sem, flags=None):
    return pltpu.CompilerParams(dimension_semantics=sem, vmem_limit_bytes=VMEM_LIMIT_BYTES, flags=flags)


def _norm_matmul_body(x_ref, g_ref, w_ref, o_ref, hn_ref):
    @pl.when(pl.program_id(1) == 0)
    def _():
        x = x_ref[...]
        ms = jnp.mean(x * x, axis=-1, keepdims=True)
        hn_ref[...] = (x * lax.rsqrt(ms + NORM_EPS) * g_ref[...]).astype(BF16)

    o_ref[...] = jnp.dot(hn_ref[...], w_ref[...], preferred_element_type=F32).astype(o_ref.dtype)


def norm_matmul(x, gain, w, out_dtype, tm=1024, tn=1024):
    m, d = x.shape
    n = w.shape[1]
    tn = min(tn, n)
    while n % tn:
        tn //= 2
    assert m % tm == 0 and tn % 128 == 0
    return pl.pallas_call(
        _norm_matmul_body,
        grid=(m // tm, n // tn),
        in_specs=[
            pl.BlockSpec((tm, d), lambda i, j: (i, 0)),
            pl.BlockSpec((1, d), lambda i, j: (0, 0)),
            pl.BlockSpec((d, tn), lambda i, j: (0, j)),
        ],
        out_specs=pl.BlockSpec((tm, tn), lambda i, j: (i, j)),
        out_shape=jax.ShapeDtypeStruct((m, n), out_dtype),
        scratch_shapes=[pltpu.VMEM((tm, d), BF16)],
        compiler_params=_cparams("parallel", "arbitrary"),
        name="norm_matmul",
    )(x, gain.reshape(1, d), w)


def _matmul_residual_body(x_ref, y_ref, w_ref, o_ref):
    o_ref[...] = x_ref[...] + jnp.dot(y_ref[...], w_ref[...], preferred_element_type=F32)


def matmul_residual(x, y, w, tm=1024):
    m, d = x.shape
    k = y.shape[1]
    return pl.pallas_call(
        _matmul_residual_body,
        grid=(m // tm,),
        in_specs=[
            pl.BlockSpec((tm, d), lambda i: (i, 0)),
            pl.BlockSpec((tm, k), lambda i: (i, 0)),
            pl.BlockSpec((k, d), lambda i: (0, 0)),
        ],
        out_specs=pl.BlockSpec((tm, d), lambda i: (i, 0)),
        out_shape=jax.ShapeDtypeStruct((m, d), F32),
        compiler_params=_cparams("parallel"),
        name="matmul_residual",
    )(x, y, w)


def _final_norm_body(x_ref, g_ref, o_ref):
    x = x_ref[...]
    ms = jnp.mean(x * x, axis=-1, keepdims=True)
    o_ref[...] = x * lax.rsqrt(ms + NORM_EPS) * g_ref[...]


def final_norm(x, gain, tm=1024):
    m, d = x.shape
    return pl.pallas_call(
        _final_norm_body,
        grid=(m // tm,),
        in_specs=[pl.BlockSpec((tm, d), lambda i: (i, 0)), pl.BlockSpec((1, d), lambda i: (0, 0))],
        out_specs=pl.BlockSpec((tm, d), lambda i: (i, 0)),
        out_shape=jax.ShapeDtypeStruct((m, d), F32),
        compiler_params=_cparams("parallel"),
        name="final_norm",
    )(x, gain.reshape(1, d))


def _top16_desc(s, with_rank):
    work = s
    rank = jnp.full(s.shape, 99.0, F32) if with_rank else None
    vals = []
    for r in range(PEER_TOPK):
        m = jnp.max(work, axis=0, keepdims=True)
        eq = work == m
        if with_rank:
            rank = jnp.where(eq, float(r + 1), rank)
        work = jnp.where(eq, -jnp.inf, work)
        vals.append(m)
    return jnp.concatenate(vals, axis=0), rank


def _peer_route_body(x_ref, g_ref, wqT_ref, keys_ref, hnT_ref, n_ref, e0_ref, r1_ref, e1_ref):
    x = x_ref[...]
    ms = jnp.mean(x * x, axis=-1, keepdims=True)
    hn = x * lax.rsqrt(ms + NORM_EPS) * g_ref[...]
    hnT = hn.T.astype(BF16)
    hnT_ref[...] = hnT
    qT = jnp.dot(wqT_ref[...], hnT, preferred_element_type=F32)
    row8 = lax.broadcasted_iota(jnp.int32, (8, x.shape[0]), 0)
    for h in range(PEER_HEADS):
        s = []
        for p in range(2):
            hp = 2 * h + p
            q_hp = qT[hp * PEER_HALF:(hp + 1) * PEER_HALF, :].astype(BF16)
            s.append(jnp.dot(keys_ref[hp], q_hp, preferred_element_type=F32))
        a0, _ = _top16_desc(s[0], False)
        b, rank1 = _top16_desc(s[1], True)
        cand = [a0[0:1, :] + b]
        for p in range(1, PEER_TOPK):
            cnt = PEER_TOPK // (p + 1)
            c = a0[p:p + 1, :] + b[0:8, :]
            cand.append(c if cnt >= 8 else jnp.where(row8 < cnt, c, -jnp.inf))
        best, _ = _top16_desc(jnp.concatenate(cand, axis=0), False)
        tau = best[PEER_TOPK - 1:PEER_TOPK, :]
        z = jnp.sum(jnp.exp(best - best[0:1, :]), axis=0, keepdims=True)
        n_map = jnp.zeros_like(s[0])
        for p in range(PEER_TOPK):
            n_p = jnp.sum(jnp.where(cand[p] >= tau, 1.0, 0.0), axis=0, keepdims=True)
            n_map = jnp.where(s[0] == a0[p:p + 1, :], n_p, n_map)
        n_ref[h] = n_map
        e0_ref[h] = jnp.exp(s[0] - a0[0:1, :]) * (0.5 / z)
        r1_ref[h] = rank1.astype(BF16)
        e1_ref[h] = jnp.exp(s[1] - b[0:1, :]).astype(BF16)


def peer_route(x, gain, wqT, keys, tm=256):
    m, d = x.shape
    nq = wqT.shape[0]
    tab = jax.ShapeDtypeStruct((PEER_HEADS, PEER_NKEYS, m), F32)
    tab16 = jax.ShapeDtypeStruct((PEER_HEADS, PEER_NKEYS, m), BF16)
    tab_spec = pl.BlockSpec((PEER_HEADS, PEER_NKEYS, tm), lambda i: (0, 0, i))
    return pl.pallas_call(
        _peer_route_body,
        grid=(m // tm,),
        in_specs=[
            pl.BlockSpec((tm, d), lambda i: (i, 0)),
            pl.BlockSpec((1, d), lambda i: (0, 0)),
            pl.BlockSpec((nq, d), lambda i: (0, 0)),
            pl.BlockSpec(keys.shape, lambda i: (0, 0, 0)),
        ],
        out_specs=[pl.BlockSpec((d, tm), lambda i: (0, i)), tab_spec, tab_spec, tab_spec, tab_spec],
        out_shape=[jax.ShapeDtypeStruct((d, m), BF16), tab, tab, tab16, tab16],
        compiler_params=_cparams("parallel"),
        name="peer_route",
    )(x, gain.reshape(1, d), wqT, keys)


def _peer_expert_body(x_ref, hnT_ref, n_ref, e0_ref, r1_ref, e1_ref, u_ref, vT_ref, o_ref,
                      a_s, acc_ref, *, ni, nb):
    j = pl.program_id(1)

    @pl.when(j == 0)
    def _():
        a_s[...] = jnp.zeros_like(a_s)
        acc_ref[...] = jnp.zeros_like(acc_ref)

    ib = jnp.minimum(j, nb - 1)
    tm = a_s.shape[2]
    pk = BF16_SUBLANES
    nk = PEER_NKEYS
    cur = j % 2
    prev = 1 - cur

    zc = ni * nk // PEER_Z_CHUNKS
    z_chunks = [jnp.dot(u_ref[pl.ds(c * zc, zc), :], hnT_ref[...], preferred_element_type=F32)
                for c in range(PEER_Z_CHUNKS)]
    acc_ref[...] += jnp.dot(vT_ref[0], a_s[prev], preferred_element_type=F32)
    for ii in range(ni):
        i = ib * ni + ii
        r0 = ii * nk - (ii * nk // zc) * zc
        z = z_chunks[ii * nk // zc][r0:r0 + nk, :].astype(BF16)
        act = z * (1.0 + lax.erf(z * (1.0 / math.sqrt(2.0))))
        gate = None
        for h in range(PEER_HEADS):
            n_b = jnp.broadcast_to(n_ref[h, pl.ds(i, 1), :], (pk, tm)).astype(BF16)[None]
            e0_b = jnp.broadcast_to(e0_ref[h, pl.ds(i, 1), :], (pk, tm)).astype(BF16)[None]
            r1 = r1_ref[h].reshape(nk // pk, pk, tm)
            e1 = e1_ref[h].reshape(nk // pk, pk, tm)
            term = jnp.where(r1 <= n_b, e1 * e0_b, jnp.zeros((), BF16))
            gate = term if gate is None else gate + term
        a_s[cur, pl.ds(ii * nk, nk), :] = (act.reshape(nk // pk, pk, tm) * gate).reshape(nk, tm)

    @pl.when(j == nb)
    def _():
        o_ref[...] = x_ref[...] + acc_ref[...].T


def peer_experts(x, hnT, n_tab, e0_tab, r1_tab, e1_tab, u, v_tab, tm=512, ni=PEER_EXPERT_ROWS):
    m, d = x.shape
    ne = u.shape[0]
    et = ni * PEER_NKEYS
    nb = ne // et
    vT = jnp.transpose(v_tab.astype(BF16).reshape(nb, et, d), (0, 2, 1))
    tab_spec = pl.BlockSpec((PEER_HEADS, PEER_NKEYS, tm), lambda i, j: (0, 0, i))
    return pl.pallas_call(
        functools.partial(_peer_expert_body, ni=ni, nb=nb),
        grid=(m // tm, nb + 1),
        in_specs=[
            pl.BlockSpec((tm, d), lambda i, j: (i, 0)),
            pl.BlockSpec((d, tm), lambda i, j: (0, i)),
            tab_spec, tab_spec, tab_spec, tab_spec,
            pl.BlockSpec((et, d), lambda i, j: (jnp.minimum(j, nb - 1), 0)),
            pl.BlockSpec((1, d, et), lambda i, j: (jnp.clip(j - 1, 0, nb - 1), 0, 0)),
        ],
        out_specs=pl.BlockSpec((tm, d), lambda i, j: (i, 0)),
        out_shape=jax.ShapeDtypeStruct((m, d), F32),
        scratch_shapes=[pltpu.VMEM((2, et, tm), BF16), pltpu.VMEM((d, tm), F32)],
        compiler_params=_cparams("parallel", "arbitrary"),
        name="peer_experts",
    )(x, hnT, n_tab, e0_tab, r1_tab, e1_tab, u, vT)


def peer_layer(x, gain, w_q, keys, u_tab, v_tab):
    wqT = w_q.T.astype(BF16)
    keys2 = keys.reshape(PEER_HEADS * 2, PEER_NKEYS, PEER_HALF).astype(BF16)
    hnT, n_tab, e0_tab, r1_tab, e1_tab = peer_route(x, gain, wqT, keys2)
    return peer_experts(x, hnT, n_tab, e0_tab, r1_tab, e1_tab, u_tab.astype(BF16), v_tab)


def _rope_lanes(t, cos, sin_signed):
    half = SWA_HEADDIM // 2
    lane = lax.broadcasted_iota(jnp.int32, t.shape, 1)
    first = (lane % SWA_HEADDIM) < half
    partner = jnp.where(first, pltpu.roll(t, 128 - half, 1), pltpu.roll(t, half, 1))
    return t * cos + partner * sin_signed


def _swa_body(q_ref, k_ref, v_ref, cos_ref, sin_ref, sink_ref, o_ref, kp_ref, vp_ref):
    w = SWA_WINDOW
    t_len = q_ref.shape[1]
    nkv = SWA_KV_HEADS * SWA_HEADDIM
    rep = SWA_HEADS // SWA_KV_HEADS
    zeros = jnp.zeros((w, nkv), BF16)
    kp_ref[pl.ds(0, w), :] = zeros
    kp_ref[pl.ds(w + t_len, w), :] = zeros
    vp_ref[pl.ds(0, w), :] = zeros
    vp_ref[pl.ds(w + t_len, w), :] = zeros
    vp_ref[pl.ds(w, t_len), :] = v_ref[0]
    cos_all = cos_ref[...]
    sin_all = sin_ref[...]
    for c in range(nkv // 128):
        kc = k_ref[0, :, c * 128:(c + 1) * 128].astype(F32)
        kp_ref[pl.ds(w, t_len), c * 128:(c + 1) * 128] = _rope_lanes(kc, cos_all, sin_all).astype(BF16)

    row = lax.broadcasted_iota(jnp.int32, (w, 3 * w), 0)
    col = lax.broadcasted_iota(jnp.int32, (w, 3 * w), 1)
    band = (col >= row) & (col <= row + 2 * w)

    def block(n, carry):
        base = pl.multiple_of(n * w, w)
        cos_b = cos_ref[pl.ds(base, w), :]
        sin_b = sin_ref[pl.ds(base, w), :]
        kpos = base - w + col
        valid = band & (kpos >= 0) & (kpos < t_len)
        outs = []
        for g in range(SWA_KV_HEADS):
            kw = kp_ref[pl.ds(base, 3 * w), g * SWA_HEADDIM:(g + 1) * SWA_HEADDIM]
            vw = vp_ref[pl.ds(base, 3 * w), g * SWA_HEADDIM:(g + 1) * SWA_HEADDIM]
            scores = []
            for c in range(g * rep // 2, (g + 1) * rep // 2):
                qc = q_ref[0, pl.ds(base, w), c * 128:(c + 1) * 128].astype(F32)
                qc = (_rope_lanes(qc, cos_b, sin_b) * (SWA_HEADDIM ** -0.5)).astype(BF16)
                for hh in range(2):
                    qh = qc[:, hh * SWA_HEADDIM:(hh + 1) * SWA_HEADDIM]
                    scores.append(lax.dot_general(qh, kw, (((1,), (1,)), ((), ())), preferred_element_type=F32))
            probs = []
            for k, s in enumerate(scores):
                s = jnp.where(valid, s, NEG)
                sk = sink_ref[g * rep + k]
                m = jnp.maximum(jnp.max(s, axis=-1, keepdims=True), sk)
                p = jnp.exp(s - m)
                inv = 1.0 / (jnp.sum(p, axis=-1, keepdims=True) + jnp.exp(sk - m))
                probs.append((p.astype(BF16), inv))
            for p, inv in probs:
                outs.append(jnp.dot(p, vw, preferred_element_type=F32) * inv)
        o_ref[0, pl.ds(base, w), :] = jnp.concatenate(outs, axis=-1).astype(o_ref.dtype)
        return carry

    lax.fori_loop(0, t_len // w, block, 0)


def swa_attention(proj, sink, bsz, t_len):
    half = SWA_HEADDIM // 2
    inv_freq = ROPE_THETA ** (-jnp.arange(half, dtype=F32) / half)
    ang = jnp.arange(t_len, dtype=F32)[:, None] * inv_freq[None, :]
    cos, sin = jnp.cos(ang), jnp.sin(ang)
    cos_t = jnp.tile(jnp.concatenate([cos, cos], axis=-1), (1, 2))
    sin_t = jnp.tile(jnp.concatenate([-sin, sin], axis=-1), (1, 2))
    nq = SWA_HEADS * SWA_HEADDIM
    nkv = SWA_KV_HEADS * SWA_HEADDIM
    return pl.pallas_call(
        _swa_body,
        grid=(bsz,),
        in_specs=[
            pl.BlockSpec((1, t_len, nq), lambda b: (b, 0, 0)),
            pl.BlockSpec((1, t_len, nkv), lambda b: (b, 0, nq // nkv)),
            pl.BlockSpec((1, t_len, nkv), lambda b: (b, 0, nq // nkv + 1)),
            pl.BlockSpec((t_len, 128), lambda b: (0, 0)),
            pl.BlockSpec((t_len, 128), lambda b: (0, 0)),
            pl.BlockSpec(memory_space=pltpu.SMEM),
        ],
        out_specs=pl.BlockSpec((1, t_len, nq), lambda b: (b, 0, 0)),
        out_shape=jax.ShapeDtypeStruct((bsz, t_len, nq), BF16),
        scratch_shapes=[pltpu.VMEM((t_len + 2 * SWA_WINDOW, nkv), BF16),
                        pltpu.VMEM((t_len + 2 * SWA_WINDOW, nkv), BF16)],
        compiler_params=_cparams("parallel"),
        name="swa_attention",
    )(proj, proj, proj, cos_t, sin_t, sink.astype(F32))


def _na_bias_table(rpb):
    qc = np.arange(GRID_W)[:, None]
    kc = np.arange(GRID_W)[None, :]
    cstart = np.clip(qc - NA_COLS // 2, 0, GRID_W - NA_COLS)
    valid = (kc >= cstart) & (kc < cstart + NA_COLS)
    cidx = np.clip(kc - qc + NA_COLS - 1, 0, 2 * NA_COLS - 2)
    onehot = (np.arange(2 * NA_COLS - 1)[:, None, None] == cidx[None]).astype(np.float32)
    toep = jnp.einsum('hrc,cqk->hrqk', rpb.astype(F32), onehot, precision=lax.Precision.HIGHEST)
    toep = jnp.where(valid[None, None], toep, NEG)
    return jnp.concatenate([toep[:, :-1], toep[:, 1:]], axis=-1)


def _na_body(q_ref, k_ref, v_ref, bias_ref, o_ref):
    t_len = q_ref.shape[1]
    rows = t_len // GRID_W
    win = NA_ROWS * GRID_W

    def row_group(gi, carry):
        chains = []
        for rr in range(NA_ROW_UNROLL):
            r = gi * NA_ROW_UNROLL + rr
            rs = jnp.clip(r - NA_ROWS // 2, 0, rows - NA_ROWS)
            d0 = rs - r + NA_ROWS - 1
            qbase = pl.multiple_of(r * GRID_W, GRID_W)
            kbase = pl.multiple_of(rs * GRID_W, GRID_W)
            qr = q_ref[0, pl.ds(qbase, GRID_W), :]
            kw = k_ref[0, pl.ds(kbase, win), :]
            for hh in range(2):
                sl = slice(hh * NA_HEADDIM, (hh + 1) * NA_HEADDIM)
                s = lax.dot_general(qr[:, sl], kw[:, sl], (((1,), (1,)), ((), ())), preferred_element_type=F32)
                chains.append((hh, d0, kbase, s))
        probs = []
        for hh, d0, kbase, s in chains:
            bias = jnp.concatenate([bias_ref[hh, d0 + 2 * c] for c in range(NA_ROWS // 2)], axis=-1)
            s = s * (NA_HEADDIM ** -0.5) + bias
            m = jnp.max(s, axis=-1, keepdims=True)
            p = jnp.exp(s - m)
            probs.append((p.astype(BF16), 1.0 / jnp.sum(p, axis=-1, keepdims=True)))
        outs = []
        for (hh, d0, kbase, s), (p, inv) in zip(chains, probs):
            vw = v_ref[0, pl.ds(kbase, win), hh * NA_HEADDIM:(hh + 1) * NA_HEADDIM]
            outs.append(jnp.dot(p, vw, preferred_element_type=F32) * inv)
        for rr in range(NA_ROW_UNROLL):
            qbase = pl.multiple_of((gi * NA_ROW_UNROLL + rr) * GRID_W, GRID_W)
            o_ref[0, pl.ds(qbase, GRID_W), :] = jnp.concatenate(outs[2 * rr:2 * rr + 2], axis=-1).astype(o_ref.dtype)
        return carry

    lax.fori_loop(0, rows // NA_ROW_UNROLL, row_group, 0)


def na_attention(proj, rpb, bsz, t_len):
    bias = _na_bias_table(rpb)
    npair = NA_HEADS // 2
    return pl.pallas_call(
        _na_body,
        grid=(npair, bsz),
        in_specs=[
            pl.BlockSpec((1, t_len, 128), lambda hp, b: (b, 0, hp)),
            pl.BlockSpec((1, t_len, 128), lambda hp, b: (b, 0, npair + hp)),
            pl.BlockSpec((1, t_len, 128), lambda hp, b: (b, 0, 2 * npair + hp)),
            pl.BlockSpec((2, 2 * NA_ROWS - 2, GRID_W, 2 * GRID_W), lambda hp, b: (hp, 0, 0, 0)),
        ],
        out_specs=pl.BlockSpec((1, t_len, 128), lambda hp, b: (b, 0, hp)),
        out_shape=jax.ShapeDtypeStruct((bsz, t_len, NA_WIDTH), BF16),
        compiler_params=_cparams("parallel", "parallel"),
        name="na_attention",
    )(proj, proj, proj, bias)


def _conv_silu_body(x_ref, w_ref, b_ref, o_ref, *, n_l2):
    t_len, tc = x_ref.shape[1], x_ref.shape[2]
    rc = CONV_ROW_CHUNK
    halo = F32_SUBLANES
    normalise = pl.program_id(1) < n_l2

    def chunk(ci, carry):
        r0 = pl.multiple_of(ci * rc, rc)
        cur = x_ref[0, pl.ds(r0, rc), :].astype(F32)
        lo = pl.multiple_of(jnp.maximum(r0 - halo, 0), halo)
        hi = pl.multiple_of(jnp.minimum(r0 + rc, t_len - halo), halo)
        before = jnp.where(r0 > 0, x_ref[0, pl.ds(lo, halo), :].astype(F32), 0.0)
        after = jnp.where(r0 + rc < t_len, x_ref[0, pl.ds(hi, halo), :].astype(F32), 0.0)
        xx = jnp.concatenate([before, cur, after], axis=0)
        acc = cur * w_ref[CONV_K // 2:CONV_K // 2 + 1, :] + b_ref[...]
        for k in range(CONV_K):
            off = k - CONV_K // 2
            if off == 0:
                continue
            acc = acc + xx[halo + off:halo + off + rc, :] * w_ref[k:k + 1, :]
        y = acc * jax.nn.sigmoid(acc)
        if n_l2 > 0:
            parts = []
            for c in range(tc // 128):
                yc = y[:, c * 128:(c + 1) * 128]
                ss = jnp.sum(yc * yc, axis=-1, keepdims=True)
                parts.append(yc * jnp.where(normalise, lax.rsqrt(ss + NORM_EPS), 1.0))
            y = jnp.concatenate(parts, axis=-1)
        o_ref[0, pl.ds(r0, rc), :] = y.astype(o_ref.dtype)
        return carry

    lax.fori_loop(0, t_len // rc, chunk, 0)


def conv_silu(proj, w, bias, col0, n_ch, n_l2=0, tc=512):
    bsz, t_len, _ = proj.shape
    assert col0 % tc == 0 and n_ch % tc == 0
    c0 = col0 // tc
    return pl.pallas_call(
        functools.partial(_conv_silu_body, n_l2=n_l2),
        grid=(bsz, n_ch // tc),
        in_specs=[
            pl.BlockSpec((1, t_len, tc), lambda b, j: (b, 0, c0 + j)),
            pl.BlockSpec((CONV_K, tc), lambda b, j: (0, j)),
            pl.BlockSpec((1, tc), lambda b, j: (0, j)),
        ],
        out_specs=pl.BlockSpec((1, t_len, tc), lambda b, j: (b, 0, j)),
        out_shape=jax.ShapeDtypeStruct((bsz, t_len, n_ch), BF16),
        compiler_params=_cparams("parallel", "parallel"),
        name="conv_silu",
    )(proj, w.astype(F32), bias.astype(F32).reshape(1, n_ch))


def _softplus(x):
    return jnp.maximum(x, 0.0) + jnp.log1p(jnp.exp(-jnp.abs(x)))


def _bmm(a, b, precision=None):
    return lax.dot_general(a, b, (((2,), (1,)), ((0,), (0,))), precision=precision,
                           preferred_element_type=F32)


def _split_bf16(a):
    hi = a.astype(BF16)
    return hi, (a - hi.astype(F32)).astype(BF16)


def _bmm_split(a, b):
    return _bmm(a[0], b[0]) + _bmm(a[0], b[1]) + _bmm(a[1], b[0])


def _bmm_nt(a, b):
    return lax.dot_general(a, b, (((2,), (2,)), ((0,), (0,))), preferred_element_type=F32)


def _gdn_body(q_ref, k_ref, v_ref, z_ref, gcol_ref, grow_ref, alog_ref, dtb_ref, nw_ref, o_ref,
              u_s, w_s, qk_s, qg_s, kg_s, el_s, o_s):
    c_len = GDN_CHUNK
    t_len = q_ref.shape[1]
    nc = t_len // c_len
    h = pl.program_id(1)
    ii = lax.broadcasted_iota(jnp.int32, (c_len, c_len), 0)
    jj = lax.broadcasted_iota(jnp.int32, (c_len, c_len), 1)
    eye = (ii == jj).astype(F32)
    cg = GDN_PHASE_A_CHUNKS

    def phase_a(gi, carry):
        c0 = pl.multiple_of(gi * cg, cg)
        rows = pl.ds(pl.multiple_of(gi * (cg * c_len), cg * c_len), cg * c_len)
        k3 = k_ref[0, rows, :].reshape(cg, c_len, GDN_DK)
        v3 = v_ref[0, rows, :].reshape(cg, c_len, GDN_DK).astype(F32)
        kf = k3.astype(F32)
        qs = q_ref[0, rows, :].reshape(cg, c_len, GDN_DK).astype(F32) * (GDN_DK ** -0.5)
        gcol = gcol_ref[0, 0, pl.ds(c0, cg)]
        grow = grow_ref[0, 0, pl.ds(c0, cg)]
        qk_raw = _bmm_nt(qs.astype(BF16), k3)
        kb, gc_c, g_last, decay, strict = [], [], [], [], []
        for d in range(2):
            incl = (ii >= jj) if d == 0 else (ii <= jj)
            strict.append((ii > jj) if d == 0 else (ii < jj))
            tri = incl.astype(F32)
            tri_t = ((ii <= jj) if d == 0 else (ii >= jj)).astype(F32)
            neg_a = -jnp.exp(jnp.full((1, 1, 1), alog_ref[d, h], F32))
            dtb = dtb_ref[d, h]
            g_c = neg_a * _softplus(gcol[:, :, d:d + 1] + dtb)
            g_r = neg_a * _softplus(grow[:, d:d + 1, :] + dtb)
            beta_c = jax.nn.sigmoid(gcol[:, :, 2 + d:3 + d])
            gc_c.append(jnp.sum(tri[None] * g_r, axis=2, keepdims=True))
            gc_r = jnp.sum(tri_t[None] * g_c, axis=1, keepdims=True)
            g_last.append(jnp.sum(g_r, axis=2, keepdims=True))
            decay.append(jnp.where(incl[None], jnp.exp(jnp.where(incl[None], gc_c[d] - gc_r, 0.0)), 0.0))
            kb.append((kf * beta_c, v3 * beta_c))
        kk = [_bmm_nt(kb[d][0].astype(BF16), k3) for d in range(2)]
        low = [jnp.where(strict[d][None], kk[d] * decay[d], 0.0) for d in range(2)]
        inv = [eye[None] - low[d] for d in range(2)]
        pw = [_split_bf16(low[d]) for d in range(2)]
        for _ in range(5):
            pw = [_split_bf16(_bmm_split(pw[d], pw[d])) for d in range(2)]
            inv = [inv[d] + _bmm_split(_split_bf16(inv[d]), pw[d]) for d in range(2)]
        rhs = [jnp.concatenate([kb[d][1], kb[d][0] * jnp.exp(gc_c[d])], axis=-1) for d in range(2)]
        sol = [_bmm_split(_split_bf16(inv[d]), _split_bf16(rhs[d])) for d in range(2)]
        for d in range(2):
            u_s[d, pl.ds(c0, cg)] = sol[d][:, :, :GDN_DK]
            w_s[d, pl.ds(c0, cg)] = sol[d][:, :, GDN_DK:].astype(BF16)
            qk_s[d, pl.ds(c0, cg)] = (qk_raw * decay[d]).astype(BF16)
            qg_s[d, pl.ds(c0, cg)] = (qs * jnp.exp(gc_c[d])).astype(BF16)
            kg_s[d, pl.ds(c0, cg)] = (kf * jnp.exp(g_last[d] - gc_c[d])).astype(BF16)
            el_s[d, pl.ds(c0, cg)] = jnp.broadcast_to(jnp.exp(g_last[d]), (cg, 1, GDN_DK))
        return carry

    lax.fori_loop(0, nc // cg, phase_a, 0)

    def step(t, carry):
        cs = (t, nc - 1 - t)
        sb = [carry[d].astype(BF16) for d in range(2)]
        ws = [jnp.dot(w_s[d, cs[d]], sb[d], preferred_element_type=F32) for d in range(2)]
        qs_ = [jnp.dot(qg_s[d, cs[d]], sb[d], preferred_element_type=F32) for d in range(2)]
        vb = [(u_s[d, cs[d]] - ws[d]).astype(BF16) for d in range(2)]
        os_ = [qs_[d] + jnp.dot(qk_s[d, cs[d]], vb[d], preferred_element_type=F32) for d in range(2)]
        kv = [lax.dot_general(kg_s[d, cs[d]], vb[d], (((0,), (0,)), ((), ())), preferred_element_type=F32)
              for d in range(2)]
        for d in range(2):
            o_s[d, cs[d]] = os_[d]
        return tuple(carry[d] * el_s[d, cs[d]] + kv[d] for d in range(2))

    s0 = jnp.zeros((GDN_DK, GDN_DK), F32)
    lax.fori_loop(0, nc, step, (s0, s0))

    o = (o_s[0] + o_s[1]).reshape(t_len, GDN_DK)
    ms = jnp.mean(o * o, axis=-1, keepdims=True)
    z = z_ref[0]
    y = o * lax.rsqrt(ms + NORM_EPS) * nw_ref[...] * (z * jax.nn.sigmoid(z))
    o_ref[0] = y.astype(o_ref.dtype)


def gdn_scan(qkv, proj, gates, a_log, dt_bias, norm_w, bsz, t_len):
    nc = t_len // GDN_CHUNK
    nh = GDN_HEADS
    g4 = gates[:, :4 * nh].reshape(bsz, nc, GDN_CHUNK, 4, nh)
    gcol = jnp.transpose(g4, (0, 4, 1, 2, 3))
    grow = jnp.transpose(g4, (0, 4, 1, 3, 2))
    dk = GDN_DK
    nq = GDN_QK // dk
    return pl.pallas_call(
        _gdn_body,
        grid=(bsz, nh),
        in_specs=[
            pl.BlockSpec((1, t_len, dk), lambda b, h: (b, 0, h)),
            pl.BlockSpec((1, t_len, dk), lambda b, h: (b, 0, nq + h)),
            pl.BlockSpec((1, t_len, dk), lambda b, h: (b, 0, 2 * nq + h)),
            pl.BlockSpec((1, t_len, dk), lambda b, h: (b, 0, 3 * nq + h)),
            pl.BlockSpec((1, 1, nc, GDN_CHUNK, 4), lambda b, h: (b, h, 0, 0, 0)),
            pl.BlockSpec((1, 1, nc, 4, GDN_CHUNK), lambda b, h: (b, h, 0, 0, 0)),
            pl.BlockSpec(memory_space=pltpu.SMEM),
            pl.BlockSpec(memory_space=pltpu.SMEM),
            pl.BlockSpec((1, dk), lambda b, h: (0, 0)),
        ],
        out_specs=pl.BlockSpec((1, t_len, dk), lambda b, h: (b, 0, h)),
        out_shape=jax.ShapeDtypeStruct((bsz, t_len, GDN_V), BF16),
        scratch_shapes=[
            pltpu.VMEM((2, nc, GDN_CHUNK, dk), F32),
            pltpu.VMEM((2, nc, GDN_CHUNK, dk), BF16),
            pltpu.VMEM((2, nc, GDN_CHUNK, GDN_CHUNK), BF16),
            pltpu.VMEM((2, nc, GDN_CHUNK, dk), BF16),
            pltpu.VMEM((2, nc, GDN_CHUNK, dk), BF16),
            pltpu.VMEM((2, nc, 1, dk), F32),
            pltpu.VMEM((2, nc, GDN_CHUNK, dk), F32),
        ],
        compiler_params=_cparams("parallel", "parallel"),
        name="gdn_scan",
    )(qkv, qkv, qkv, proj, gcol, grow, a_log.astype(F32), dt_bias.astype(F32),
      norm_w.astype(F32).reshape(1, dk))


def _pad_cols(w, n):
    return jnp.pad(w, ((0, 0), (0, n - w.shape[1])))


def gdn_layer(x, gain, w_in, conv_w, a_log, dt_bias, norm_w, w_out, bsz, t_len):
    n_main = GDN_CONV_CH + GDN_V
    proj = norm_matmul(x, gain, w_in[:, :n_main].astype(BF16), F32)
    gates = norm_matmul(x, gain, _pad_cols(w_in[:, n_main:], 128).astype(BF16), F32)
    proj3 = proj.reshape(bsz, t_len, n_main)
    qkv = conv_silu(proj3, conv_w, jnp.zeros((GDN_CONV_CH,), F32), 0, GDN_CONV_CH,
                    n_l2=2 * GDN_QK // 512)
    y = gdn_scan(qkv, proj3, gates, a_log, dt_bias, norm_w, bsz, t_len)
    return matmul_residual(x, y.reshape(bsz * t_len, GDN_V), w_out.astype(BF16))


def _ssd_body(xs_ref, b_ref, c_ref, z_ref, dcol_ref, drow_ref, alog_ref, dtbc_ref, dtbr_ref, dskip_ref, nw_ref, o_ref,
              xw_s, ea_s, el_s, y_s, st_s):
    c_len = SSD_CHUNK
    t_len = xs_ref.shape[1]
    nc = t_len // c_len
    nr = SSD_HEADS // SSD_GROUPS
    hp = SSD_HEADDIM
    g = pl.program_id(1)
    ii = lax.broadcasted_iota(jnp.int32, (c_len, c_len), 0)
    jj = lax.broadcasted_iota(jnp.int32, (c_len, c_len), 1)
    cg = SSD_PHASE_A_CHUNKS

    def phase_a(gi, carry):
        c0 = pl.multiple_of(gi * cg, cg)
        rows = pl.ds(pl.multiple_of(gi * (cg * c_len), cg * c_len), cg * c_len)
        x3 = xs_ref[0, rows, :].reshape(cg, c_len, nr * hp)
        b3 = b_ref[0, rows, :].reshape(cg, c_len, SSD_STATE)
        c3 = c_ref[0, rows, :].reshape(cg, c_len, SSD_STATE)
        dcol = dcol_ref[0, 0, pl.ds(c0, cg)]
        drow = drow_ref[0, 0, pl.ds(c0, cg)]
        cb = _bmm_nt(c3, b3)
        dt_c_all = _softplus(dcol + dtbc_ref[0])
        dt_r_all = _softplus(drow + dtbr_ref[0])
        ydiag = None
        for d in range(2):
            incl = (ii >= jj) if d == 0 else (ii <= jj)
            tri = incl.astype(F32)
            tri_t = ((ii <= jj) if d == 0 else (ii >= jj)).astype(F32)
            yd, xw, ea, el = [], [], [], []
            for r in range(nr):
                hidx = g * nr + r
                a = -jnp.exp(jnp.full((1, 1, 1), alog_ref[d, hidx], F32))
                col = d * nr + r
                dt_c = dt_c_all[:, :, col:col + 1]
                dt_r = dt_r_all[:, col:col + 1, :]
                ac_c = jnp.sum(tri[None] * (dt_r * a), axis=2, keepdims=True)
                ac_r = jnp.sum(tri_t[None] * (dt_c * a), axis=1, keepdims=True)
                a_last = jnp.sum(dt_r * a, axis=2, keepdims=True)
                seg = jnp.where(incl[None], jnp.exp(jnp.where(incl[None], ac_c - ac_r, 0.0)), 0.0)
                xr = x3[:, :, r * hp:(r + 1) * hp].astype(F32) * dt_c
                yd.append(_bmm((cb * seg).astype(BF16), xr.astype(BF16)))
                xw.append((xr * jnp.exp(a_last - ac_c)).astype(BF16))
                ea.append(jnp.broadcast_to(jnp.exp(ac_c), (cg, c_len, hp)))
                el.append(jnp.broadcast_to(jnp.exp(a_last), (cg, 1, hp)))
            yd = jnp.concatenate(yd, axis=-1)
            ydiag = yd if ydiag is None else ydiag + yd
            xw_s[d, pl.ds(c0, cg)] = jnp.concatenate(xw, axis=-1)
            ea_s[d, pl.ds(c0, cg)] = jnp.concatenate(ea, axis=-1)
            el_s[d, pl.ds(c0, cg)] = jnp.concatenate(el, axis=-1)
        y_s[pl.ds(c0, cg)] = ydiag
        return carry

    lax.fori_loop(0, nc // cg, phase_a, 0)

    st_s[...] = jnp.zeros_like(st_s)

    def step(t, carry):
        for d in range(2):
            c = t if d == 0 else nc - 1 - t
            base = pl.multiple_of(c * c_len, c_len)
            cc = c_ref[0, pl.ds(base, c_len), :]
            bb = b_ref[0, pl.ds(base, c_len), :]
            st = st_s[d]
            y_s[c] += jnp.dot(cc, st.astype(BF16), preferred_element_type=F32) * ea_s[d, c]
            st_s[d] = st * el_s[d, c] + lax.dot_general(bb, xw_s[d, c], (((0,), (0,)), ((), ())),
                                                        preferred_element_type=F32)
        return carry

    lax.fori_loop(0, nc, step, 0)

    xs = xs_ref[0].astype(F32)
    z = z_ref[0]
    y = (y_s[...].reshape(t_len, nr * hp) + dskip_ref[...] * xs) * (z * jax.nn.sigmoid(z))
    ms = jnp.mean(y * y, axis=-1, keepdims=True)
    o_ref[0] = (y * lax.rsqrt(ms + NORM_EPS) * nw_ref[...]).astype(o_ref.dtype)


def ssd_scan(xbc, proj, dts, a_log, dt_bias, d_skip, norm_w, bsz, t_len):
    nc = t_len // SSD_CHUNK
    ng = SSD_GROUPS
    nr = SSD_HEADS // ng
    gw = nr * SSD_HEADDIM
    d6 = dts[:, :2 * SSD_HEADS].reshape(bsz, nc, SSD_CHUNK, 2, ng, nr)
    dcol = jnp.transpose(d6, (0, 4, 1, 2, 3, 5)).reshape(bsz, ng, nc, SSD_CHUNK, 2 * nr)
    drow = jnp.transpose(d6, (0, 4, 1, 3, 5, 2)).reshape(bsz, ng, nc, 2 * nr, SSD_CHUNK)
    nb0 = SSD_INNER // SSD_STATE
    dtb_g = jnp.transpose(dt_bias.astype(F32).reshape(2, ng, nr), (1, 0, 2)).reshape(ng, 2 * nr)
    return pl.pallas_call(
        _ssd_body,
        grid=(bsz, ng),
        in_specs=[
            pl.BlockSpec((1, t_len, gw), lambda b, g: (b, 0, g)),
            pl.BlockSpec((1, t_len, SSD_STATE), lambda b, g: (b, 0, nb0 + g)),
            pl.BlockSpec((1, t_len, SSD_STATE), lambda b, g: (b, 0, nb0 + ng + g)),
            pl.BlockSpec((1, t_len, gw), lambda b, g: (b, 0, g)),
            pl.BlockSpec((1, 1, nc, SSD_CHUNK, 2 * nr), lambda b, g: (b, g, 0, 0, 0)),
            pl.BlockSpec((1, 1, nc, 2 * nr, SSD_CHUNK), lambda b, g: (b, g, 0, 0, 0)),
            pl.BlockSpec(memory_space=pltpu.SMEM),
            pl.BlockSpec((1, 1, 2 * nr), lambda b, g: (g, 0, 0)),
            pl.BlockSpec((1, 2 * nr, 1), lambda b, g: (g, 0, 0)),
            pl.BlockSpec((1, gw), lambda b, g: (0, g)),
            pl.BlockSpec((1, gw), lambda b, g: (0, g)),
        ],
        out_specs=pl.BlockSpec((1, t_len, gw), lambda b, g: (b, 0, g)),
        out_shape=jax.ShapeDtypeStruct((bsz, t_len, SSD_INNER), BF16),
        scratch_shapes=[
            pltpu.VMEM((2, nc, SSD_CHUNK, gw), BF16),
            pltpu.VMEM((2, nc, SSD_CHUNK, gw), F32),
            pltpu.VMEM((2, nc, 1, gw), F32),
            pltpu.VMEM((nc, SSD_CHUNK, gw), F32),
            pltpu.VMEM((2, SSD_STATE, gw), F32),
        ],
        compiler_params=_cparams("parallel", "parallel"),
        name="ssd_scan",
    )(xbc, xbc, xbc, proj, dcol, drow, a_log.astype(F32), dtb_g.reshape(ng, 1, 2 * nr),
      dtb_g.reshape(ng, 2 * nr, 1),
      jnp.repeat(d_skip.astype(F32), SSD_HEADDIM).reshape(1, SSD_INNER),
      norm_w.astype(F32).reshape(1, SSD_INNER))


def ssd_layer(x, gain, w_in, conv_w, conv_b, a_log, dt_bias, d_skip, norm_w, w_out, bsz, t_len):
    n_main = SSD_INNER + SSD_CONV_CH
    proj = norm_matmul(x, gain, w_in[:, :n_main].astype(BF16), F32)
    dts = norm_matmul(x, gain, _pad_cols(w_in[:, n_main:], 128).astype(BF16), F32)
    proj3 = proj.reshape(bsz, t_len, n_main)
    xbc = conv_silu(proj3, conv_w, conv_b, SSD_INNER, SSD_CONV_CH)
    y = ssd_scan(xbc, proj3, dts, a_log, dt_bias, d_skip, norm_w, bsz, t_len)
    return matmul_residual(x, y.reshape(bsz * t_len, SSD_INNER), w_out.astype(BF16))


def swa_layer(x, gain, w_in, sink, w_out, bsz, t_len):
    proj = norm_matmul(x, gain, w_in.astype(BF16), BF16)
    o = swa_attention(proj.reshape(bsz, t_len, -1), sink, bsz, t_len)
    return matmul_residual(x, o.reshape(bsz * t_len, -1), w_out.astype(BF16))


def na_layer(x, gain, w_in, rpb, w_out, bsz, t_len):
    proj = norm_matmul(x, gain, w_in.astype(BF16), BF16)
    o = na_attention(proj.reshape(bsz, t_len, -1), rpb, bsz, t_len)
    return matmul_residual(x, o.reshape(bsz * t_len, -1), w_out.astype(BF16))


def kernel(x, norm_mix, norm_ffn, norm_final, gdn_w_in, gdn_conv, gdn_a_log, gdn_dt_bias, gdn_norm, gdn_w_out, ssd_w_in, ssd_conv, ssd_conv_b, ssd_a_log, ssd_dt_bias, ssd_d, ssd_norm, ssd_w_out, swa_w_in, swa_sink, swa_w_out, na_w_in, na_rpb, na_w_out, peer_w_q, peer_keys, peer_u, peer_v):
    bsz, t_len, d = x.shape
    depth = norm_mix.shape[0]
    xf = x.reshape(bsz * t_len, d)
    for i in range(depth):
        mixer, j = i % 4, i // 4
        if mixer == 0:
            xf = gdn_layer(xf, norm_mix[i], gdn_w_in[j], gdn_conv[j], gdn_a_log[j], gdn_dt_bias[j],
                           gdn_norm[j], gdn_w_out[j], bsz, t_len)
        elif mixer == 1:
            xf = ssd_layer(xf, norm_mix[i], ssd_w_in[j], ssd_conv[j], ssd_conv_b[j], ssd_a_log[j],
                           ssd_dt_bias[j], ssd_d[j], ssd_norm[j], ssd_w_out[j], bsz, t_len)
        elif mixer == 2:
            xf = swa_layer(xf, norm_mix[i], swa_w_in[j], swa_sink[j], swa_w_out[j], bsz, t_len)
        else:
            xf = na_layer(xf, norm_mix[i], na_w_in[j], na_rpb[j], na_w_out[j], bsz, t_len)
        xf = peer_layer(xf, norm_ffn[i], peer_w_q[i], peer_keys[i], peer_u[i], peer_v[i])
    return final_norm(xf, norm_final).reshape(bsz, t_len, d)
```

```python
import functools
import math

import jax
import jax.numpy as jnp
import numpy as np
from jax import lax
from jax.experimental import pallas as pl
from jax.experimental.pallas import tpu as pltpu

F32 = jnp.float32
BF16 = jnp.bfloat16
NEG = -1e30

D_MODEL = 1024
SEQ = 2048
GRID_W = 64
CONV_K = 5
NORM_EPS = 1e-6

GDN_HEADS = 8
GDN_DK = 128
GDN_CHUNK = 64
GDN_QK = 1024
GDN_V = 1024
GDN_CONV_CH = 3072

SSD_INNER = 2048
SSD_HEADDIM = 64
SSD_HEADS = 32
SSD_GROUPS = 4
SSD_STATE = 128
SSD_CHUNK = 64
SSD_BC = 512
SSD_CONV_CH = 3072

SWA_HEADS = 16
SWA_KV_HEADS = 4
SWA_HEADDIM = 64
SWA_WINDOW = 128
ROPE_THETA = 10000.0

NA_HEADS = 16
NA_HEADDIM = 64
NA_ROWS = 8
NA_COLS = 16
NA_WIDTH = 1024

PEER_HEADS = 8
PEER_NKEYS = 128
PEER_QDIM = 256
PEER_HALF = 128
PEER_TOPK = 16

GDN_PHASE_A_CHUNKS = 8
SSD_PHASE_A_CHUNKS = 4

PEER_EXPERT_ROWS = 8
PEER_Z_CHUNKS = 4

NA_ROW_UNROLL = 4

BF16_SUBLANES = 16
F32_SUBLANES = 8

CONV_ROW_CHUNK = 256

VMEM_LIMIT_BYTES = 52 * 1024 * 1024


def _cparams(*sem, flags=None):
    return pltpu.CompilerParams(dimension_semantics=sem, vmem_limit_bytes=VMEM_LIMIT_BYTES, flags=flags)


def _norm_matmul_body(x_ref, g_ref, w_ref, o_ref, hn_ref):
    @pl.when(pl.program_id(1) == 0)
    def _():
        x = x_ref[...]
        ms = jnp.mean(x * x, axis=-1, keepdims=True)
        hn_ref[...] = (x * lax.rsqrt(ms + NORM_EPS) * g_ref[...]).astype(BF16)

    o_ref[...] = jnp.dot(hn_ref[...], w_ref[...], preferred_element_type=F32).astype(o_ref.dtype)


def norm_matmul(x, gain, w, out_dtype, tm=1024, tn=1024):
    m, d = x.shape
    n = w.shape[1]
    tn = min(tn, n)
    while n % tn:
        tn //= 2
    assert m % tm == 0 and tn % 128 == 0
    return pl.pallas_call(
        _norm_matmul_body,
        grid=(m // tm, n // tn),
        in_specs=[
            pl.BlockSpec((tm, d), lambda i, j: (i, 0)),
            pl.BlockSpec((1, d), lambda i, j: (0, 0)),
            pl.BlockSpec((d, tn), lambda i, j: (0, j)),
        ],
        out_specs=pl.BlockSpec((tm, tn), lambda i, j: (i, j)),
        out_shape=jax.ShapeDtypeStruct((m, n), out_dtype),
        scratch_shapes=[pltpu.VMEM((tm, d), BF16)],
        compiler_params=_cparams("parallel", "arbitrary"),
        name="norm_matmul",
    )(x, gain.reshape(1, d), w)


def _matmul_residual_body(x_ref, y_ref, w_ref, o_ref):
    o_ref[...] = x_ref[...] + jnp.dot(y_ref[...], w_ref[...], preferred_element_type=F32)


def matmul_residual(x, y, w, tm=1024):
    m, d = x.shape
    k = y.shape[1]
    return pl.pallas_call(
        _matmul_residual_body,
        grid=(m // tm,),
        in_specs=[
            pl.BlockSpec((tm, d), lambda i: (i, 0)),
            pl.BlockSpec((tm, k), lambda i: (i, 0)),
            pl.BlockSpec((k, d), lambda i: (0, 0)),
        ],
        out_specs=pl.BlockSpec((tm, d), lambda i: (i, 0)),
        out_shape=jax.ShapeDtypeStruct((m, d), F32),
        compiler_params=_cparams("parallel"),
        name="matmul_residual",
    )(x, y, w)


def _final_norm_body(x_ref, g_ref, o_ref):
    x = x_ref[...]
    ms = jnp.mean(x * x, axis=-1, keepdims=True)
    o_ref[...] = x * lax.rsqrt(ms + NORM_EPS) * g_ref[...]


def final_norm(x, gain, tm=1024):
    m, d = x.shape
    return pl.pallas_call(
        _final_norm_body,
        grid=(m // tm,),
        in_specs=[pl.BlockSpec((tm, d), lambda i: (i, 0)), pl.BlockSpec((1, d), lambda i: (0, 0))],
        out_specs=pl.BlockSpec((tm, d), lambda i: (i, 0)),
        out_shape=jax.ShapeDtypeStruct((m, d), F32),
        compiler_params=_cparams("parallel"),
        name="final_norm",
    )(x, gain.reshape(1, d))


def _top16_desc(s, with_rank):
    work = s
    rank = jnp.full(s.shape, 99.0, F32) if with_rank else None
    vals = []
    for r in range(PEER_TOPK):
        m = jnp.max(work, axis=0, keepdims=True)
        eq = work == m
        if with_rank:
            rank = jnp.where(eq, float(r + 1), rank)
        work = jnp.where(eq, -jnp.inf, work)
        vals.append(m)
    return jnp.concatenate(vals, axis=0), rank


def _peer_route_body(x_ref, g_ref, wqT_ref, keys_ref, hnT_ref, n_ref, e0_ref, r1_ref, e1_ref):
    x = x_ref[...]
    ms = jnp.mean(x * x, axis=-1, keepdims=True)
    hn = x * lax.rsqrt(ms + NORM_EPS) * g_ref[...]
    hnT = hn.T.astype(BF16)
    hnT_ref[...] = hnT
    qT = jnp.dot(wqT_ref[...], hnT, preferred_element_type=F32)
    row8 = lax.broadcasted_iota(jnp.int32, (8, x.shape[0]), 0)
    for h in range(PEER_HEADS):
        s = []
        for p in range(2):
            hp = 2 * h + p
            q_hp = qT[hp * PEER_HALF:(hp + 1) * PEER_HALF, :].astype(BF16)
            s.append(jnp.dot(keys_ref[hp], q_hp, preferred_element_type=F32))
        a0, _ = _top16_desc(s[0], False)
        b, rank1 = _top16_desc(s[1], True)
        cand = [a0[0:1, :] + b]
        for p in range(1, PEER_TOPK):
            cnt = PEER_TOPK // (p + 1)
            c = a0[p:p + 1, :] + b[0:8, :]
            cand.append(c if cnt >= 8 else jnp.where(row8 < cnt, c, -jnp.inf))
        best, _ = _top16_desc(jnp.concatenate(cand, axis=0), False)
        tau = best[PEER_TOPK - 1:PEER_TOPK, :]
        z = jnp.sum(jnp.exp(best - best[0:1, :]), axis=0, keepdims=True)
        n_map = jnp.zeros_like(s[0])
        for p in range(PEER_TOPK):
            n_p = jnp.sum(jnp.where(cand[p] >= tau, 1.0, 0.0), axis=0, keepdims=True)
            n_map = jnp.where(s[0] == a0[p:p + 1, :], n_p, n_map)
        n_ref[h] = n_map
        e0_ref[h] = jnp.exp(s[0] - a0[0:1, :]) * (0.5 / z)
        r1_ref[h] = rank1.astype(BF16)
        e1_ref[h] = jnp.exp(s[1] - b[0:1, :]).astype(BF16)


def peer_route(x, gain, wqT, keys, tm=256):
    m, d = x.shape
    nq = wqT.shape[0]
    tab = jax.ShapeDtypeStruct((PEER_HEADS, PEER_NKEYS, m), F32)
    tab16 = jax.ShapeDtypeStruct((PEER_HEADS, PEER_NKEYS, m), BF16)
    tab_spec = pl.BlockSpec((PEER_HEADS, PEER_NKEYS, tm), lambda i: (0, 0, i))
    return pl.pallas_call(
        _peer_route_body,
        grid=(m // tm,),
        in_specs=[
            pl.BlockSpec((tm, d), lambda i: (i, 0)),
            pl.BlockSpec((1, d), lambda i: (0, 0)),
            pl.BlockSpec((nq, d), lambda i: (0, 0)),
            pl.BlockSpec(keys.shape, lambda i: (0, 0, 0)),
        ],
        out_specs=[pl.BlockSpec((d, tm), lambda i: (0, i)), tab_spec, tab_spec, tab_spec, tab_spec],
        out_shape=[jax.ShapeDtypeStruct((d, m), BF16), tab, tab, tab16, tab16],
        compiler_params=_cparams("parallel"),
        name="peer_route",
    )(x, gain.reshape(1, d), wqT, keys)


def _peer_expert_body(x_ref, hnT_ref, n_ref, e0_ref, r1_ref, e1_ref, u_ref, vT_ref, o_ref,
                      a_s, acc_ref, *, ni, nb):
    j = pl.program_id(1)

    @pl.when(j == 0)
    def _():
        a_s[...] = jnp.zeros_like(a_s)
        acc_ref[...] = jnp.zeros_like(acc_ref)

    tm = a_s.shape[2]
    pk = BF16_SUBLANES
    nk = PEER_NKEYS
    cur = j % 2
    prev = 1 - cur

    @pl.when(j < nb)
    def _():
        zc = ni * nk // PEER_Z_CHUNKS
        z_chunks = [jnp.dot(u_ref[pl.ds(c * zc, zc), :], hnT_ref[...], preferred_element_type=F32)
                    for c in range(PEER_Z_CHUNKS)]
        acc_ref[...] += jnp.dot(vT_ref[0], a_s[prev], preferred_element_type=F32)
        for ii in range(ni):
            i = j * ni + ii
            r0 = ii * nk - (ii * nk // zc) * zc
            z = z_chunks[ii * nk // zc][r0:r0 + nk, :].astype(BF16)
            act = z * (1.0 + lax.erf(z * (1.0 / math.sqrt(2.0))))
            gate = None
            for h in range(PEER_HEADS):
                n_b = jnp.broadcast_to(n_ref[h, pl.ds(i, 1), :], (pk, tm)).astype(BF16)[None]
                e0_b = jnp.broadcast_to(e0_ref[h, pl.ds(i, 1), :], (pk, tm)).astype(BF16)[None]
                r1 = r1_ref[h].reshape(nk // pk, pk, tm)
                e1 = e1_ref[h].reshape(nk // pk, pk, tm)
                term = jnp.where(r1 <= n_b, e1 * e0_b, jnp.zeros((), BF16))
                gate = term if gate is None else gate + term
            a_s[cur, pl.ds(ii * nk, nk), :] = (act.reshape(nk // pk, pk, tm) * gate).reshape(nk, tm)

    @pl.when(j == nb)
    def _():
        acc = acc_ref[...] + jnp.dot(vT_ref[0], a_s[prev], preferred_element_type=F32)
        o_ref[...] = x_ref[...] + acc.T


def peer_experts(x, hnT, n_tab, e0_tab, r1_tab, e1_tab, u, v_tab, tm=512, ni=PEER_EXPERT_ROWS):
    m, d = x.shape
    ne = u.shape[0]
    et = ni * PEER_NKEYS
    nb = ne // et
    vT = jnp.transpose(v_tab.astype(BF16).reshape(nb, et, d), (0, 2, 1))
    tab_spec = pl.BlockSpec((PEER_HEADS, PEER_NKEYS, tm), lambda i, j: (0, 0, i))
    return pl.pallas_call(
        functools.partial(_peer_expert_body, ni=ni, nb=nb),
        grid=(m // tm, nb + 1),
        in_specs=[
            pl.BlockSpec((tm, d), lambda i, j: (i, 0)),
            pl.BlockSpec((d, tm), lambda i, j: (0, i)),
            tab_spec, tab_spec, tab_spec, tab_spec,
            pl.BlockSpec((et, d), lambda i, j: (jnp.minimum(j, nb - 1), 0)),
            pl.BlockSpec((1, d, et), lambda i, j: (jnp.clip(j - 1, 0, nb - 1), 0, 0)),
        ],
        out_specs=pl.BlockSpec((tm, d), lambda i, j: (i, 0)),
        out_shape=jax.ShapeDtypeStruct((m, d), F32),
        scratch_shapes=[pltpu.VMEM((2, et, tm), BF16), pltpu.VMEM((d, tm), F32)],
        compiler_params=_cparams("parallel", "arbitrary"),
        name="peer_experts",
    )(x, hnT, n_tab, e0_tab, r1_tab, e1_tab, u, vT)


def peer_layer(x, gain, w_q, keys, u_tab, v_tab):
    wqT = w_q.T.astype(BF16)
    keys2 = keys.reshape(PEER_HEADS * 2, PEER_NKEYS, PEER_HALF).astype(BF16)
    hnT, n_tab, e0_tab, r1_tab, e1_tab = peer_route(x, gain, wqT, keys2)
    return peer_experts(x, hnT, n_tab, e0_tab, r1_tab, e1_tab, u_tab.astype(BF16), v_tab)


def _rope_lanes(t, cos, sin_signed):
    half = SWA_HEADDIM // 2
    lane = lax.broadcasted_iota(jnp.int32, t.shape, 1)
    first = (lane % SWA_HEADDIM) < half
    partner = jnp.where(first, pltpu.roll(t, 128 - half, 1), pltpu.roll(t, half, 1))
    return t * cos + partner * sin_signed


def _swa_body(q_ref, k_ref, v_ref, cos_ref, sin_ref, sink_ref, o_ref, kp_ref, vp_ref):
    w = SWA_WINDOW
    t_len = q_ref.shape[1]
    nkv = SWA_KV_HEADS * SWA_HEADDIM
    rep = SWA_HEADS // SWA_KV_HEADS
    zeros = jnp.zeros((w, nkv), BF16)
    kp_ref[pl.ds(0, w), :] = zeros
    kp_ref[pl.ds(w + t_len, w), :] = zeros
    vp_ref[pl.ds(0, w), :] = zeros
    vp_ref[pl.ds(w + t_len, w), :] = zeros
    vp_ref[pl.ds(w, t_len), :] = v_ref[0]
    cos_all = cos_ref[...]
    sin_all = sin_ref[...]
    for c in range(nkv // 128):
        kc = k_ref[0, :, c * 128:(c + 1) * 128].astype(F32)
        kp_ref[pl.ds(w, t_len), c * 128:(c + 1) * 128] = _rope_lanes(kc, cos_all, sin_all).astype(BF16)

    row = lax.broadcasted_iota(jnp.int32, (w, 3 * w), 0)
    col = lax.broadcasted_iota(jnp.int32, (w, 3 * w), 1)
    band = (col >= row) & (col <= row + 2 * w)

    def block(n, carry):
        base = pl.multiple_of(n * w, w)
        cos_b = cos_ref[pl.ds(base, w), :]
        sin_b = sin_ref[pl.ds(base, w), :]
        kpos = base - w + col
        valid = band & (kpos >= 0) & (kpos < t_len)
        outs = []
        for g in range(SWA_KV_HEADS):
            kw = kp_ref[pl.ds(base, 3 * w), g * SWA_HEADDIM:(g + 1) * SWA_HEADDIM]
            vw = vp_ref[pl.ds(base, 3 * w), g * SWA_HEADDIM:(g + 1) * SWA_HEADDIM]
            scores = []
            for c in range(g * rep // 2, (g + 1) * rep // 2):
                qc = q_ref[0, pl.ds(base, w), c * 128:(c + 1) * 128].astype(F32)
                qc = (_rope_lanes(qc, cos_b, sin_b) * (SWA_HEADDIM ** -0.5)).astype(BF16)
                for hh in range(2):
                    qh = qc[:, hh * SWA_HEADDIM:(hh + 1) * SWA_HEADDIM]
                    scores.append(lax.dot_general(qh, kw, (((1,), (1,)), ((), ())), preferred_element_type=F32))
            probs = []
            for k, s in enumerate(scores):
                s = jnp.where(valid, s, NEG)
                sk = sink_ref[g * rep + k]
                m = jnp.maximum(jnp.max(s, axis=-1, keepdims=True), sk)
                p = jnp.exp(s - m)
                inv = 1.0 / (jnp.sum(p, axis=-1, keepdims=True) + jnp.exp(sk - m))
                probs.append((p.astype(BF16), inv))
            for p, inv in probs:
                outs.append(jnp.dot(p, vw, preferred_element_type=F32) * inv)
        o_ref[0, pl.ds(base, w), :] = jnp.concatenate(outs, axis=-1).astype(o_ref.dtype)
        return carry

    lax.fori_loop(0, t_len // w, block, 0)


def swa_attention(proj, sink, bsz, t_len):
    half = SWA_HEADDIM // 2
    inv_freq = ROPE_THETA ** (-jnp.arange(half, dtype=F32) / half)
    ang = jnp.arange(t_len, dtype=F32)[:, None] * inv_freq[None, :]
    cos, sin = jnp.cos(ang), jnp.sin(ang)
    cos_t = jnp.tile(jnp.concatenate([cos, cos], axis=-1), (1, 2))
    sin_t = jnp.tile(jnp.concatenate([-sin, sin], axis=-1), (1, 2))
    nq = SWA_HEADS * SWA_HEADDIM
    nkv = SWA_KV_HEADS * SWA_HEADDIM
    return pl.pallas_call(
        _swa_body,
        grid=(bsz,),
        in_specs=[
            pl.BlockSpec((1, t_len, nq), lambda b: (b, 0, 0)),
            pl.BlockSpec((1, t_len, nkv), lambda b: (b, 0, nq // nkv)),
            pl.BlockSpec((1, t_len, nkv), lambda b: (b, 0, nq // nkv + 1)),
            pl.BlockSpec((t_len, 128), lambda b: (0, 0)),
            pl.BlockSpec((t_len, 128), lambda b: (0, 0)),
            pl.BlockSpec(memory_space=pltpu.SMEM),
        ],
        out_specs=pl.BlockSpec((1, t_len, nq), lambda b: (b, 0, 0)),
        out_shape=jax.ShapeDtypeStruct((bsz, t_len, nq), BF16),
        scratch_shapes=[pltpu.VMEM((t_len + 2 * SWA_WINDOW, nkv), BF16),
                        pltpu.VMEM((t_len + 2 * SWA_WINDOW, nkv), BF16)],
        compiler_params=_cparams("parallel"),
        name="swa_attention",
    )(proj, proj, proj, cos_t, sin_t, sink.astype(F32))


def _na_bias_table(rpb):
    qc = np.arange(GRID_W)[:, None]
    kc = np.arange(GRID_W)[None, :]
    cstart = np.clip(qc - NA_COLS // 2, 0, GRID_W - NA_COLS)
    valid = (kc >= cstart) & (kc < cstart + NA_COLS)
    cidx = np.clip(kc - qc + NA_COLS - 1, 0, 2 * NA_COLS - 2)
    onehot = (np.arange(2 * NA_COLS - 1)[:, None, None] == cidx[None]).astype(np.float32)
    toep = jnp.einsum('hrc,cqk->hrqk', rpb.astype(F32), onehot, precision=lax.Precision.HIGHEST)
    toep = jnp.where(valid[None, None], toep, NEG)
    return jnp.concatenate([toep[:, :-1], toep[:, 1:]], axis=-1)


def _na_body(q_ref, k_ref, v_ref, bias_ref, o_ref):
    t_len = q_ref.shape[1]
    rows = t_len // GRID_W
    win = NA_ROWS * GRID_W

    def row_group(gi, carry):
        chains = []
        for rr in range(NA_ROW_UNROLL):
            r = gi * NA_ROW_UNROLL + rr
            rs = jnp.clip(r - NA_ROWS // 2, 0, rows - NA_ROWS)
            d0 = rs - r + NA_ROWS - 1
            qbase = pl.multiple_of(r * GRID_W, GRID_W)
            kbase = pl.multiple_of(rs * GRID_W, GRID_W)
            qr = q_ref[0, pl.ds(qbase, GRID_W), :]
            kw = k_ref[0, pl.ds(kbase, win), :]
            for hh in range(2):
                sl = slice(hh * NA_HEADDIM, (hh + 1) * NA_HEADDIM)
                s = lax.dot_general(qr[:, sl], kw[:, sl], (((1,), (1,)), ((), ())), preferred_element_type=F32)
                chains.append((hh, d0, kbase, s))
        probs = []
        for hh, d0, kbase, s in chains:
            bias = jnp.concatenate([bias_ref[hh, d0 + 2 * c] for c in range(NA_ROWS // 2)], axis=-1)
            s = s * (NA_HEADDIM ** -0.5) + bias
            m = jnp.max(s, axis=-1, keepdims=True)
            p = jnp.exp(s - m)
            probs.append((p.astype(BF16), 1.0 / jnp.sum(p, axis=-1, keepdims=True)))
        outs = []
        for (hh, d0, kbase, s), (p, inv) in zip(chains, probs):
            vw = v_ref[0, pl.ds(kbase, win), hh * NA_HEADDIM:(hh + 1) * NA_HEADDIM]
            outs.append(jnp.dot(p, vw, preferred_element_type=F32) * inv)
        for rr in range(NA_ROW_UNROLL):
            qbase = pl.multiple_of((gi * NA_ROW_UNROLL + rr) * GRID_W, GRID_W)
            o_ref[0, pl.ds(qbase, GRID_W), :] = jnp.concatenate(outs[2 * rr:2 * rr + 2], axis=-1).astype(o_ref.dtype)
        return carry

    lax.fori_loop(0, rows // NA_ROW_UNROLL, row_group, 0)


def na_attention(proj, rpb, bsz, t_len):
    bias = _na_bias_table(rpb)
    npair = NA_HEADS // 2
    return pl.pallas_call(
        _na_body,
        grid=(npair, bsz),
        in_specs=[
            pl.BlockSpec((1, t_len, 128), lambda hp, b: (b, 0, hp)),
            pl.BlockSpec((1, t_len, 128), lambda hp, b: (b, 0, npair + hp)),
            pl.BlockSpec((1, t_len, 128), lambda hp, b: (b, 0, 2 * npair + hp)),
            pl.BlockSpec((2, 2 * NA_ROWS - 2, GRID_W, 2 * GRID_W), lambda hp, b: (hp, 0, 0, 0)),
        ],
        out_specs=pl.BlockSpec((1, t_len, 128), lambda hp, b: (b, 0, hp)),
        out_shape=jax.ShapeDtypeStruct((bsz, t_len, NA_WIDTH), BF16),
        compiler_params=_cparams("parallel", "parallel"),
        name="na_attention",
    )(proj, proj, proj, bias)


def _conv_silu_body(x_ref, w_ref, b_ref, o_ref, *, n_l2):
    t_len, tc = x_ref.shape[1], x_ref.shape[2]
    rc = CONV_ROW_CHUNK
    halo = F32_SUBLANES
    normalise = pl.program_id(1) < n_l2

    def chunk(ci, carry):
        r0 = pl.multiple_of(ci * rc, rc)
        cur = x_ref[0, pl.ds(r0, rc), :].astype(F32)
        lo = pl.multiple_of(jnp.maximum(r0 - halo, 0), halo)
        hi = pl.multiple_of(jnp.minimum(r0 + rc, t_len - halo), halo)
        before = jnp.where(r0 > 0, x_ref[0, pl.ds(lo, halo), :].astype(F32), 0.0)
        after = jnp.where(r0 + rc < t_len, x_ref[0, pl.ds(hi, halo), :].astype(F32), 0.0)
        xx = jnp.concatenate([before, cur, after], axis=0)
        acc = cur * w_ref[CONV_K // 2:CONV_K // 2 + 1, :] + b_ref[...]
        for k in range(CONV_K):
            off = k - CONV_K // 2
            if off == 0:
                continue
            acc = acc + xx[halo + off:halo + off + rc, :] * w_ref[k:k + 1, :]
        y = acc * jax.nn.sigmoid(acc)
        if n_l2 > 0:
            parts = []
            for c in range(tc // 128):
                yc = y[:, c * 128:(c + 1) * 128]
                ss = jnp.sum(yc * yc, axis=-1, keepdims=True)
                parts.append(yc * jnp.where(normalise, lax.rsqrt(ss + NORM_EPS), 1.0))
            y = jnp.concatenate(parts, axis=-1)
        o_ref[0, pl.ds(r0, rc), :] = y.astype(o_ref.dtype)
        return carry

    lax.fori_loop(0, t_len // rc, chunk, 0)


def conv_silu(proj, w, bias, col0, n_ch, n_l2=0, tc=512):
    bsz, t_len, _ = proj.shape
    assert col0 % tc == 0 and n_ch % tc == 0
    c0 = col0 // tc
    return pl.pallas_call(
        functools.partial(_conv_silu_body, n_l2=n_l2),
        grid=(bsz, n_ch // tc),
        in_specs=[
            pl.BlockSpec((1, t_len, tc), lambda b, j: (b, 0, c0 + j)),
            pl.BlockSpec((CONV_K, tc), lambda b, j: (0, j)),
            pl.BlockSpec((1, tc), lambda b, j: (0, j)),
        ],
        out_specs=pl.BlockSpec((1, t_len, tc), lambda b, j: (b, 0, j)),
        out_shape=jax.ShapeDtypeStruct((bsz, t_len, n_ch), BF16),
        compiler_params=_cparams("parallel", "parallel"),
        name="conv_silu",
    )(proj, w.astype(F32), bias.astype(F32).reshape(1, n_ch))


def _softplus(x):
    return jnp.maximum(x, 0.0) + jnp.log1p(jnp.exp(-jnp.abs(x)))


def _bmm(a, b, precision=None):
    return lax.dot_general(a, b, (((2,), (1,)), ((0,), (0,))), precision=precision,
                           preferred_element_type=F32)


def _split_bf16(a):
    hi = a.astype(BF16)
    return hi, (a - hi.astype(F32)).astype(BF16)


def _split3_bf16(a):
    hi = a.astype(BF16)
    rest = a - hi.astype(F32)
    mid = rest.astype(BF16)
    return hi, mid, (rest - mid.astype(F32)).astype(BF16)


def _bmm_split(a, b):
    return _bmm(a[0], b[0]) + _bmm(a[0], b[1]) + _bmm(a[1], b[0])


def _bmm_nt(a, b):
    return lax.dot_general(a, b, (((2,), (2,)), ((0,), (0,))), preferred_element_type=F32)


def _gdn_body(q_ref, k_ref, v_ref, z_ref, gcol_ref, grow_ref, alog_ref, dtb_ref, nw_ref, o_ref,
              u_s, w_s, qk_s, qg_s, kg_s, el_s, o_s):
    c_len = GDN_CHUNK
    t_len = q_ref.shape[1]
    nc = t_len // c_len
    h = pl.program_id(1)
    ii = lax.broadcasted_iota(jnp.int32, (c_len, c_len), 0)
    jj = lax.broadcasted_iota(jnp.int32, (c_len, c_len), 1)
    eye = (ii == jj).astype(F32)
    cg = GDN_PHASE_A_CHUNKS

    def phase_a(gi, carry):
        c0 = pl.multiple_of(gi * cg, cg)
        rows = pl.ds(pl.multiple_of(gi * (cg * c_len), cg * c_len), cg * c_len)
        k3 = k_ref[0, rows, :].reshape(cg, c_len, GDN_DK)
        v3 = v_ref[0, rows, :].reshape(cg, c_len, GDN_DK).astype(F32)
        kf = k3.astype(F32)
        qs = q_ref[0, rows, :].reshape(cg, c_len, GDN_DK).astype(F32) * (GDN_DK ** -0.5)
        gcol = gcol_ref[0, 0, pl.ds(c0, cg)]
        grow = grow_ref[0, 0, pl.ds(c0, cg)]
        qk_raw = _bmm_nt(qs.astype(BF16), k3)
        kb, gc_c, g_last, decay, strict = [], [], [], [], []
        for d in range(2):
            incl = (ii >= jj) if d == 0 else (ii <= jj)
            strict.append((ii > jj) if d == 0 else (ii < jj))
            tri = incl.astype(F32)
            tri_t = ((ii <= jj) if d == 0 else (ii >= jj)).astype(F32)
            neg_a = -jnp.exp(jnp.full((1, 1, 1), alog_ref[d, h], F32))
            dtb = dtb_ref[d, h]
            g_c = neg_a * _softplus(gcol[:, :, d:d + 1] + dtb)
            g_r = neg_a * _softplus(grow[:, d:d + 1, :] + dtb)
            beta_c = jax.nn.sigmoid(gcol[:, :, 2 + d:3 + d])
            gc_c.append(jnp.sum(tri[None] * g_r, axis=2, keepdims=True))
            gc_r = jnp.sum(tri_t[None] * g_c, axis=1, keepdims=True)
            g_last.append(jnp.sum(g_r, axis=2, keepdims=True))
            decay.append(jnp.where(incl[None], jnp.exp(jnp.where(incl[None], gc_c[d] - gc_r, 0.0)), 0.0))
            kb.append((kf * beta_c, v3 * beta_c))
        kk = [_bmm_nt(kb[d][0].astype(BF16), k3) for d in range(2)]
        low = [jnp.where(strict[d][None], kk[d] * decay[d], 0.0) for d in range(2)]
        inv = [eye[None] - low[d] for d in range(2)]
        pw = [_split_bf16(low[d]) for d in range(2)]
        for _ in range(5):
            pw = [_split_bf16(_bmm_split(pw[d], pw[d])) for d in range(2)]
            inv = [inv[d] + _bmm_split(_split_bf16(inv[d]), pw[d]) for d in range(2)]
        rhs = [jnp.concatenate([kb[d][1], kb[d][0] * jnp.exp(gc_c[d])], axis=-1) for d in range(2)]
        sol = [_bmm_split(_split_bf16(inv[d]), _split_bf16(rhs[d])) for d in range(2)]
        for d in range(2):
            u_s[d, pl.ds(c0, cg)] = sol[d][:, :, :GDN_DK]
            w_s[d, pl.ds(c0, cg)] = sol[d][:, :, GDN_DK:].astype(BF16)
            qk_s[d, pl.ds(c0, cg)] = (qk_raw * decay[d]).astype(BF16)
            qg_s[d, pl.ds(c0, cg)] = (qs * jnp.exp(gc_c[d])).astype(BF16)
            kg_s[d, pl.ds(c0, cg)] = (kf * jnp.exp(g_last[d] - gc_c[d])).astype(BF16)
            el_s[d, pl.ds(c0, cg)] = jnp.broadcast_to(jnp.exp(g_last[d]), (cg, 1, GDN_DK))
        return carry

    lax.fori_loop(0, nc // cg, phase_a, 0)

    def step(t, carry):
        cs = (t, nc - 1 - t)
        sb = [carry[d].astype(BF16) for d in range(2)]
        ws = [jnp.dot(w_s[d, cs[d]], sb[d], preferred_element_type=F32) for d in range(2)]
        qs_ = [jnp.dot(qg_s[d, cs[d]], sb[d], preferred_element_type=F32) for d in range(2)]
        vb = [(u_s[d, cs[d]] - ws[d]).astype(BF16) for d in range(2)]
        os_ = [qs_[d] + jnp.dot(qk_s[d, cs[d]], vb[d], preferred_element_type=F32) for d in range(2)]
        kv = [lax.dot_general(kg_s[d, cs[d]], vb[d], (((0,), (0,)), ((), ())), preferred_element_type=F32)
              for d in range(2)]
        for d in range(2):
            o_s[d, cs[d]] = os_[d]
        return tuple(carry[d] * el_s[d, cs[d]] + kv[d] for d in range(2))

    s0 = jnp.zeros((GDN_DK, GDN_DK), F32)
    lax.fori_loop(0, nc, step, (s0, s0))

    o = (o_s[0] + o_s[1]).reshape(t_len, GDN_DK)
    ms = jnp.mean(o * o, axis=-1, keepdims=True)
    z = z_ref[0]
    y = o * lax.rsqrt(ms + NORM_EPS) * nw_ref[...] * (z * jax.nn.sigmoid(z))
    o_ref[0] = y.astype(o_ref.dtype)


def gdn_scan(qkv, proj, gates, a_log, dt_bias, norm_w, bsz, t_len):
    nc = t_len // GDN_CHUNK
    nh = GDN_HEADS
    g4 = gates[:, :4 * nh].reshape(bsz, nc, GDN_CHUNK, 4, nh)
    gcol = jnp.transpose(g4, (0, 4, 1, 2, 3))
    grow = jnp.transpose(g4, (0, 4, 1, 3, 2))
    dk = GDN_DK
    nq = GDN_QK // dk
    return pl.pallas_call(
        _gdn_body,
        grid=(bsz, nh),
        in_specs=[
            pl.BlockSpec((1, t_len, dk), lambda b, h: (b, 0, h)),
            pl.BlockSpec((1, t_len, dk), lambda b, h: (b, 0, nq + h)),
            pl.BlockSpec((1, t_len, dk), lambda b, h: (b, 0, 2 * nq + h)),
            pl.BlockSpec((1, t_len, dk), lambda b, h: (b, 0, 3 * nq + h)),
            pl.BlockSpec((1, 1, nc, GDN_CHUNK, 4), lambda b, h: (b, h, 0, 0, 0)),
            pl.BlockSpec((1, 1, nc, 4, GDN_CHUNK), lambda b, h: (b, h, 0, 0, 0)),
            pl.BlockSpec(memory_space=pltpu.SMEM),
            pl.BlockSpec(memory_space=pltpu.SMEM),
            pl.BlockSpec((1, dk), lambda b, h: (0, 0)),
        ],
        out_specs=pl.BlockSpec((1, t_len, dk), lambda b, h: (b, 0, h)),
        out_shape=jax.ShapeDtypeStruct((bsz, t_len, GDN_V), BF16),
        scratch_shapes=[
            pltpu.VMEM((2, nc, GDN_CHUNK, dk), F32),
            pltpu.VMEM((2, nc, GDN_CHUNK, dk), BF16),
            pltpu.VMEM((2, nc, GDN_CHUNK, GDN_CHUNK), BF16),
            pltpu.VMEM((2, nc, GDN_CHUNK, dk), BF16),
            pltpu.VMEM((2, nc, GDN_CHUNK, dk), BF16),
            pltpu.VMEM((2, nc, 1, dk), F32),
            pltpu.VMEM((2, nc, GDN_CHUNK, dk), F32),
        ],
        compiler_params=_cparams("parallel", "parallel"),
        name="gdn_scan",
    )(qkv, qkv, qkv, proj, gcol, grow, a_log.astype(F32), dt_bias.astype(F32),
      norm_w.astype(F32).reshape(1, dk))


def _pad_cols(w, n):
    return jnp.pad(w, ((0, 0), (0, n - w.shape[1])))


def gdn_layer(x, gain, w_in, conv_w, a_log, dt_bias, norm_w, w_out, bsz, t_len):
    n_main = GDN_CONV_CH + GDN_V
    proj = norm_matmul(x, gain, w_in[:, :n_main].astype(BF16), F32)
    gates = norm_matmul(x, gain, _pad_cols(w_in[:, n_main:], 128).astype(BF16), F32)
    proj3 = proj.reshape(bsz, t_len, n_main)
    qkv = conv_silu(proj3, conv_w, jnp.zeros((GDN_CONV_CH,), F32), 0, GDN_CONV_CH,
                    n_l2=2 * GDN_QK // 512)
    y = gdn_scan(qkv, proj3, gates, a_log, dt_bias, norm_w, bsz, t_len)
    return matmul_residual(x, y.reshape(bsz * t_len, GDN_V), w_out.astype(BF16))


def _ssd_body(xs_ref, b_ref, c_ref, z_ref, dcol_ref, drow_ref, alogc_ref, alogr_ref, dtbc_ref, dtbr_ref, dskip_ref,
              nw_ref, o_ref, xw_s, ea_s, el_s, y_s, st_s):
    assert SSD_CHUNK == SSD_HEADDIM
    c_len = SSD_CHUNK
    t_len = xs_ref.shape[1]
    nc = t_len // c_len
    nr = SSD_HEADS // SSD_GROUPS
    hp = SSD_HEADDIM
    g = pl.program_id(1)
    ii = lax.broadcasted_iota(jnp.int32, (c_len, c_len), 0)
    jj = lax.broadcasted_iota(jnp.int32, (c_len, c_len), 1)
    cg = SSD_PHASE_A_CHUNKS

    def phase_a(gi, carry):
        c0 = pl.multiple_of(gi * cg, cg)
        rows = pl.ds(pl.multiple_of(gi * (cg * c_len), cg * c_len), cg * c_len)
        x3 = xs_ref[0, rows, :].reshape(cg, c_len, nr * hp)
        b3 = b_ref[0, rows, :].reshape(cg, c_len, SSD_STATE)
        c3 = c_ref[0, rows, :].reshape(cg, c_len, SSD_STATE)
        dcol = dcol_ref[0, 0, pl.ds(c0, cg)]
        drow = drow_ref[0, 0, pl.ds(c0, cg)]
        cb = _bmm_nt(c3, b3)
        dt_c = _softplus(dcol + dtbc_ref[0])
        dt_r = _softplus(drow + dtbr_ref[0])
        da_c = dt_c * -jnp.exp(alogc_ref[0])
        da_r = dt_r * -jnp.exp(alogr_ref[0])
        tri_f = jnp.broadcast_to((ii >= jj).astype(BF16)[None], (cg, c_len, c_len))
        tri_b = jnp.broadcast_to((ii <= jj).astype(BF16)[None], (cg, c_len, c_len))
        fwd_c = lax.broadcasted_iota(jnp.int32, (1, 1, 2 * nr), 2) < nr
        fwd_r = lax.broadcasted_iota(jnp.int32, (1, 2 * nr, 1), 1) < nr
        da_c3, da_r3 = _split3_bf16(da_c), _split3_bf16(da_r)
        ac_c = jnp.where(fwd_c, sum(_bmm(tri_f, p) for p in da_c3), sum(_bmm(tri_b, p) for p in da_c3))
        ac_r = jnp.where(fwd_r, sum(_bmm(p, tri_b) for p in da_r3), sum(_bmm(p, tri_f) for p in da_r3))
        ydiag = None
        for d in range(2):
            incl = (ii >= jj) if d == 0 else (ii <= jj)
            sel = (lax.broadcasted_iota(jnp.int32, (2 * nr, nr * hp), 1) // hp + d * nr
                   == lax.broadcasted_iota(jnp.int32, (2 * nr, nr * hp), 0)).astype(BF16)

            def spread(cols, pieces):
                parts = pieces(cols.reshape(cg * c_len, 2 * nr))
                return sum(jnp.dot(p, sel, preferred_element_type=F32) for p in parts).reshape(cg, c_len, nr * hp)

            dt_w = spread(dt_c, _split_bf16)
            ac_w = spread(ac_c, _split3_bf16)
            last = ac_w[:, c_len - 1:c_len, :] if d == 0 else ac_w[:, 0:1, :]
            xr = x3.astype(F32) * dt_w
            xw_s[d, pl.ds(c0, cg)] = (xr * jnp.exp(last - ac_w)).astype(BF16)
            ea_s[d, pl.ds(c0, cg)] = jnp.exp(ac_w)
            el_s[d, pl.ds(c0, cg)] = jnp.exp(last)
            xr = xr.astype(BF16)
            yd = []
            for r in range(nr):
                col = d * nr + r
                diff = ac_w[:, :, r * hp:r * hp + c_len] - ac_r[:, col:col + 1, :]
                seg = jnp.where(incl[None], jnp.exp(jnp.where(incl[None], diff, 0.0)), 0.0)
                yd.append(_bmm((cb * seg).astype(BF16), xr[:, :, r * hp:(r + 1) * hp]))
            yd = jnp.concatenate(yd, axis=-1)
            ydiag = yd if ydiag is None else ydiag + yd
        y_s[pl.ds(c0, cg)] = ydiag
        return carry

    lax.fori_loop(0, nc // cg, phase_a, 0)

    st_s[...] = jnp.zeros_like(st_s)

    def step(t, carry):
        for d in range(2):
            c = t if d == 0 else nc - 1 - t
            base = pl.multiple_of(c * c_len, c_len)
            cc = c_ref[0, pl.ds(base, c_len), :]
            bb = b_ref[0, pl.ds(base, c_len), :]
            st = st_s[d]
            y_s[c] += jnp.dot(cc, st.astype(BF16), preferred_element_type=F32) * ea_s[d, c]
            st_s[d] = st * el_s[d, c] + lax.dot_general(bb, xw_s[d, c], (((0,), (0,)), ((), ())),
                                                        preferred_element_type=F32)
        return carry

    lax.fori_loop(0, nc, step, 0)

    xs = xs_ref[0].astype(F32)
    z = z_ref[0]
    y = (y_s[...].reshape(t_len, nr * hp) + dskip_ref[...] * xs) * (z * jax.nn.sigmoid(z))
    ms = jnp.mean(y * y, axis=-1, keepdims=True)
    o_ref[0] = (y * lax.rsqrt(ms + NORM_EPS) * nw_ref[...]).astype(o_ref.dtype)


def ssd_scan(xbc, proj, dts, a_log, dt_bias, d_skip, norm_w, bsz, t_len):
    nc = t_len // SSD_CHUNK
    ng = SSD_GROUPS
    nr = SSD_HEADS // ng
    gw = nr * SSD_HEADDIM
    d6 = dts[:, :2 * SSD_HEADS].reshape(bsz, nc, SSD_CHUNK, 2, ng, nr)
    dcol = jnp.transpose(d6, (0, 4, 1, 2, 3, 5)).reshape(bsz, ng, nc, SSD_CHUNK, 2 * nr)
    drow = jnp.transpose(d6, (0, 4, 1, 3, 5, 2)).reshape(bsz, ng, nc, 2 * nr, SSD_CHUNK)
    nb0 = SSD_INNER // SSD_STATE
    dtb_g = jnp.transpose(dt_bias.astype(F32).reshape(2, ng, nr), (1, 0, 2)).reshape(ng, 2 * nr)
    alog_g = jnp.transpose(a_log.astype(F32).reshape(2, ng, nr), (1, 0, 2)).reshape(ng, 2 * nr)
    return pl.pallas_call(
        _ssd_body,
        grid=(bsz, ng),
        in_specs=[
            pl.BlockSpec((1, t_len, gw), lambda b, g: (b, 0, g)),
            pl.BlockSpec((1, t_len, SSD_STATE), lambda b, g: (b, 0, nb0 + g)),
            pl.BlockSpec((1, t_len, SSD_STATE), lambda b, g: (b, 0, nb0 + ng + g)),
            pl.BlockSpec((1, t_len, gw), lambda b, g: (b, 0, g)),
            pl.BlockSpec((1, 1, nc, SSD_CHUNK, 2 * nr), lambda b, g: (b, g, 0, 0, 0)),
            pl.BlockSpec((1, 1, nc, 2 * nr, SSD_CHUNK), lambda b, g: (b, g, 0, 0, 0)),
            pl.BlockSpec((1, 1, 2 * nr), lambda b, g: (g, 0, 0)),
            pl.BlockSpec((1, 2 * nr, 1), lambda b, g: (g, 0, 0)),
            pl.BlockSpec((1, 1, 2 * nr), lambda b, g: (g, 0, 0)),
            pl.BlockSpec((1, 2 * nr, 1), lambda b, g: (g, 0, 0)),
            pl.BlockSpec((1, gw), lambda b, g: (0, g)),
            pl.BlockSpec((1, gw), lambda b, g: (0, g)),
        ],
        out_specs=pl.BlockSpec((1, t_len, gw), lambda b, g: (b, 0, g)),
        out_shape=jax.ShapeDtypeStruct((bsz, t_len, SSD_INNER), BF16),
        scratch_shapes=[
            pltpu.VMEM((2, nc, SSD_CHUNK, gw), BF16),
            pltpu.VMEM((2, nc, SSD_CHUNK, gw), F32),
            pltpu.VMEM((2, nc, 1, gw), F32),
            pltpu.VMEM((nc, SSD_CHUNK, gw), F32),
            pltpu.VMEM((2, SSD_STATE, gw), F32),
        ],
        compiler_params=_cparams("parallel", "parallel"),
        name="ssd_scan",
    )(xbc, xbc, xbc, proj, dcol, drow, alog_g.reshape(ng, 1, 2 * nr), alog_g.reshape(ng, 2 * nr, 1),
      dtb_g.reshape(ng, 1, 2 * nr), dtb_g.reshape(ng, 2 * nr, 1),
      jnp.repeat(d_skip.astype(F32), SSD_HEADDIM).reshape(1, SSD_INNER),
      norm_w.astype(F32).reshape(1, SSD_INNER))


def ssd_layer(x, gain, w_in, conv_w, conv_b, a_log, dt_bias, d_skip, norm_w, w_out, bsz, t_len):
    n_main = SSD_INNER + SSD_CONV_CH
    proj = norm_matmul(x, gain, w_in[:, :n_main].astype(BF16), F32)
    dts = norm_matmul(x, gain, _pad_cols(w_in[:, n_main:], 128).astype(BF16), F32)
    proj3 = proj.reshape(bsz, t_len, n_main)
    xbc = conv_silu(proj3, conv_w, conv_b, SSD_INNER, SSD_CONV_CH)
    y = ssd_scan(xbc, proj3, dts, a_log, dt_bias, d_skip, norm_w, bsz, t_len)
    return matmul_residual(x, y.reshape(bsz * t_len, SSD_INNER), w_out.astype(BF16))


def swa_layer(x, gain, w_in, sink, w_out, bsz, t_len):
    proj = norm_matmul(x, gain, w_in.astype(BF16), BF16)
    o = swa_attention(proj.reshape(bsz, t_len, -1), sink, bsz, t_len)
    return matmul_residual(x, o.reshape(bsz * t_len, -1), w_out.astype(BF16))


def na_layer(x, gain, w_in, rpb, w_out, bsz, t_len):
    proj = norm_matmul(x, gain, w_in.astype(BF16), BF16)
    o = na_attention(proj.reshape(bsz, t_len, -1), rpb, bsz, t_len)
    return matmul_residual(x, o.reshape(bsz * t_len, -1), w_out.astype(BF16))


def kernel(x, norm_mix, norm_ffn, norm_final, gdn_w_in, gdn_conv, gdn_a_log, gdn_dt_bias, gdn_norm, gdn_w_out, ssd_w_in, ssd_conv, ssd_conv_b, ssd_a_log, ssd_dt_bias, ssd_d, ssd_norm, ssd_w_out, swa_w_in, swa_sink, swa_w_out, na_w_in, na_rpb, na_w_out, peer_w_q, peer_keys, peer_u, peer_v):
    bsz, t_len, d = x.shape
    depth = norm_mix.shape[0]
    xf = x.reshape(bsz * t_len, d)
    for i in range(depth):
        mixer, j = i % 4, i // 4
        if mixer == 0:
            xf = gdn_layer(xf, norm_mix[i], gdn_w_in[j], gdn_conv[j], gdn_a_log[j], gdn_dt_bias[j],
                           gdn_norm[j], gdn_w_out[j], bsz, t_len)
        elif mixer == 1:
            xf = ssd_layer(xf, norm_mix[i], ssd_w_in[j], ssd_conv[j], ssd_conv_b[j], ssd_a_log[j],
                           ssd_dt_bias[j], ssd_d[j], ssd_norm[j], ssd_w_out[j], bsz, t_len)
        elif mixer == 2:
            xf = swa_layer(xf, norm_mix[i], swa_w_in[j], swa_sink[j], swa_w_out[j], bsz, t_len)
        else:
            xf = na_layer(xf, norm_mix[i], na_w_in[j], na_rpb[j], na_w_out[j], bsz, t_len)
        xf = peer_layer(xf, norm_ffn[i], peer_w_q[i], peer_keys[i], peer_u[i], peer_v[i])
    return final_norm(xf, norm_final).reshape(bsz, t_len, d)
```

```python
import functools
import math

import jax
import jax.numpy as jnp
import numpy as np
from jax import lax
from jax.experimental import pallas as pl
from jax.experimental.pallas import tpu as pltpu

F32 = jnp.float32
BF16 = jnp.bfloat16
NEG = -1e30

D_MODEL = 1024
SEQ = 2048
GRID_W = 64
CONV_K = 5
NORM_EPS = 1e-6

GDN_HEADS = 8
GDN_DK = 128
GDN_CHUNK = 64
GDN_QK = 1024
GDN_V = 1024
GDN_CONV_CH = 3072

SSD_INNER = 2048
SSD_HEADDIM = 64
SSD_HEADS = 32
SSD_GROUPS = 4
SSD_STATE = 128
SSD_CHUNK = 64
SSD_BC = 512
SSD_CONV_CH = 3072

SWA_HEADS = 16
SWA_KV_HEADS = 4
SWA_HEADDIM = 64
SWA_WINDOW = 128
ROPE_THETA = 10000.0

NA_HEADS = 16
NA_HEADDIM = 64
NA_ROWS = 8
NA_COLS = 16
NA_WIDTH = 1024

PEER_HEADS = 8
PEER_NKEYS = 128
PEER_QDIM = 256
PEER_HALF = 128
PEER_TOPK = 16

GDN_PHASE_A_CHUNKS = 8
SSD_PHASE_A_CHUNKS = 8

PEER_EXPERT_ROWS = 8
PEER_Z_CHUNKS = 4

NA_ROW_UNROLL = 4

BF16_SUBLANES = 16
F32_SUBLANES = 8

CONV_ROW_CHUNK = 256

VMEM_LIMIT_BYTES = 52 * 1024 * 1024


def _cparams(*sem, flags=None):
    return pltpu.CompilerParams(dimension_semantics=sem, vmem_limit_bytes=VMEM_LIMIT_BYTES, flags=flags)


def _norm_matmul_body(x_ref, g_ref, w_ref, o_ref, hn_ref):
    @pl.when(pl.program_id(1) == 0)
    def _():
        x = x_ref[...]
        ms = jnp.mean(x * x, axis=-1, keepdims=True)
        hn_ref[...] = (x * lax.rsqrt(ms + NORM_EPS) * g_ref[...]).astype(BF16)

    o_ref[...] = jnp.dot(hn_ref[...], w_ref[...], preferred_element_type=F32).astype(o_ref.dtype)


def norm_matmul(x, gain, w, out_dtype, tm=1024, tn=1024):
    m, d = x.shape
    n = w.shape[1]
    tn = min(tn, n)
    while n % tn:
        tn //= 2
    assert m % tm == 0 and tn % 128 == 0
    return pl.pallas_call(
        _norm_matmul_body,
        grid=(m // tm, n // tn),
        in_specs=[
            pl.BlockSpec((tm, d), lambda i, j: (i, 0)),
            pl.BlockSpec((1, d), lambda i, j: (0, 0)),
            pl.BlockSpec((d, tn), lambda i, j: (0, j)),
        ],
        out_specs=pl.BlockSpec((tm, tn), lambda i, j: (i, j)),
        out_shape=jax.ShapeDtypeStruct((m, n), out_dtype),
        scratch_shapes=[pltpu.VMEM((tm, d), BF16)],
        compiler_params=_cparams("parallel", "arbitrary"),
        name="norm_matmul",
    )(x, gain.reshape(1, d), w)


def _matmul_residual_body(x_ref, y_ref, w_ref, o_ref):
    o_ref[...] = x_ref[...] + jnp.dot(y_ref[...], w_ref[...], preferred_element_type=F32)


def matmul_residual(x, y, w, tm=1024):
    m, d = x.shape
    k = y.shape[1]
    return pl.pallas_call(
        _matmul_residual_body,
        grid=(m // tm,),
        in_specs=[
            pl.BlockSpec((tm, d), lambda i: (i, 0)),
            pl.BlockSpec((tm, k), lambda i: (i, 0)),
            pl.BlockSpec((k, d), lambda i: (0, 0)),
        ],
        out_specs=pl.BlockSpec((tm, d), lambda i: (i, 0)),
        out_shape=jax.ShapeDtypeStruct((m, d), F32),
        compiler_params=_cparams("parallel"),
        name="matmul_residual",
    )(x, y, w)


def _final_norm_body(x_ref, g_ref, o_ref):
    x = x_ref[...]
    ms = jnp.mean(x * x, axis=-1, keepdims=True)
    o_ref[...] = x * lax.rsqrt(ms + NORM_EPS) * g_ref[...]


def final_norm(x, gain, tm=1024):
    m, d = x.shape
    return pl.pallas_call(
        _final_norm_body,
        grid=(m // tm,),
        in_specs=[pl.BlockSpec((tm, d), lambda i: (i, 0)), pl.BlockSpec((1, d), lambda i: (0, 0))],
        out_specs=pl.BlockSpec((tm, d), lambda i: (i, 0)),
        out_shape=jax.ShapeDtypeStruct((m, d), F32),
        compiler_params=_cparams("parallel"),
        name="final_norm",
    )(x, gain.reshape(1, d))


def _top16_desc(s, with_rank):
    work = s
    rank = jnp.full(s.shape, 99.0, F32) if with_rank else None
    vals = []
    for r in range(PEER_TOPK):
        m = jnp.max(work, axis=0, keepdims=True)
        eq = work == m
        if with_rank:
            rank = jnp.where(eq, float(r + 1), rank)
        work = jnp.where(eq, -jnp.inf, work)
        vals.append(m)
    return jnp.concatenate(vals, axis=0), rank


def _peer_route_body(x_ref, g_ref, wqT_ref, keys_ref, hnT_ref, n_ref, e0_ref, r1_ref, e1_ref):
    x = x_ref[...]
    ms = jnp.mean(x * x, axis=-1, keepdims=True)
    hn = x * lax.rsqrt(ms + NORM_EPS) * g_ref[...]
    hnT = hn.T.astype(BF16)
    hnT_ref[...] = hnT
    qT = jnp.dot(wqT_ref[...], hnT, preferred_element_type=F32)
    row8 = lax.broadcasted_iota(jnp.int32, (8, x.shape[0]), 0)
    for h in range(PEER_HEADS):
        s = []
        for p in range(2):
            hp = 2 * h + p
            q_hp = qT[hp * PEER_HALF:(hp + 1) * PEER_HALF, :].astype(BF16)
            s.append(jnp.dot(keys_ref[hp], q_hp, preferred_element_type=F32))
        a0, _ = _top16_desc(s[0], False)
        b, rank1 = _top16_desc(s[1], True)
        cand = [a0[0:1, :] + b]
        for p in range(1, PEER_TOPK):
            cnt = PEER_TOPK // (p + 1)
            c = a0[p:p + 1, :] + b[0:8, :]
            cand.append(c if cnt >= 8 else jnp.where(row8 < cnt, c, -jnp.inf))
        best, _ = _top16_desc(jnp.concatenate(cand, axis=0), False)
        tau = best[PEER_TOPK - 1:PEER_TOPK, :]
        z = jnp.sum(jnp.exp(best - best[0:1, :]), axis=0, keepdims=True)
        n_map = jnp.zeros_like(s[0])
        for p in range(PEER_TOPK):
            n_p = jnp.sum(jnp.where(cand[p] >= tau, 1.0, 0.0), axis=0, keepdims=True)
            n_map = jnp.where(s[0] == a0[p:p + 1, :], n_p, n_map)
        n_ref[h] = n_map
        e0_ref[h] = jnp.exp(s[0] - a0[0:1, :]) * (0.5 / z)
        r1_ref[h] = rank1.astype(BF16)
        e1_ref[h] = jnp.exp(s[1] - b[0:1, :]).astype(BF16)


def peer_route(x, gain, wqT, keys, tm=256):
    m, d = x.shape
    nq = wqT.shape[0]
    tab = jax.ShapeDtypeStruct((PEER_HEADS, PEER_NKEYS, m), F32)
    tab16 = jax.ShapeDtypeStruct((PEER_HEADS, PEER_NKEYS, m), BF16)
    tab_spec = pl.BlockSpec((PEER_HEADS, PEER_NKEYS, tm), lambda i: (0, 0, i))
    return pl.pallas_call(
        _peer_route_body,
        grid=(m // tm,),
        in_specs=[
            pl.BlockSpec((tm, d), lambda i: (i, 0)),
            pl.BlockSpec((1, d), lambda i: (0, 0)),
            pl.BlockSpec((nq, d), lambda i: (0, 0)),
            pl.BlockSpec(keys.shape, lambda i: (0, 0, 0)),
        ],
        out_specs=[pl.BlockSpec((d, tm), lambda i: (0, i)), tab_spec, tab_spec, tab_spec, tab_spec],
        out_shape=[jax.ShapeDtypeStruct((d, m), BF16), tab, tab, tab16, tab16],
        compiler_params=_cparams("parallel"),
        name="peer_route",
    )(x, gain.reshape(1, d), wqT, keys)


def _peer_expert_body(x_ref, hnT_ref, n_ref, e0_ref, r1_ref, e1_ref, u_ref, vT_ref, o_ref,
                      a_s, acc_ref, *, ni, nb):
    j = pl.program_id(1)

    @pl.when(j == 0)
    def _():
        a_s[...] = jnp.zeros_like(a_s)
        acc_ref[...] = jnp.zeros_like(acc_ref)

    tm = a_s.shape[2]
    pk = BF16_SUBLANES
    nk = PEER_NKEYS
    cur = j % 2
    prev = 1 - cur

    ib = jnp.minimum(j, nb - 1)
    zc = ni * nk // PEER_Z_CHUNKS
    z_chunks = [jnp.dot(u_ref[pl.ds(c * zc, zc), :], hnT_ref[...], preferred_element_type=F32)
                for c in range(PEER_Z_CHUNKS)]
    acc_ref[...] += jnp.dot(vT_ref[0], a_s[prev], preferred_element_type=F32)
    for ii in range(ni):
        i = ib * ni + ii
        r0 = ii * nk - (ii * nk // zc) * zc
        z = z_chunks[ii * nk // zc][r0:r0 + nk, :].astype(BF16)
        act = z * (1.0 + lax.erf(z * (1.0 / math.sqrt(2.0))))
        gate = None
        for h in range(PEER_HEADS):
            n_b = jnp.broadcast_to(n_ref[h, pl.ds(i, 1), :], (pk, tm)).astype(BF16)[None]
            e0_b = jnp.broadcast_to(e0_ref[h, pl.ds(i, 1), :], (pk, tm)).astype(BF16)[None]
            r1 = r1_ref[h].reshape(nk // pk, pk, tm)
            e1 = e1_ref[h].reshape(nk // pk, pk, tm)
            term = jnp.where(r1 <= n_b, e1 * e0_b, jnp.zeros((), BF16))
            gate = term if gate is None else gate + term
        a_s[cur, pl.ds(ii * nk, nk), :] = (act.reshape(nk // pk, pk, tm) * gate).reshape(nk, tm)

    @pl.when(j == nb)
    def _():
        o_ref[...] = x_ref[...] + acc_ref[...].T


def peer_experts(x, hnT, n_tab, e0_tab, r1_tab, e1_tab, u, v_tab, tm=512, ni=PEER_EXPERT_ROWS):
    m, d = x.shape
    ne = u.shape[0]
    et = ni * PEER_NKEYS
    nb = ne // et
    vT = jnp.transpose(v_tab.astype(BF16).reshape(nb, et, d), (0, 2, 1))
    tab_spec = pl.BlockSpec((PEER_HEADS, PEER_NKEYS, tm), lambda i, j: (0, 0, i))
    return pl.pallas_call(
        functools.partial(_peer_expert_body, ni=ni, nb=nb),
        grid=(m // tm, nb + 1),
        in_specs=[
            pl.BlockSpec((tm, d), lambda i, j: (i, 0)),
            pl.BlockSpec((d, tm), lambda i, j: (0, i)),
            tab_spec, tab_spec, tab_spec, tab_spec,
            pl.BlockSpec((et, d), lambda i, j: (jnp.minimum(j, nb - 1), 0)),
            pl.BlockSpec((1, d, et), lambda i, j: (jnp.clip(j - 1, 0, nb - 1), 0, 0)),
        ],
        out_specs=pl.BlockSpec((tm, d), lambda i, j: (i, 0)),
        out_shape=jax.ShapeDtypeStruct((m, d), F32),
        scratch_shapes=[pltpu.VMEM((2, et, tm), BF16), pltpu.VMEM((d, tm), F32)],
        compiler_params=_cparams("parallel", "arbitrary"),
        name="peer_experts",
    )(x, hnT, n_tab, e0_tab, r1_tab, e1_tab, u, vT)


def peer_layer(x, gain, w_q, keys, u_tab, v_tab):
    wqT = w_q.T.astype(BF16)
    keys2 = keys.reshape(PEER_HEADS * 2, PEER_NKEYS, PEER_HALF).astype(BF16)
    hnT, n_tab, e0_tab, r1_tab, e1_tab = peer_route(x, gain, wqT, keys2)
    return peer_experts(x, hnT, n_tab, e0_tab, r1_tab, e1_tab, u_tab.astype(BF16), v_tab)


def _rope_lanes(t, cos, sin_signed):
    half = SWA_HEADDIM // 2
    lane = lax.broadcasted_iota(jnp.int32, t.shape, 1)
    first = (lane % SWA_HEADDIM) < half
    partner = jnp.where(first, pltpu.roll(t, 128 - half, 1), pltpu.roll(t, half, 1))
    return t * cos + partner * sin_signed


def _swa_body(q_ref, k_ref, v_ref, cos_ref, sin_ref, sink_ref, o_ref, kp_ref, vp_ref):
    w = SWA_WINDOW
    t_len = q_ref.shape[1]
    nkv = SWA_KV_HEADS * SWA_HEADDIM
    rep = SWA_HEADS // SWA_KV_HEADS
    zeros = jnp.zeros((w, nkv), BF16)
    kp_ref[pl.ds(0, w), :] = zeros
    kp_ref[pl.ds(w + t_len, w), :] = zeros
    vp_ref[pl.ds(0, w), :] = zeros
    vp_ref[pl.ds(w + t_len, w), :] = zeros
    vp_ref[pl.ds(w, t_len), :] = v_ref[0]
    cos_all = cos_ref[...]
    sin_all = sin_ref[...]
    for c in range(nkv // 128):
        kc = k_ref[0, :, c * 128:(c + 1) * 128].astype(F32)
        kp_ref[pl.ds(w, t_len), c * 128:(c + 1) * 128] = _rope_lanes(kc, cos_all, sin_all).astype(BF16)

    row = lax.broadcasted_iota(jnp.int32, (w, 3 * w), 0)
    col = lax.broadcasted_iota(jnp.int32, (w, 3 * w), 1)
    band = (col >= row) & (col <= row + 2 * w)

    def block(n, carry):
        base = pl.multiple_of(n * w, w)
        cos_b = cos_ref[pl.ds(base, w), :]
        sin_b = sin_ref[pl.ds(base, w), :]
        kpos = base - w + col
        valid = band & (kpos >= 0) & (kpos < t_len)
        outs = []
        for g in range(SWA_KV_HEADS):
            kw = kp_ref[pl.ds(base, 3 * w), g * SWA_HEADDIM:(g + 1) * SWA_HEADDIM]
            vw = vp_ref[pl.ds(base, 3 * w), g * SWA_HEADDIM:(g + 1) * SWA_HEADDIM]
            scores = []
            for c in range(g * rep // 2, (g + 1) * rep // 2):
                qc = q_ref[0, pl.ds(base, w), c * 128:(c + 1) * 128].astype(F32)
                qc = (_rope_lanes(qc, cos_b, sin_b) * (SWA_HEADDIM ** -0.5)).astype(BF16)
                for hh in range(2):
                    qh = qc[:, hh * SWA_HEADDIM:(hh + 1) * SWA_HEADDIM]
                    scores.append(lax.dot_general(qh, kw, (((1,), (1,)), ((), ())), preferred_element_type=F32))
            probs = []
            for k, s in enumerate(scores):
                s = jnp.where(valid, s, NEG)
                sk = sink_ref[g * rep + k]
                m = jnp.maximum(jnp.max(s, axis=-1, keepdims=True), sk)
                p = jnp.exp(s - m)
                inv = 1.0 / (jnp.sum(p, axis=-1, keepdims=True) + jnp.exp(sk - m))
                probs.append((p.astype(BF16), inv))
            for p, inv in probs:
                outs.append(jnp.dot(p, vw, preferred_element_type=F32) * inv)
        o_ref[0, pl.ds(base, w), :] = jnp.concatenate(outs, axis=-1).astype(o_ref.dtype)
        return carry

    lax.fori_loop(0, t_len // w, block, 0)


def swa_attention(proj, sink, bsz, t_len):
    half = SWA_HEADDIM // 2
    inv_freq = ROPE_THETA ** (-jnp.arange(half, dtype=F32) / half)
    ang = jnp.arange(t_len, dtype=F32)[:, None] * inv_freq[None, :]
    cos, sin = jnp.cos(ang), jnp.sin(ang)
    cos_t = jnp.tile(jnp.concatenate([cos, cos], axis=-1), (1, 2))
    sin_t = jnp.tile(jnp.concatenate([-sin, sin], axis=-1), (1, 2))
    nq = SWA_HEADS * SWA_HEADDIM
    nkv = SWA_KV_HEADS * SWA_HEADDIM
    return pl.pallas_call(
        _swa_body,
        grid=(bsz,),
        in_specs=[
            pl.BlockSpec((1, t_len, nq), lambda b: (b, 0, 0)),
            pl.BlockSpec((1, t_len, nkv), lambda b: (b, 0, nq // nkv)),
            pl.BlockSpec((1, t_len, nkv), lambda b: (b, 0, nq // nkv + 1)),
            pl.BlockSpec((t_len, 128), lambda b: (0, 0)),
            pl.BlockSpec((t_len, 128), lambda b: (0, 0)),
            pl.BlockSpec(memory_space=pltpu.SMEM),
        ],
        out_specs=pl.BlockSpec((1, t_len, nq), lambda b: (b, 0, 0)),
        out_shape=jax.ShapeDtypeStruct((bsz, t_len, nq), BF16),
        scratch_shapes=[pltpu.VMEM((t_len + 2 * SWA_WINDOW, nkv), BF16),
                        pltpu.VMEM((t_len + 2 * SWA_WINDOW, nkv), BF16)],
        compiler_params=_cparams("parallel"),
        name="swa_attention",
    )(proj, proj, proj, cos_t, sin_t, sink.astype(F32))


def _na_bias_table(rpb):
    qc = np.arange(GRID_W)[:, None]
    kc = np.arange(GRID_W)[None, :]
    cstart = np.clip(qc - NA_COLS // 2, 0, GRID_W - NA_COLS)
    valid = (kc >= cstart) & (kc < cstart + NA_COLS)
    cidx = np.clip(kc - qc + NA_COLS - 1, 0, 2 * NA_COLS - 2)
    onehot = (np.arange(2 * NA_COLS - 1)[:, None, None] == cidx[None]).astype(np.float32)
    toep = jnp.einsum('hrc,cqk->hrqk', rpb.astype(F32), onehot, precision=lax.Precision.HIGHEST)
    toep = jnp.where(valid[None, None], toep, NEG)
    return jnp.concatenate([toep[:, :-1], toep[:, 1:]], axis=-1)


def _na_body(q_ref, k_ref, v_ref, bias_ref, o_ref):
    t_len = q_ref.shape[1]
    rows = t_len // GRID_W
    win = NA_ROWS * GRID_W

    def row_group(gi, carry):
        chains = []
        for rr in range(NA_ROW_UNROLL):
            r = gi * NA_ROW_UNROLL + rr
            rs = jnp.clip(r - NA_ROWS // 2, 0, rows - NA_ROWS)
            d0 = rs - r + NA_ROWS - 1
            qbase = pl.multiple_of(r * GRID_W, GRID_W)
            kbase = pl.multiple_of(rs * GRID_W, GRID_W)
            qr = q_ref[0, pl.ds(qbase, GRID_W), :]
            kw = k_ref[0, pl.ds(kbase, win), :]
            for hh in range(2):
                sl = slice(hh * NA_HEADDIM, (hh + 1) * NA_HEADDIM)
                s = lax.dot_general(qr[:, sl], kw[:, sl], (((1,), (1,)), ((), ())), preferred_element_type=F32)
                chains.append((hh, d0, kbase, s))
        probs = []
        for hh, d0, kbase, s in chains:
            bias = jnp.concatenate([bias_ref[hh, d0 + 2 * c] for c in range(NA_ROWS // 2)], axis=-1)
            s = s * (NA_HEADDIM ** -0.5) + bias
            m = jnp.max(s, axis=-1, keepdims=True)
            p = jnp.exp(s - m)
            probs.append((p.astype(BF16), 1.0 / jnp.sum(p, axis=-1, keepdims=True)))
        outs = []
        for (hh, d0, kbase, s), (p, inv) in zip(chains, probs):
            vw = v_ref[0, pl.ds(kbase, win), hh * NA_HEADDIM:(hh + 1) * NA_HEADDIM]
            outs.append(jnp.dot(p, vw, preferred_element_type=F32) * inv)
        for rr in range(NA_ROW_UNROLL):
            qbase = pl.multiple_of((gi * NA_ROW_UNROLL + rr) * GRID_W, GRID_W)
            o_ref[0, pl.ds(qbase, GRID_W), :] = jnp.concatenate(outs[2 * rr:2 * rr + 2], axis=-1).astype(o_ref.dtype)
        return carry

    lax.fori_loop(0, rows // NA_ROW_UNROLL, row_group, 0)


def na_attention(proj, rpb, bsz, t_len):
    bias = _na_bias_table(rpb)
    npair = NA_HEADS // 2
    return pl.pallas_call(
        _na_body,
        grid=(npair, bsz),
        in_specs=[
            pl.BlockSpec((1, t_len, 128), lambda hp, b: (b, 0, hp)),
            pl.BlockSpec((1, t_len, 128), lambda hp, b: (b, 0, npair + hp)),
            pl.BlockSpec((1, t_len, 128), lambda hp, b: (b, 0, 2 * npair + hp)),
            pl.BlockSpec((2, 2 * NA_ROWS - 2, GRID_W, 2 * GRID_W), lambda hp, b: (hp, 0, 0, 0)),
        ],
        out_specs=pl.BlockSpec((1, t_len, 128), lambda hp, b: (b, 0, hp)),
        out_shape=jax.ShapeDtypeStruct((bsz, t_len, NA_WIDTH), BF16),
        compiler_params=_cparams("parallel", "parallel"),
        name="na_attention",
    )(proj, proj, proj, bias)


def _conv_silu_body(x_ref, w_ref, b_ref, o_ref, *, n_l2):
    t_len, tc = x_ref.shape[1], x_ref.shape[2]
    rc = CONV_ROW_CHUNK
    halo = F32_SUBLANES
    normalise = pl.program_id(1) < n_l2

    def chunk(ci, carry):
        r0 = pl.multiple_of(ci * rc, rc)
        cur = x_ref[0, pl.ds(r0, rc), :].astype(F32)
        lo = pl.multiple_of(jnp.maximum(r0 - halo, 0), halo)
        hi = pl.multiple_of(jnp.minimum(r0 + rc, t_len - halo), halo)
        before = jnp.where(r0 > 0, x_ref[0, pl.ds(lo, halo), :].astype(F32), 0.0)
        after = jnp.where(r0 + rc < t_len, x_ref[0, pl.ds(hi, halo), :].astype(F32), 0.0)
        xx = jnp.concatenate([before, cur, after], axis=0)
        acc = cur * w_ref[CONV_K // 2:CONV_K // 2 + 1, :] + b_ref[...]
        for k in range(CONV_K):
            off = k - CONV_K // 2
            if off == 0:
                continue
            acc = acc + xx[halo + off:halo + off + rc, :] * w_ref[k:k + 1, :]
        y = acc * jax.nn.sigmoid(acc)
        if n_l2 > 0:
            parts = []
            for c in range(tc // 128):
                yc = y[:, c * 128:(c + 1) * 128]
                ss = jnp.sum(yc * yc, axis=-1, keepdims=True)
                parts.append(yc * jnp.where(normalise, lax.rsqrt(ss + NORM_EPS), 1.0))
            y = jnp.concatenate(parts, axis=-1)
        o_ref[0, pl.ds(r0, rc), :] = y.astype(o_ref.dtype)
        return carry

    lax.fori_loop(0, t_len // rc, chunk, 0)


def conv_silu(proj, w, bias, col0, n_ch, n_l2=0, tc=512):
    bsz, t_len, _ = proj.shape
    assert col0 % tc == 0 and n_ch % tc == 0
    c0 = col0 // tc
    return pl.pallas_call(
        functools.partial(_conv_silu_body, n_l2=n_l2),
        grid=(bsz, n_ch // tc),
        in_specs=[
            pl.BlockSpec((1, t_len, tc), lambda b, j: (b, 0, c0 + j)),
            pl.BlockSpec((CONV_K, tc), lambda b, j: (0, j)),
            pl.BlockSpec((1, tc), lambda b, j: (0, j)),
        ],
        out_specs=pl.BlockSpec((1, t_len, tc), lambda b, j: (b, 0, j)),
        out_shape=jax.ShapeDtypeStruct((bsz, t_len, n_ch), BF16),
        compiler_params=_cparams("parallel", "parallel"),
        name="conv_silu",
    )(proj, w.astype(F32), bias.astype(F32).reshape(1, n_ch))


def _softplus(x):
    return jnp.maximum(x, 0.0) + jnp.log1p(jnp.exp(-jnp.abs(x)))


def _bmm(a, b, precision=None):
    return lax.dot_general(a, b, (((2,), (1,)), ((0,), (0,))), precision=precision,
                           preferred_element_type=F32)


def _split_bf16(a):
    hi = a.astype(BF16)
    return hi, (a - hi.astype(F32)).astype(BF16)


def _split3_bf16(a):
    hi = a.astype(BF16)
    rest = a - hi.astype(F32)
    mid = rest.astype(BF16)
    return hi, mid, (rest - mid.astype(F32)).astype(BF16)


def _bmm_split(a, b):
    return _bmm(a[0], b[0]) + _bmm(a[0], b[1]) + _bmm(a[1], b[0])


def _bmm_nt(a, b):
    return lax.dot_general(a, b, (((2,), (2,)), ((0,), (0,))), preferred_element_type=F32)


def _gdn_body(q_ref, k_ref, v_ref, z_ref, gcol_ref, grow_ref, alog_ref, dtb_ref, nw_ref, o_ref,
              u_s, w_s, qk_s, qg_s, kg_s, el_s, o_s):
    c_len = GDN_CHUNK
    t_len = q_ref.shape[1]
    nc = t_len // c_len
    h = pl.program_id(1)
    ii = lax.broadcasted_iota(jnp.int32, (c_len, c_len), 0)
    jj = lax.broadcasted_iota(jnp.int32, (c_len, c_len), 1)
    eye = (ii == jj).astype(F32)
    cg = GDN_PHASE_A_CHUNKS

    def phase_a(gi, carry):
        c0 = pl.multiple_of(gi * cg, cg)
        rows = pl.ds(pl.multiple_of(gi * (cg * c_len), cg * c_len), cg * c_len)
        k3 = k_ref[0, rows, :].reshape(cg, c_len, GDN_DK)
        v3 = v_ref[0, rows, :].reshape(cg, c_len, GDN_DK).astype(F32)
        kf = k3.astype(F32)
        qs = q_ref[0, rows, :].reshape(cg, c_len, GDN_DK).astype(F32) * (GDN_DK ** -0.5)
        gcol = gcol_ref[0, 0, pl.ds(c0, cg)]
        grow = grow_ref[0, 0, pl.ds(c0, cg)]
        qk_raw = _bmm_nt(qs.astype(BF16), k3)
        kb, gc_c, g_last, decay, strict = [], [], [], [], []
        for d in range(2):
            incl = (ii >= jj) if d == 0 else (ii <= jj)
            strict.append((ii > jj) if d == 0 else (ii < jj))
            tri = incl.astype(F32)
            tri_t = ((ii <= jj) if d == 0 else (ii >= jj)).astype(F32)
            neg_a = -jnp.exp(jnp.full((1, 1, 1), alog_ref[d, h], F32))
            dtb = dtb_ref[d, h]
            g_c = neg_a * _softplus(gcol[:, :, d:d + 1] + dtb)
            g_r = neg_a * _softplus(grow[:, d:d + 1, :] + dtb)
            beta_c = jax.nn.sigmoid(gcol[:, :, 2 + d:3 + d])
            gc_c.append(jnp.sum(tri[None] * g_r, axis=2, keepdims=True))
            gc_r = jnp.sum(tri_t[None] * g_c, axis=1, keepdims=True)
            g_last.append(jnp.sum(g_r, axis=2, keepdims=True))
            decay.append(jnp.where(incl[None], jnp.exp(jnp.where(incl[None], gc_c[d] - gc_r, 0.0)), 0.0))
            kb.append((kf * beta_c, v3 * beta_c))
        kk = [_bmm_nt(kb[d][0].astype(BF16), k3) for d in range(2)]
        low = [jnp.where(strict[d][None], kk[d] * decay[d], 0.0) for d in range(2)]
        inv = [eye[None] - low[d] for d in range(2)]
        pw = [_split_bf16(low[d]) for d in range(2)]
        for _ in range(5):
            pw = [_split_bf16(_bmm_split(pw[d], pw[d])) for d in range(2)]
            inv = [inv[d] + _bmm_split(_split_bf16(inv[d]), pw[d]) for d in range(2)]
        rhs = [jnp.concatenate([kb[d][1], kb[d][0] * jnp.exp(gc_c[d])], axis=-1) for d in range(2)]
        sol = [_bmm_split(_split_bf16(inv[d]), _split_bf16(rhs[d])) for d in range(2)]
        for d in range(2):
            u_s[d, pl.ds(c0, cg)] = sol[d][:, :, :GDN_DK]
            w_s[d, pl.ds(c0, cg)] = sol[d][:, :, GDN_DK:].astype(BF16)
            qk_s[d, pl.ds(c0, cg)] = (qk_raw * decay[d]).astype(BF16)
            qg_s[d, pl.ds(c0, cg)] = (qs * jnp.exp(gc_c[d])).astype(BF16)
            kg_s[d, pl.ds(c0, cg)] = (kf * jnp.exp(g_last[d] - gc_c[d])).astype(BF16)
            el_s[d, pl.ds(c0, cg)] = jnp.broadcast_to(jnp.exp(g_last[d]), (cg, 1, GDN_DK))
        return carry

    lax.fori_loop(0, nc // cg, phase_a, 0)

    def step(t, carry):
        cs = (t, nc - 1 - t)
        sb = [carry[d].astype(BF16) for d in range(2)]
        ws = [jnp.dot(w_s[d, cs[d]], sb[d], preferred_element_type=F32) for d in range(2)]
        qs_ = [jnp.dot(qg_s[d, cs[d]], sb[d], preferred_element_type=F32) for d in range(2)]
        vb = [(u_s[d, cs[d]] - ws[d]).astype(BF16) for d in range(2)]
        os_ = [qs_[d] + jnp.dot(qk_s[d, cs[d]], vb[d], preferred_element_type=F32) for d in range(2)]
        kv = [lax.dot_general(kg_s[d, cs[d]], vb[d], (((0,), (0,)), ((), ())), preferred_element_type=F32)
              for d in range(2)]
        for d in range(2):
            o_s[d, cs[d]] = os_[d]
        return tuple(carry[d] * el_s[d, cs[d]] + kv[d] for d in range(2))

    s0 = jnp.zeros((GDN_DK, GDN_DK), F32)
    lax.fori_loop(0, nc, step, (s0, s0))

    o = (o_s[0] + o_s[1]).reshape(t_len, GDN_DK)
    ms = jnp.mean(o * o, axis=-1, keepdims=True)
    z = z_ref[0]
    y = o * lax.rsqrt(ms + NORM_EPS) * nw_ref[...] * (z * jax.nn.sigmoid(z))
    o_ref[0] = y.astype(o_ref.dtype)


def gdn_scan(qkv, proj, gates, a_log, dt_bias, norm_w, bsz, t_len):
    nc = t_len // GDN_CHUNK
    nh = GDN_HEADS
    g4 = gates[:, :4 * nh].reshape(bsz, nc, GDN_CHUNK, 4, nh)
    gcol = jnp.transpose(g4, (0, 4, 1, 2, 3))
    grow = jnp.transpose(g4, (0, 4, 1, 3, 2))
    dk = GDN_DK
    nq = GDN_QK // dk
    return pl.pallas_call(
        _gdn_body,
        grid=(bsz, nh),
        in_specs=[
            pl.BlockSpec((1, t_len, dk), lambda b, h: (b, 0, h)),
            pl.BlockSpec((1, t_len, dk), lambda b, h: (b, 0, nq + h)),
            pl.BlockSpec((1, t_len, dk), lambda b, h: (b, 0, 2 * nq + h)),
            pl.BlockSpec((1, t_len, dk), lambda b, h: (b, 0, 3 * nq + h)),
            pl.BlockSpec((1, 1, nc, GDN_CHUNK, 4), lambda b, h: (b, h, 0, 0, 0)),
            pl.BlockSpec((1, 1, nc, 4, GDN_CHUNK), lambda b, h: (b, h, 0, 0, 0)),
            pl.BlockSpec(memory_space=pltpu.SMEM),
            pl.BlockSpec(memory_space=pltpu.SMEM),
            pl.BlockSpec((1, dk), lambda b, h: (0, 0)),
        ],
        out_specs=pl.BlockSpec((1, t_len, dk), lambda b, h: (b, 0, h)),
        out_shape=jax.ShapeDtypeStruct((bsz, t_len, GDN_V), BF16),
        scratch_shapes=[
            pltpu.VMEM((2, nc, GDN_CHUNK, dk), F32),
            pltpu.VMEM((2, nc, GDN_CHUNK, dk), BF16),
            pltpu.VMEM((2, nc, GDN_CHUNK, GDN_CHUNK), BF16),
            pltpu.VMEM((2, nc, GDN_CHUNK, dk), BF16),
            pltpu.VMEM((2, nc, GDN_CHUNK, dk), BF16),
            pltpu.VMEM((2, nc, 1, dk), F32),
            pltpu.VMEM((2, nc, GDN_CHUNK, dk), F32),
        ],
        compiler_params=_cparams("parallel", "parallel"),
        name="gdn_scan",
    )(qkv, qkv, qkv, proj, gcol, grow, a_log.astype(F32), dt_bias.astype(F32),
      norm_w.astype(F32).reshape(1, dk))


def _pad_cols(w, n):
    return jnp.pad(w, ((0, 0), (0, n - w.shape[1])))


def gdn_layer(x, gain, w_in, conv_w, a_log, dt_bias, norm_w, w_out, bsz, t_len):
    n_main = GDN_CONV_CH + GDN_V
    proj = norm_matmul(x, gain, w_in[:, :n_main].astype(BF16), F32)
    gates = norm_matmul(x, gain, _pad_cols(w_in[:, n_main:], 128).astype(BF16), F32)
    proj3 = proj.reshape(bsz, t_len, n_main)
    qkv = conv_silu(proj3, conv_w, jnp.zeros((GDN_CONV_CH,), F32), 0, GDN_CONV_CH,
                    n_l2=2 * GDN_QK // 512)
    y = gdn_scan(qkv, proj3, gates, a_log, dt_bias, norm_w, bsz, t_len)
    return matmul_residual(x, y.reshape(bsz * t_len, GDN_V), w_out.astype(BF16))


def _ssd_body(xs_ref, b_ref, c_ref, z_ref, dcol_ref, drow_ref, alogc_ref, alogr_ref, dtbc_ref, dtbr_ref, dskip_ref,
              nw_ref, o_ref, xw_s, ea_s, el_s, y_s, st_s):
    assert SSD_CHUNK == SSD_HEADDIM
    c_len = SSD_CHUNK
    t_len = xs_ref.shape[1]
    nc = t_len // c_len
    nr = SSD_HEADS // SSD_GROUPS
    hp = SSD_HEADDIM
    g = pl.program_id(1)
    ii = lax.broadcasted_iota(jnp.int32, (c_len, c_len), 0)
    jj = lax.broadcasted_iota(jnp.int32, (c_len, c_len), 1)
    cg = SSD_PHASE_A_CHUNKS

    def phase_a(gi, carry):
        c0 = pl.multiple_of(gi * cg, cg)
        rows = pl.ds(pl.multiple_of(gi * (cg * c_len), cg * c_len), cg * c_len)
        x3 = xs_ref[0, rows, :].reshape(cg, c_len, nr * hp)
        b3 = b_ref[0, rows, :].reshape(cg, c_len, SSD_STATE)
        c3 = c_ref[0, rows, :].reshape(cg, c_len, SSD_STATE)
        dcol = dcol_ref[0, 0, pl.ds(c0, cg)]
        drow = drow_ref[0, 0, pl.ds(c0, cg)]
        cb = _bmm_nt(c3, b3)
        dt_c = _softplus(dcol + dtbc_ref[0])
        dt_r = _softplus(drow + dtbr_ref[0])
        da_c = dt_c * -jnp.exp(alogc_ref[0])
        da_r = dt_r * -jnp.exp(alogr_ref[0])
        tri_f = jnp.broadcast_to((ii >= jj).astype(BF16)[None], (cg, c_len, c_len))
        tri_b = jnp.broadcast_to((ii <= jj).astype(BF16)[None], (cg, c_len, c_len))
        fwd_c = lax.broadcasted_iota(jnp.int32, (1, 1, 2 * nr), 2) < nr
        fwd_r = lax.broadcasted_iota(jnp.int32, (1, 2 * nr, 1), 1) < nr
        da_c3, da_r3 = _split3_bf16(da_c), _split3_bf16(da_r)
        ac_c = jnp.where(fwd_c, sum(_bmm(tri_f, p) for p in da_c3), sum(_bmm(tri_b, p) for p in da_c3))
        ac_r = jnp.where(fwd_r, sum(_bmm(p, tri_b) for p in da_r3), sum(_bmm(p, tri_f) for p in da_r3))
        ydiag = None
        for d in range(2):
            incl = (ii >= jj) if d == 0 else (ii <= jj)
            sel = (lax.broadcasted_iota(jnp.int32, (2 * nr, nr * hp), 1) // hp + d * nr
                   == lax.broadcasted_iota(jnp.int32, (2 * nr, nr * hp), 0)).astype(BF16)

            def spread(cols, pieces):
                parts = pieces(cols.reshape(cg * c_len, 2 * nr))
                return sum(jnp.dot(p, sel, preferred_element_type=F32) for p in parts).reshape(cg, c_len, nr * hp)

            dt_w = spread(dt_c, _split_bf16)
            ac_w = spread(ac_c, _split3_bf16)
            last = ac_w[:, c_len - 1:c_len, :] if d == 0 else ac_w[:, 0:1, :]
            xr = x3.astype(F32) * dt_w
            xw_s[d, pl.ds(c0, cg)] = (xr * jnp.exp(last - ac_w)).astype(BF16)
            ea_s[d, pl.ds(c0, cg)] = jnp.exp(ac_w)
            el_s[d, pl.ds(c0, cg)] = jnp.exp(last)
            xr = xr.astype(BF16)
            yd = []
            for r in range(nr):
                col = d * nr + r
                diff = ac_w[:, :, r * hp:r * hp + c_len] - ac_r[:, col:col + 1, :]
                seg = jnp.where(incl[None], jnp.exp(jnp.where(incl[None], diff, 0.0)), 0.0)
                yd.append(_bmm((cb * seg).astype(BF16), xr[:, :, r * hp:(r + 1) * hp]))
            yd = jnp.concatenate(yd, axis=-1)
            ydiag = yd if ydiag is None else ydiag + yd
        y_s[pl.ds(c0, cg)] = ydiag
        return carry

    lax.fori_loop(0, nc // cg, phase_a, 0)

    st_s[...] = jnp.zeros_like(st_s)

    def step(t, carry):
        for d in range(2):
            c = t if d == 0 else nc - 1 - t
            base = pl.multiple_of(c * c_len, c_len)
            cc = c_ref[0, pl.ds(base, c_len), :]
            bb = b_ref[0, pl.ds(base, c_len), :]
            st = st_s[d]
            y_s[c] += jnp.dot(cc, st.astype(BF16), preferred_element_type=F32) * ea_s[d, c]
            st_s[d] = st * el_s[d, c] + lax.dot_general(bb, xw_s[d, c], (((0,), (0,)), ((), ())),
                                                        preferred_element_type=F32)
        return carry

    lax.fori_loop(0, nc, step, 0)

    xs = xs_ref[0].astype(F32)
    z = z_ref[0]
    y = (y_s[...].reshape(t_len, nr * hp) + dskip_ref[...] * xs) * (z * jax.nn.sigmoid(z))
    ms = jnp.mean(y * y, axis=-1, keepdims=True)
    o_ref[0] = (y * lax.rsqrt(ms + NORM_EPS) * nw_ref[...]).astype(o_ref.dtype)


def ssd_scan(xbc, proj, dts, a_log, dt_bias, d_skip, norm_w, bsz, t_len):
    nc = t_len // SSD_CHUNK
    ng = SSD_GROUPS
    nr = SSD_HEADS // ng
    gw = nr * SSD_HEADDIM
    d6 = dts[:, :2 * SSD_HEADS].reshape(bsz, nc, SSD_CHUNK, 2, ng, nr)
    dcol = jnp.transpose(d6, (0, 4, 1, 2, 3, 5)).reshape(bsz, ng, nc, SSD_CHUNK, 2 * nr)
    drow = jnp.transpose(d6, (0, 4, 1, 3, 5, 2)).reshape(bsz, ng, nc, 2 * nr, SSD_CHUNK)
    nb0 = SSD_INNER // SSD_STATE
    dtb_g = jnp.transpose(dt_bias.astype(F32).reshape(2, ng, nr), (1, 0, 2)).reshape(ng, 2 * nr)
    alog_g = jnp.transpose(a_log.astype(F32).reshape(2, ng, nr), (1, 0, 2)).reshape(ng, 2 * nr)
    return pl.pallas_call(
        _ssd_body,
        grid=(bsz, ng),
        in_specs=[
            pl.BlockSpec((1, t_len, gw), lambda b, g: (b, 0, g)),
            pl.BlockSpec((1, t_len, SSD_STATE), lambda b, g: (b, 0, nb0 + g)),
            pl.BlockSpec((1, t_len, SSD_STATE), lambda b, g: (b, 0, nb0 + ng + g)),
            pl.BlockSpec((1, t_len, gw), lambda b, g: (b, 0, g)),
            pl.BlockSpec((1, 1, nc, SSD_CHUNK, 2 * nr), lambda b, g: (b, g, 0, 0, 0)),
            pl.BlockSpec((1, 1, nc, 2 * nr, SSD_CHUNK), lambda b, g: (b, g, 0, 0, 0)),
            pl.BlockSpec((1, 1, 2 * nr), lambda b, g: (g, 0, 0)),
            pl.BlockSpec((1, 2 * nr, 1), lambda b, g: (g, 0, 0)),
            pl.BlockSpec((1, 1, 2 * nr), lambda b, g: (g, 0, 0)),
            pl.BlockSpec((1, 2 * nr, 1), lambda b, g: (g, 0, 0)),
            pl.BlockSpec((1, gw), lambda b, g: (0, g)),
            pl.BlockSpec((1, gw), lambda b, g: (0, g)),
        ],
        out_specs=pl.BlockSpec((1, t_len, gw), lambda b, g: (b, 0, g)),
        out_shape=jax.ShapeDtypeStruct((bsz, t_len, SSD_INNER), BF16),
        scratch_shapes=[
            pltpu.VMEM((2, nc, SSD_CHUNK, gw), BF16),
            pltpu.VMEM((2, nc, SSD_CHUNK, gw), F32),
            pltpu.VMEM((2, nc, 1, gw), F32),
            pltpu.VMEM((nc, SSD_CHUNK, gw), F32),
            pltpu.VMEM((2, SSD_STATE, gw), F32),
        ],
        compiler_params=_cparams("parallel", "parallel"),
        name="ssd_scan",
    )(xbc, xbc, xbc, proj, dcol, drow, alog_g.reshape(ng, 1, 2 * nr), alog_g.reshape(ng, 2 * nr, 1),
      dtb_g.reshape(ng, 1, 2 * nr), dtb_g.reshape(ng, 2 * nr, 1),
      jnp.repeat(d_skip.astype(F32), SSD_HEADDIM).reshape(1, SSD_INNER),
      norm_w.astype(F32).reshape(1, SSD_INNER))


def ssd_layer(x, gain, w_in, conv_w, conv_b, a_log, dt_bias, d_skip, norm_w, w_out, bsz, t_len):
    n_main = SSD_INNER + SSD_CONV_CH
    proj = norm_matmul(x, gain, w_in[:, :n_main].astype(BF16), F32)
    dts = norm_matmul(x, gain, _pad_cols(w_in[:, n_main:], 128).astype(BF16), F32)
    proj3 = proj.reshape(bsz, t_len, n_main)
    xbc = conv_silu(proj3, conv_w, conv_b, SSD_INNER, SSD_CONV_CH)
    y = ssd_scan(xbc, proj3, dts, a_log, dt_bias, d_skip, norm_w, bsz, t_len)
    return matmul_residual(x, y.reshape(bsz * t_len, SSD_INNER), w_out.astype(BF16))


def swa_layer(x, gain, w_in, sink, w_out, bsz, t_len):
    proj = norm_matmul(x, gain, w_in.astype(BF16), BF16)
    o = swa_attention(proj.reshape(bsz, t_len, -1), sink, bsz, t_len)
    return matmul_residual(x, o.reshape(bsz * t_len, -1), w_out.astype(BF16))


def na_layer(x, gain, w_in, rpb, w_out, bsz, t_len):
    proj = norm_matmul(x, gain, w_in.astype(BF16), BF16)
    o = na_attention(proj.reshape(bsz, t_len, -1), rpb, bsz, t_len)
    return matmul_residual(x, o.reshape(bsz * t_len, -1), w_out.astype(BF16))


def kernel(x, norm_mix, norm_ffn, norm_final, gdn_w_in, gdn_conv, gdn_a_log, gdn_dt_bias, gdn_norm, gdn_w_out, ssd_w_in, ssd_conv, ssd_conv_b, ssd_a_log, ssd_dt_bias, ssd_d, ssd_norm, ssd_w_out, swa_w_in, swa_sink, swa_w_out, na_w_in, na_rpb, na_w_out, peer_w_q, peer_keys, peer_u, peer_v):
    bsz, t_len, d = x.shape
    depth = norm_mix.shape[0]
    xf = x.reshape(bsz * t_len, d)
    for i in range(depth):
        mixer, j = i % 4, i // 4
        if mixer == 0:
            xf = gdn_layer(xf, norm_mix[i], gdn_w_in[j], gdn_conv[j], gdn_a_log[j], gdn_dt_bias[j],
                           gdn_norm[j], gdn_w_out[j], bsz, t_len)
        elif mixer == 1:
            xf = ssd_layer(xf, norm_mix[i], ssd_w_in[j], ssd_conv[j], ssd_conv_b[j], ssd_a_log[j],
                           ssd_dt_bias[j], ssd_d[j], ssd_norm[j], ssd_w_out[j], bsz, t_len)
        elif mixer == 2:
            xf = swa_layer(xf, norm_mix[i], swa_w_in[j], swa_sink[j], swa_w_out[j], bsz, t_len)
        else:
            xf = na_layer(xf, norm_mix[i], na_w_in[j], na_rpb[j], na_w_out[j], bsz, t_len)
        xf = peer_layer(xf, norm_ffn[i], peer_w_q[i], peer_keys[i], peer_u[i], peer_v[i])
    return final_norm(xf, norm_final).reshape(bsz, t_len, d)
```

```python
import functools
import math

import jax
import jax.numpy as jnp
import numpy as np
from jax import lax
from jax.experimental import pallas as pl
from jax.experimental.pallas import tpu as pltpu

F32 = jnp.float32
BF16 = jnp.bfloat16
NEG = -1e30

D_MODEL = 1024
SEQ = 2048
GRID_W = 64
CONV_K = 5
NORM_EPS = 1e-6

GDN_HEADS = 8
GDN_DK = 128
GDN_CHUNK = 64
GDN_QK = 1024
GDN_V = 1024
GDN_CONV_CH = 3072

SSD_INNER = 2048
SSD_HEADDIM = 64
SSD_HEADS = 32
SSD_GROUPS = 4
SSD_STATE = 128
SSD_CHUNK = 64
SSD_BC = 512
SSD_CONV_CH = 3072

SWA_HEADS = 16
SWA_KV_HEADS = 4
SWA_HEADDIM = 64
SWA_WINDOW = 128
ROPE_THETA = 10000.0

NA_HEADS = 16
NA_HEADDIM = 64
NA_ROWS = 8
NA_COLS = 16
NA_WIDTH = 1024

PEER_HEADS = 8
PEER_NKEYS = 128
PEER_QDIM = 256
PEER_HALF = 128
PEER_TOPK = 16

GDN_PHASE_A_CHUNKS = 8
SSD_PHASE_A_CHUNKS = 8

PEER_EXPERT_ROWS = 8
PEER_Z_CHUNKS = 4

NA_ROW_UNROLL = 4

BF16_SUBLANES = 16
F32_SUBLANES = 8

CONV_ROW_CHUNK = 256

VMEM_LIMIT_BYTES = 52 * 1024 * 1024


def _cparams(*sem, flags=None):
    return pltpu.CompilerParams(dimension_semantics=sem, vmem_limit_bytes=VMEM_LIMIT_BYTES, flags=flags)


def _norm_matmul_body(x_ref, g_ref, w_ref, o_ref, hn_ref):
    @pl.when(pl.program_id(1) == 0)
    def _():
        x = x_ref[...]
        ms = jnp.mean(x * x, axis=-1, keepdims=True)
        hn_ref[...] = (x * lax.rsqrt(ms + NORM_EPS) * g_ref[...]).astype(BF16)

    o_ref[...] = jnp.dot(hn_ref[...], w_ref[...], preferred_element_type=F32).astype(o_ref.dtype)


def norm_matmul(x, gain, w, out_dtype, tm=1024, tn=1024):
    m, d = x.shape
    n = w.shape[1]
    tn = min(tn, n)
    while n % tn:
        tn //= 2
    assert m % tm == 0 and tn % 128 == 0
    return pl.pallas_call(
        _norm_matmul_body,
        grid=(m // tm, n // tn),
        in_specs=[
            pl.BlockSpec((tm, d), lambda i, j: (i, 0)),
            pl.BlockSpec((1, d), lambda i, j: (0, 0)),
            pl.BlockSpec((d, tn), lambda i, j: (0, j)),
        ],
        out_specs=pl.BlockSpec((tm, tn), lambda i, j: (i, j)),
        out_shape=jax.ShapeDtypeStruct((m, n), out_dtype),
        scratch_shapes=[pltpu.VMEM((tm, d), BF16)],
        compiler_params=_cparams("parallel", "arbitrary"),
        name="norm_matmul",
    )(x, gain.reshape(1, d), w)


def _matmul_residual_body(x_ref, y_ref, w_ref, o_ref):
    o_ref[...] = x_ref[...] + jnp.dot(y_ref[...], w_ref[...], preferred_element_type=F32)


def matmul_residual(x, y, w, tm=1024):
    m, d = x.shape
    k = y.shape[1]
    return pl.pallas_call(
        _matmul_residual_body,
        grid=(m // tm,),
        in_specs=[
            pl.BlockSpec((tm, d), lambda i: (i, 0)),
            pl.BlockSpec((tm, k), lambda i: (i, 0)),
            pl.BlockSpec((k, d), lambda i: (0, 0)),
        ],
        out_specs=pl.BlockSpec((tm, d), lambda i: (i, 0)),
        out_shape=jax.ShapeDtypeStruct((m, d), F32),
        compiler_params=_cparams("parallel"),
        name="matmul_residual",
    )(x, y, w)


def _final_norm_body(x_ref, g_ref, o_ref):
    x = x_ref[...]
    ms = jnp.mean(x * x, axis=-1, keepdims=True)
    o_ref[...] = x * lax.rsqrt(ms + NORM_EPS) * g_ref[...]


def final_norm(x, gain, tm=1024):
    m, d = x.shape
    return pl.pallas_call(
        _final_norm_body,
        grid=(m // tm,),
        in_specs=[pl.BlockSpec((tm, d), lambda i: (i, 0)), pl.BlockSpec((1, d), lambda i: (0, 0))],
        out_specs=pl.BlockSpec((tm, d), lambda i: (i, 0)),
        out_shape=jax.ShapeDtypeStruct((m, d), F32),
        compiler_params=_cparams("parallel"),
        name="final_norm",
    )(x, gain.reshape(1, d))


def _top16_desc(s, with_rank):
    work = s
    rank = jnp.full(s.shape, 99.0, F32) if with_rank else None
    vals = []
    for r in range(PEER_TOPK):
        m = jnp.max(work, axis=0, keepdims=True)
        eq = work == m
        if with_rank:
            rank = jnp.where(eq, float(r + 1), rank)
        work = jnp.where(eq, -jnp.inf, work)
        vals.append(m)
    return jnp.concatenate(vals, axis=0), rank


def _peer_route_body(x_ref, g_ref, wqT_ref, keys_ref, hnT_ref, n_ref, e0_ref, r1_ref, e1_ref):
    x = x_ref[...]
    ms = jnp.mean(x * x, axis=-1, keepdims=True)
    hn = x * lax.rsqrt(ms + NORM_EPS) * g_ref[...]
    hnT = hn.T.astype(BF16)
    hnT_ref[...] = hnT
    qT = jnp.dot(wqT_ref[...], hnT, preferred_element_type=F32)
    row8 = lax.broadcasted_iota(jnp.int32, (8, x.shape[0]), 0)
    for h in range(PEER_HEADS):
        s = []
        for p in range(2):
            hp = 2 * h + p
            q_hp = qT[hp * PEER_HALF:(hp + 1) * PEER_HALF, :].astype(BF16)
            s.append(jnp.dot(keys_ref[hp], q_hp, preferred_element_type=F32))
        a0, _ = _top16_desc(s[0], False)
        b, rank1 = _top16_desc(s[1], True)
        cand = [a0[0:1, :] + b]
        for p in range(1, PEER_TOPK):
            cnt = PEER_TOPK // (p + 1)
            c = a0[p:p + 1, :] + b[0:8, :]
            cand.append(c if cnt >= 8 else jnp.where(row8 < cnt, c, -jnp.inf))
        best, _ = _top16_desc(jnp.concatenate(cand, axis=0), False)
        tau = best[PEER_TOPK - 1:PEER_TOPK, :]
        z = jnp.sum(jnp.exp(best - best[0:1, :]), axis=0, keepdims=True)
        n_map = jnp.zeros_like(s[0])
        for p in range(PEER_TOPK):
            n_p = jnp.sum(jnp.where(cand[p] >= tau, 1.0, 0.0), axis=0, keepdims=True)
            n_map = jnp.where(s[0] == a0[p:p + 1, :], n_p, n_map)
        n_ref[h] = n_map
        e0_ref[h] = jnp.exp(s[0] - a0[0:1, :]) * (0.5 / z)
        r1_ref[h] = rank1.astype(BF16)
        e1_ref[h] = jnp.exp(s[1] - b[0:1, :]).astype(BF16)


def peer_route(x, gain, wqT, keys, tm=256):
    m, d = x.shape
    nq = wqT.shape[0]
    tab = jax.ShapeDtypeStruct((PEER_HEADS, PEER_NKEYS, m), F32)
    tab16 = jax.ShapeDtypeStruct((PEER_HEADS, PEER_NKEYS, m), BF16)
    tab_spec = pl.BlockSpec((PEER_HEADS, PEER_NKEYS, tm), lambda i: (0, 0, i))
    return pl.pallas_call(
        _peer_route_body,
        grid=(m // tm,),
        in_specs=[
            pl.BlockSpec((tm, d), lambda i: (i, 0)),
            pl.BlockSpec((1, d), lambda i: (0, 0)),
            pl.BlockSpec((nq, d), lambda i: (0, 0)),
            pl.BlockSpec(keys.shape, lambda i: (0, 0, 0)),
        ],
        out_specs=[pl.BlockSpec((d, tm), lambda i: (0, i)), tab_spec, tab_spec, tab_spec, tab_spec],
        out_shape=[jax.ShapeDtypeStruct((d, m), BF16), tab, tab, tab16, tab16],
        compiler_params=_cparams("parallel"),
        name="peer_route",
    )(x, gain.reshape(1, d), wqT, keys)


def _peer_expert_body(x_ref, hnT_ref, n_ref, e0_ref, r1_ref, e1_ref, u_ref, vT_ref, o_ref,
                      a_s, acc_ref, *, ni, nb):
    j = pl.program_id(1)

    @pl.when(j == 0)
    def _():
        a_s[...] = jnp.zeros_like(a_s)
        acc_ref[...] = jnp.zeros_like(acc_ref)

    tm = a_s.shape[2]
    pk = BF16_SUBLANES
    nk = PEER_NKEYS
    cur = j % 2
    prev = 1 - cur

    ib = jnp.minimum(j, nb - 1)
    zc = ni * nk // PEER_Z_CHUNKS
    z_chunks = [jnp.dot(u_ref[pl.ds(c * zc, zc), :], hnT_ref[...], preferred_element_type=F32)
                for c in range(PEER_Z_CHUNKS)]
    acc_ref[...] += jnp.dot(vT_ref[0], a_s[prev], preferred_element_type=F32)
    for ii in range(ni):
        i = ib * ni + ii
        r0 = ii * nk - (ii * nk // zc) * zc
        z = z_chunks[ii * nk // zc][r0:r0 + nk, :].astype(BF16)
        act = z * (1.0 + lax.erf(z * (1.0 / math.sqrt(2.0))))
        gate = None
        for h in range(PEER_HEADS):
            n_b = jnp.broadcast_to(n_ref[h, pl.ds(i, 1), :], (pk, tm)).astype(BF16)[None]
            e0_b = jnp.broadcast_to(e0_ref[h, pl.ds(i, 1), :], (pk, tm)).astype(BF16)[None]
            r1 = r1_ref[h].reshape(nk // pk, pk, tm)
            e1 = e1_ref[h].reshape(nk // pk, pk, tm)
            term = jnp.where(r1 <= n_b, e1 * e0_b, jnp.zeros((), BF16))
            gate = term if gate is None else gate + term
        a_s[cur, pl.ds(ii * nk, nk), :] = (act.reshape(nk // pk, pk, tm) * gate).reshape(nk, tm)

    @pl.when(j == nb)
    def _():
        o_ref[...] = x_ref[...] + acc_ref[...].T


def peer_experts(x, hnT, n_tab, e0_tab, r1_tab, e1_tab, u, v_tab, tm=512, ni=PEER_EXPERT_ROWS):
    m, d = x.shape
    ne = u.shape[0]
    et = ni * PEER_NKEYS
    nb = ne // et
    vT = jnp.transpose(v_tab.astype(BF16).reshape(nb, et, d), (0, 2, 1))
    tab_spec = pl.BlockSpec((PEER_HEADS, PEER_NKEYS, tm), lambda i, j: (0, 0, i))
    return pl.pallas_call(
        functools.partial(_peer_expert_body, ni=ni, nb=nb),
        grid=(m // tm, nb + 1),
        in_specs=[
            pl.BlockSpec((tm, d), lambda i, j: (i, 0)),
            pl.BlockSpec((d, tm), lambda i, j: (0, i)),
            tab_spec, tab_spec, tab_spec, tab_spec,
            pl.BlockSpec((et, d), lambda i, j: (jnp.minimum(j, nb - 1), 0)),
            pl.BlockSpec((1, d, et), lambda i, j: (jnp.clip(j - 1, 0, nb - 1), 0, 0)),
        ],
        out_specs=pl.BlockSpec((tm, d), lambda i, j: (i, 0)),
        out_shape=jax.ShapeDtypeStruct((m, d), F32),
        scratch_shapes=[pltpu.VMEM((2, et, tm), BF16), pltpu.VMEM((d, tm), F32)],
        compiler_params=_cparams("parallel", "arbitrary"),
        name="peer_experts",
    )(x, hnT, n_tab, e0_tab, r1_tab, e1_tab, u, vT)


def peer_layer(x, gain, w_q, keys, u_tab, v_tab):
    wqT = w_q.T.astype(BF16)
    keys2 = keys.reshape(PEER_HEADS * 2, PEER_NKEYS, PEER_HALF).astype(BF16)
    hnT, n_tab, e0_tab, r1_tab, e1_tab = peer_route(x, gain, wqT, keys2)
    return peer_experts(x, hnT, n_tab, e0_tab, r1_tab, e1_tab, u_tab.astype(BF16), v_tab)


def _rope_lanes(t, cos, sin_signed):
    half = SWA_HEADDIM // 2
    lane = lax.broadcasted_iota(jnp.int32, t.shape, 1)
    first = (lane % SWA_HEADDIM) < half
    partner = jnp.where(first, pltpu.roll(t, 128 - half, 1), pltpu.roll(t, half, 1))
    return t * cos + partner * sin_signed


def _swa_body(q_ref, k_ref, v_ref, cos_ref, sin_ref, sink_ref, o_ref, kp_ref, vp_ref):
    w = SWA_WINDOW
    t_len = q_ref.shape[1]
    nkv = SWA_KV_HEADS * SWA_HEADDIM
    rep = SWA_HEADS // SWA_KV_HEADS
    zeros = jnp.zeros((w, nkv), BF16)
    kp_ref[pl.ds(0, w), :] = zeros
    kp_ref[pl.ds(w + t_len, w), :] = zeros
    vp_ref[pl.ds(0, w), :] = zeros
    vp_ref[pl.ds(w + t_len, w), :] = zeros
    vp_ref[pl.ds(w, t_len), :] = v_ref[0]
    cos_all = cos_ref[...]
    sin_all = sin_ref[...]
    for c in range(nkv // 128):
        kc = k_ref[0, :, c * 128:(c + 1) * 128].astype(F32)
        kp_ref[pl.ds(w, t_len), c * 128:(c + 1) * 128] = _rope_lanes(kc, cos_all, sin_all).astype(BF16)

    row = lax.broadcasted_iota(jnp.int32, (w, 3 * w), 0)
    col = lax.broadcasted_iota(jnp.int32, (w, 3 * w), 1)
    band = (col >= row) & (col <= row + 2 * w)

    def block(n, carry):
        base = pl.multiple_of(n * w, w)
        cos_b = cos_ref[pl.ds(base, w), :]
        sin_b = sin_ref[pl.ds(base, w), :]
        kpos = base - w + col
        valid = band & (kpos >= 0) & (kpos < t_len)
        outs = []
        for g in range(SWA_KV_HEADS):
            kw = kp_ref[pl.ds(base, 3 * w), g * SWA_HEADDIM:(g + 1) * SWA_HEADDIM]
            vw = vp_ref[pl.ds(base, 3 * w), g * SWA_HEADDIM:(g + 1) * SWA_HEADDIM]
            scores = []
            for c in range(g * rep // 2, (g + 1) * rep // 2):
                qc = q_ref[0, pl.ds(base, w), c * 128:(c + 1) * 128].astype(F32)
                qc = (_rope_lanes(qc, cos_b, sin_b) * (SWA_HEADDIM ** -0.5)).astype(BF16)
                for hh in range(2):
                    qh = qc[:, hh * SWA_HEADDIM:(hh + 1) * SWA_HEADDIM]
                    scores.append(lax.dot_general(qh, kw, (((1,), (1,)), ((), ())), preferred_element_type=F32))
            probs = []
            for k, s in enumerate(scores):
                s = jnp.where(valid, s, NEG)
                sk = sink_ref[g * rep + k]
                m = jnp.maximum(jnp.max(s, axis=-1, keepdims=True), sk)
                p = jnp.exp(s - m)
                inv = 1.0 / (jnp.sum(p, axis=-1, keepdims=True) + jnp.exp(sk - m))
                probs.append((p.astype(BF16), inv))
            for p, inv in probs:
                outs.append(jnp.dot(p, vw, preferred_element_type=F32) * inv)
        o_ref[0, pl.ds(base, w), :] = jnp.concatenate(outs, axis=-1).astype(o_ref.dtype)
        return carry

    lax.fori_loop(0, t_len // w, block, 0)


def swa_attention(proj, sink, bsz, t_len):
    half = SWA_HEADDIM // 2
    inv_freq = ROPE_THETA ** (-jnp.arange(half, dtype=F32) / half)
    ang = jnp.arange(t_len, dtype=F32)[:, None] * inv_freq[None, :]
    cos, sin = jnp.cos(ang), jnp.sin(ang)
    cos_t = jnp.tile(jnp.concatenate([cos, cos], axis=-1), (1, 2))
    sin_t = jnp.tile(jnp.concatenate([-sin, sin], axis=-1), (1, 2))
    nq = SWA_HEADS * SWA_HEADDIM
    nkv = SWA_KV_HEADS * SWA_HEADDIM
    return pl.pallas_call(
        _swa_body,
        grid=(bsz,),
        in_specs=[
            pl.BlockSpec((1, t_len, nq), lambda b: (b, 0, 0)),
            pl.BlockSpec((1, t_len, nkv), lambda b: (b, 0, nq // nkv)),
            pl.BlockSpec((1, t_len, nkv), lambda b: (b, 0, nq // nkv + 1)),
            pl.BlockSpec((t_len, 128), lambda b: (0, 0)),
            pl.BlockSpec((t_len, 128), lambda b: (0, 0)),
            pl.BlockSpec(memory_space=pltpu.SMEM),
        ],
        out_specs=pl.BlockSpec((1, t_len, nq), lambda b: (b, 0, 0)),
        out_shape=jax.ShapeDtypeStruct((bsz, t_len, nq), BF16),
        scratch_shapes=[pltpu.VMEM((t_len + 2 * SWA_WINDOW, nkv), BF16),
                        pltpu.VMEM((t_len + 2 * SWA_WINDOW, nkv), BF16)],
        compiler_params=_cparams("parallel"),
        name="swa_attention",
    )(proj, proj, proj, cos_t, sin_t, sink.astype(F32))


def _na_bias_table(rpb):
    qc = np.arange(GRID_W)[:, None]
    kc = np.arange(GRID_W)[None, :]
    cstart = np.clip(qc - NA_COLS // 2, 0, GRID_W - NA_COLS)
    valid = (kc >= cstart) & (kc < cstart + NA_COLS)
    cidx = np.clip(kc - qc + NA_COLS - 1, 0, 2 * NA_COLS - 2)
    onehot = (np.arange(2 * NA_COLS - 1)[:, None, None] == cidx[None]).astype(np.float32)
    toep = jnp.einsum('hrc,cqk->hrqk', rpb.astype(F32), onehot, precision=lax.Precision.HIGHEST)
    toep = jnp.where(valid[None, None], toep, NEG)
    return jnp.concatenate([toep[:, :-1], toep[:, 1:]], axis=-1)


def _na_body(q_ref, k_ref, v_ref, bias_ref, o_ref):
    t_len = q_ref.shape[1]
    rows = t_len // GRID_W
    win = NA_ROWS * GRID_W

    def row_group(gi, carry):
        chains = []
        for rr in range(NA_ROW_UNROLL):
            r = gi * NA_ROW_UNROLL + rr
            rs = jnp.clip(r - NA_ROWS // 2, 0, rows - NA_ROWS)
            d0 = rs - r + NA_ROWS - 1
            qbase = pl.multiple_of(r * GRID_W, GRID_W)
            kbase = pl.multiple_of(rs * GRID_W, GRID_W)
            qr = q_ref[0, pl.ds(qbase, GRID_W), :]
            kw = k_ref[0, pl.ds(kbase, win), :]
            for hh in range(2):
                sl = slice(hh * NA_HEADDIM, (hh + 1) * NA_HEADDIM)
                s = lax.dot_general(qr[:, sl], kw[:, sl], (((1,), (1,)), ((), ())), preferred_element_type=F32)
                chains.append((hh, d0, kbase, s))
        probs = []
        for hh, d0, kbase, s in chains:
            bias = jnp.concatenate([bias_ref[hh, d0 + 2 * c] for c in range(NA_ROWS // 2)], axis=-1)
            s = s * (NA_HEADDIM ** -0.5) + bias
            m = jnp.max(s, axis=-1, keepdims=True)
            p = jnp.exp(s - m)
            probs.append((p.astype(BF16), 1.0 / jnp.sum(p, axis=-1, keepdims=True)))
        outs = []
        for (hh, d0, kbase, s), (p, inv) in zip(chains, probs):
            vw = v_ref[0, pl.ds(kbase, win), hh * NA_HEADDIM:(hh + 1) * NA_HEADDIM]
            outs.append(jnp.dot(p, vw, preferred_element_type=F32) * inv)
        for rr in range(NA_ROW_UNROLL):
            qbase = pl.multiple_of((gi * NA_ROW_UNROLL + rr) * GRID_W, GRID_W)
            o_ref[0, pl.ds(qbase, GRID_W), :] = jnp.concatenate(outs[2 * rr:2 * rr + 2], axis=-1).astype(o_ref.dtype)
        return carry

    lax.fori_loop(0, rows // NA_ROW_UNROLL, row_group, 0)


def na_attention(proj, rpb, bsz, t_len):
    bias = _na_bias_table(rpb)
    npair = NA_HEADS // 2
    return pl.pallas_call(
        _na_body,
        grid=(npair, bsz),
        in_specs=[
            pl.BlockSpec((1, t_len, 128), lambda hp, b: (b, 0, hp)),
            pl.BlockSpec((1, t_len, 128), lambda hp, b: (b, 0, npair + hp)),
            pl.BlockSpec((1, t_len, 128), lambda hp, b: (b, 0, 2 * npair + hp)),
            pl.BlockSpec((2, 2 * NA_ROWS - 2, GRID_W, 2 * GRID_W), lambda hp, b: (hp, 0, 0, 0)),
        ],
        out_specs=pl.BlockSpec((1, t_len, 128), lambda hp, b: (b, 0, hp)),
        out_shape=jax.ShapeDtypeStruct((bsz, t_len, NA_WIDTH), BF16),
        compiler_params=_cparams("parallel", "parallel"),
        name="na_attention",
    )(proj, proj, proj, bias)


def _conv_silu_body(x_ref, w_ref, b_ref, o_ref, *, n_l2):
    t_len, tc = x_ref.shape[1], x_ref.shape[2]
    rc = CONV_ROW_CHUNK
    halo = F32_SUBLANES
    normalise = pl.program_id(1) < n_l2

    def chunk(ci, carry):
        r0 = pl.multiple_of(ci * rc, rc)
        cur = x_ref[0, pl.ds(r0, rc), :].astype(F32)
        lo = pl.multiple_of(jnp.maximum(r0 - halo, 0), halo)
        hi = pl.multiple_of(jnp.minimum(r0 + rc, t_len - halo), halo)
        before = jnp.where(r0 > 0, x_ref[0, pl.ds(lo, halo), :].astype(F32), 0.0)
        after = jnp.where(r0 + rc < t_len, x_ref[0, pl.ds(hi, halo), :].astype(F32), 0.0)
        xx = jnp.concatenate([before, cur, after], axis=0)
        acc = cur * w_ref[CONV_K // 2:CONV_K // 2 + 1, :] + b_ref[...]
        for k in range(CONV_K):
            off = k - CONV_K // 2
            if off == 0:
                continue
            acc = acc + xx[halo + off:halo + off + rc, :] * w_ref[k:k + 1, :]
        y = acc * jax.nn.sigmoid(acc)
        if n_l2 > 0:
            parts = []
            for c in range(tc // 128):
                yc = y[:, c * 128:(c + 1) * 128]
                ss = jnp.sum(yc * yc, axis=-1, keepdims=True)
                parts.append(yc * jnp.where(normalise, lax.rsqrt(ss + NORM_EPS), 1.0))
            y = jnp.concatenate(parts, axis=-1)
        o_ref[0, pl.ds(r0, rc), :] = y.astype(o_ref.dtype)
        return carry

    lax.fori_loop(0, t_len // rc, chunk, 0)


def conv_silu(proj, w, bias, col0, n_ch, n_l2=0, tc=512):
    bsz, t_len, _ = proj.shape
    assert col0 % tc == 0 and n_ch % tc == 0
    c0 = col0 // tc
    return pl.pallas_call(
        functools.partial(_conv_silu_body, n_l2=n_l2),
        grid=(bsz, n_ch // tc),
        in_specs=[
            pl.BlockSpec((1, t_len, tc), lambda b, j: (b, 0, c0 + j)),
            pl.BlockSpec((CONV_K, tc), lambda b, j: (0, j)),
            pl.BlockSpec((1, tc), lambda b, j: (0, j)),
        ],
        out_specs=pl.BlockSpec((1, t_len, tc), lambda b, j: (b, 0, j)),
        out_shape=jax.ShapeDtypeStruct((bsz, t_len, n_ch), BF16),
        compiler_params=_cparams("parallel", "parallel"),
        name="conv_silu",
    )(proj, w.astype(F32), bias.astype(F32).reshape(1, n_ch))


def _softplus(x):
    return jnp.maximum(x, 0.0) + jnp.log1p(jnp.exp(-jnp.abs(x)))


def _bmm(a, b, precision=None):
    return lax.dot_general(a, b, (((2,), (1,)), ((0,), (0,))), precision=precision,
                           preferred_element_type=F32)


def _split_bf16(a):
    hi = a.astype(BF16)
    return hi, (a - hi.astype(F32)).astype(BF16)


def _split3_bf16(a):
    hi = a.astype(BF16)
    rest = a - hi.astype(F32)
    mid = rest.astype(BF16)
    return hi, mid, (rest - mid.astype(F32)).astype(BF16)


def _bmm_split(a, b):
    return _bmm(a[0], b[0]) + _bmm(a[0], b[1]) + _bmm(a[1], b[0])


def _bmm_tn(a, b):
    return lax.dot_general(a, b, (((1,), (1,)), ((0,), (0,))), preferred_element_type=F32)


def _bmm_nt(a, b):
    return lax.dot_general(a, b, (((2,), (2,)), ((0,), (0,))), preferred_element_type=F32)


def _gdn_body(q_ref, k_ref, v_ref, z_ref, gcol_ref, grow_ref, alog_ref, dtb_ref, nw_ref, o_ref,
              m_s, b_s, q_s, o0_s, el_s, o_s):
    c_len = GDN_CHUNK
    t_len = q_ref.shape[1]
    nc = t_len // c_len
    h = pl.program_id(1)
    ii = lax.broadcasted_iota(jnp.int32, (c_len, c_len), 0)
    jj = lax.broadcasted_iota(jnp.int32, (c_len, c_len), 1)
    eye = (ii == jj).astype(F32)
    cg = GDN_PHASE_A_CHUNKS

    def phase_a(gi, carry):
        c0 = pl.multiple_of(gi * cg, cg)
        rows = pl.ds(pl.multiple_of(gi * (cg * c_len), cg * c_len), cg * c_len)
        k3 = k_ref[0, rows, :].reshape(cg, c_len, GDN_DK)
        v3 = v_ref[0, rows, :].reshape(cg, c_len, GDN_DK).astype(F32)
        kf = k3.astype(F32)
        qs = q_ref[0, rows, :].reshape(cg, c_len, GDN_DK).astype(F32) * (GDN_DK ** -0.5)
        gcol = gcol_ref[0, 0, pl.ds(c0, cg)]
        grow = grow_ref[0, 0, pl.ds(c0, cg)]
        qk_raw = _bmm_nt(qs.astype(BF16), k3)
        kb, gc_c, g_last, decay, strict = [], [], [], [], []
        for d in range(2):
            incl = (ii >= jj) if d == 0 else (ii <= jj)
            strict.append((ii > jj) if d == 0 else (ii < jj))
            tri = incl.astype(F32)
            tri_t = ((ii <= jj) if d == 0 else (ii >= jj)).astype(F32)
            neg_a = -jnp.exp(jnp.full((1, 1, 1), alog_ref[d, h], F32))
            dtb = dtb_ref[d, h]
            g_c = neg_a * _softplus(gcol[:, :, d:d + 1] + dtb)
            g_r = neg_a * _softplus(grow[:, d:d + 1, :] + dtb)
            beta_c = jax.nn.sigmoid(gcol[:, :, 2 + d:3 + d])
            gc_c.append(jnp.sum(tri[None] * g_r, axis=2, keepdims=True))
            gc_r = jnp.sum(tri_t[None] * g_c, axis=1, keepdims=True)
            g_last.append(jnp.sum(g_r, axis=2, keepdims=True))
            decay.append(jnp.where(incl[None], jnp.exp(jnp.where(incl[None], gc_c[d] - gc_r, 0.0)), 0.0))
            kb.append((kf * beta_c, v3 * beta_c))
        kk = [_bmm_nt(kb[d][0].astype(BF16), k3) for d in range(2)]
        low = [jnp.where(strict[d][None], kk[d] * decay[d], 0.0) for d in range(2)]
        inv = [eye[None] - low[d] for d in range(2)]
        pw = [_split_bf16(low[d]) for d in range(2)]
        for _ in range(5):
            pw = [_split_bf16(_bmm_split(pw[d], pw[d])) for d in range(2)]
            inv = [inv[d] + _bmm_split(_split_bf16(inv[d]), pw[d]) for d in range(2)]
        rhs = [jnp.concatenate([kb[d][1], kb[d][0] * jnp.exp(gc_c[d])], axis=-1) for d in range(2)]
        sol = [_bmm_split(_split_bf16(inv[d]), _split_bf16(rhs[d])) for d in range(2)]
        for d in range(2):
            ub = sol[d][:, :, :GDN_DK].astype(BF16)
            w = sol[d][:, :, GDN_DK:].astype(BF16)
            qk = (qk_raw * decay[d]).astype(BF16)
            kg = (kf * jnp.exp(g_last[d] - gc_c[d])).astype(BF16)
            m_s[d, pl.ds(c0, cg)] = _bmm_tn(kg, w).astype(BF16)
            b_s[d, pl.ds(c0, cg)] = _bmm_tn(kg, ub)
            q_s[d, pl.ds(c0, cg)] = (qs * jnp.exp(gc_c[d]) - _bmm(qk, w)).astype(BF16)
            o0_s[d, pl.ds(c0, cg)] = _bmm(qk, ub)
            el_s[d, pl.ds(c0, cg)] = jnp.broadcast_to(jnp.exp(g_last[d]), (cg, 1, GDN_DK))
        return carry

    lax.fori_loop(0, nc // cg, phase_a, 0)

    def step(t, carry):
        cs = (t, nc - 1 - t)
        sb = [carry[d].astype(BF16) for d in range(2)]
        ms = [jnp.dot(m_s[d, cs[d]], sb[d], preferred_element_type=F32) for d in range(2)]
        os_ = [jnp.dot(q_s[d, cs[d]], sb[d], preferred_element_type=F32) for d in range(2)]
        for d in range(2):
            o_s[d, cs[d]] = os_[d] + o0_s[d, cs[d]]
        return tuple(carry[d] * el_s[d, cs[d]] - ms[d] + b_s[d, cs[d]] for d in range(2))

    s0 = jnp.zeros((GDN_DK, GDN_DK), F32)
    lax.fori_loop(0, nc, step, (s0, s0))

    o = (o_s[0] + o_s[1]).reshape(t_len, GDN_DK)
    ms = jnp.mean(o * o, axis=-1, keepdims=True)
    z = z_ref[0]
    y = o * lax.rsqrt(ms + NORM_EPS) * nw_ref[...] * (z * jax.nn.sigmoid(z))
    o_ref[0] = y.astype(o_ref.dtype)


def gdn_scan(qkv, proj, gates, a_log, dt_bias, norm_w, bsz, t_len):
    nc = t_len // GDN_CHUNK
    nh = GDN_HEADS
    g4 = gates[:, :4 * nh].reshape(bsz, nc, GDN_CHUNK, 4, nh)
    gcol = jnp.transpose(g4, (0, 4, 1, 2, 3))
    grow = jnp.transpose(g4, (0, 4, 1, 3, 2))
    dk = GDN_DK
    nq = GDN_QK // dk
    return pl.pallas_call(
        _gdn_body,
        grid=(bsz, nh),
        in_specs=[
            pl.BlockSpec((1, t_len, dk), lambda b, h: (b, 0, h)),
            pl.BlockSpec((1, t_len, dk), lambda b, h: (b, 0, nq + h)),
            pl.BlockSpec((1, t_len, dk), lambda b, h: (b, 0, 2 * nq + h)),
            pl.BlockSpec((1, t_len, dk), lambda b, h: (b, 0, 3 * nq + h)),
            pl.BlockSpec((1, 1, nc, GDN_CHUNK, 4), lambda b, h: (b, h, 0, 0, 0)),
            pl.BlockSpec((1, 1, nc, 4, GDN_CHUNK), lambda b, h: (b, h, 0, 0, 0)),
            pl.BlockSpec(memory_space=pltpu.SMEM),
            pl.BlockSpec(memory_space=pltpu.SMEM),
            pl.BlockSpec((1, dk), lambda b, h: (0, 0)),
        ],
        out_specs=pl.BlockSpec((1, t_len, dk), lambda b, h: (b, 0, h)),
        out_shape=jax.ShapeDtypeStruct((bsz, t_len, GDN_V), BF16),
        scratch_shapes=[
            pltpu.VMEM((2, nc, dk, dk), BF16),
            pltpu.VMEM((2, nc, dk, dk), F32),
            pltpu.VMEM((2, nc, GDN_CHUNK, dk), BF16),
            pltpu.VMEM((2, nc, GDN_CHUNK, dk), F32),
            pltpu.VMEM((2, nc, 1, dk), F32),
            pltpu.VMEM((2, nc, GDN_CHUNK, dk), F32),
        ],
        compiler_params=_cparams("parallel", "parallel"),
        name="gdn_scan",
    )(qkv, qkv, qkv, proj, gcol, grow, a_log.astype(F32), dt_bias.astype(F32),
      norm_w.astype(F32).reshape(1, dk))


def _pad_cols(w, n):
    return jnp.pad(w, ((0, 0), (0, n - w.shape[1])))


def gdn_layer(x, gain, w_in, conv_w, a_log, dt_bias, norm_w, w_out, bsz, t_len):
    n_main = GDN_CONV_CH + GDN_V
    proj = norm_matmul(x, gain, w_in[:, :n_main].astype(BF16), F32)
    gates = norm_matmul(x, gain, _pad_cols(w_in[:, n_main:], 128).astype(BF16), F32)
    proj3 = proj.reshape(bsz, t_len, n_main)
    qkv = conv_silu(proj3, conv_w, jnp.zeros((GDN_CONV_CH,), F32), 0, GDN_CONV_CH,
                    n_l2=2 * GDN_QK // 512)
    y = gdn_scan(qkv, proj3, gates, a_log, dt_bias, norm_w, bsz, t_len)
    return matmul_residual(x, y.reshape(bsz * t_len, GDN_V), w_out.astype(BF16))


def _ssd_body(xs_ref, b_ref, c_ref, z_ref, dcol_ref, drow_ref, alogc_ref, alogr_ref, dtbc_ref, dtbr_ref, dskip_ref,
              nw_ref, o_ref, xw_s, ea_s, el_s, y_s, st_s):
    assert SSD_CHUNK == SSD_HEADDIM
    c_len = SSD_CHUNK
    t_len = xs_ref.shape[1]
    nc = t_len // c_len
    nr = SSD_HEADS // SSD_GROUPS
    hp = SSD_HEADDIM
    g = pl.program_id(1)
    ii = lax.broadcasted_iota(jnp.int32, (c_len, c_len), 0)
    jj = lax.broadcasted_iota(jnp.int32, (c_len, c_len), 1)
    cg = SSD_PHASE_A_CHUNKS

    def phase_a(gi, carry):
        c0 = pl.multiple_of(gi * cg, cg)
        rows = pl.ds(pl.multiple_of(gi * (cg * c_len), cg * c_len), cg * c_len)
        x3 = xs_ref[0, rows, :].reshape(cg, c_len, nr * hp)
        b3 = b_ref[0, rows, :].reshape(cg, c_len, SSD_STATE)
        c3 = c_ref[0, rows, :].reshape(cg, c_len, SSD_STATE)
        dcol = dcol_ref[0, 0, pl.ds(c0, cg)]
        drow = drow_ref[0, 0, pl.ds(c0, cg)]
        cb = _bmm_nt(c3, b3)
        dt_c = _softplus(dcol + dtbc_ref[0])
        dt_r = _softplus(drow + dtbr_ref[0])
        da_c = dt_c * -jnp.exp(alogc_ref[0])
        da_r = dt_r * -jnp.exp(alogr_ref[0])
        tri_f = jnp.broadcast_to((ii >= jj).astype(BF16)[None], (cg, c_len, c_len))
        tri_b = jnp.broadcast_to((ii <= jj).astype(BF16)[None], (cg, c_len, c_len))
        fwd_c = lax.broadcasted_iota(jnp.int32, (1, 1, 2 * nr), 2) < nr
        fwd_r = lax.broadcasted_iota(jnp.int32, (1, 2 * nr, 1), 1) < nr
        da_c3, da_r3 = _split3_bf16(da_c), _split3_bf16(da_r)
        ac_c = jnp.where(fwd_c, sum(_bmm(tri_f, p) for p in da_c3), sum(_bmm(tri_b, p) for p in da_c3))
        ac_r = jnp.where(fwd_r, sum(_bmm(p, tri_b) for p in da_r3), sum(_bmm(p, tri_f) for p in da_r3))
        ydiag = None
        for d in range(2):
            incl = (ii >= jj) if d == 0 else (ii <= jj)
            sel = (lax.broadcasted_iota(jnp.int32, (2 * nr, nr * hp), 1) // hp + d * nr
                   == lax.broadcasted_iota(jnp.int32, (2 * nr, nr * hp), 0)).astype(BF16)

            def spread(cols, pieces):
                parts = pieces(cols.reshape(cg * c_len, 2 * nr))
                return sum(jnp.dot(p, sel, preferred_element_type=F32) for p in parts).reshape(cg, c_len, nr * hp)

            dt_w = spread(dt_c, _split_bf16)
            ac_w = spread(ac_c, _split3_bf16)
            last = ac_w[:, c_len - 1:c_len, :] if d == 0 else ac_w[:, 0:1, :]
            xr = x3.astype(F32) * dt_w
            xw_s[d, pl.ds(c0, cg)] = (xr * jnp.exp(last - ac_w)).astype(BF16)
            ea_s[d, pl.ds(c0, cg)] = jnp.exp(ac_w)
            el_s[d, pl.ds(c0, cg)] = jnp.exp(last)
            xr = xr.astype(BF16)
            yd = []
            for r in range(nr):
                col = d * nr + r
                diff = ac_w[:, :, r * hp:r * hp + c_len] - ac_r[:, col:col + 1, :]
                seg = jnp.where(incl[None], jnp.exp(jnp.where(incl[None], diff, 0.0)), 0.0)
                yd.append(_bmm((cb * seg).astype(BF16), xr[:, :, r * hp:(r + 1) * hp]))
            yd = jnp.concatenate(yd, axis=-1)
            ydiag = yd if ydiag is None else ydiag + yd
        y_s[pl.ds(c0, cg)] = ydiag
        return carry

    lax.fori_loop(0, nc // cg, phase_a, 0)

    st_s[...] = jnp.zeros_like(st_s)

    def step(t, carry):
        for d in range(2):
            c = t if d == 0 else nc - 1 - t
            base = pl.multiple_of(c * c_len, c_len)
            cc = c_ref[0, pl.ds(base, c_len), :]
            bb = b_ref[0, pl.ds(base, c_len), :]
            st = st_s[d]
            y_s[c] += jnp.dot(cc, st.astype(BF16), preferred_element_type=F32) * ea_s[d, c]
            st_s[d] = st * el_s[d, c] + lax.dot_general(bb, xw_s[d, c], (((0,), (0,)), ((), ())),
                                                        preferred_element_type=F32)
        return carry

    lax.fori_loop(0, nc, step, 0)

    xs = xs_ref[0].astype(F32)
    z = z_ref[0]
    y = (y_s[...].reshape(t_len, nr * hp) + dskip_ref[...] * xs) * (z * jax.nn.sigmoid(z))
    ms = jnp.mean(y * y, axis=-1, keepdims=True)
    o_ref[0] = (y * lax.rsqrt(ms + NORM_EPS) * nw_ref[...]).astype(o_ref.dtype)


def ssd_scan(xbc, proj, dts, a_log, dt_bias, d_skip, norm_w, bsz, t_len):
    nc = t_len // SSD_CHUNK
    ng = SSD_GROUPS
    nr = SSD_HEADS // ng
    gw = nr * SSD_HEADDIM
    d6 = dts[:, :2 * SSD_HEADS].reshape(bsz, nc, SSD_CHUNK, 2, ng, nr)
    dcol = jnp.transpose(d6, (0, 4, 1, 2, 3, 5)).reshape(bsz, ng, nc, SSD_CHUNK, 2 * nr)
    drow = jnp.transpose(d6, (0, 4, 1, 3, 5, 2)).reshape(bsz, ng, nc, 2 * nr, SSD_CHUNK)
    nb0 = SSD_INNER // SSD_STATE
    dtb_g = jnp.transpose(dt_bias.astype(F32).reshape(2, ng, nr), (1, 0, 2)).reshape(ng, 2 * nr)
    alog_g = jnp.transpose(a_log.astype(F32).reshape(2, ng, nr), (1, 0, 2)).reshape(ng, 2 * nr)
    return pl.pallas_call(
        _ssd_body,
        grid=(bsz, ng),
        in_specs=[
            pl.BlockSpec((1, t_len, gw), lambda b, g: (b, 0, g)),
            pl.BlockSpec((1, t_len, SSD_STATE), lambda b, g: (b, 0, nb0 + g)),
            pl.BlockSpec((1, t_len, SSD_STATE), lambda b, g: (b, 0, nb0 + ng + g)),
            pl.BlockSpec((1, t_len, gw), lambda b, g: (b, 0, g)),
            pl.BlockSpec((1, 1, nc, SSD_CHUNK, 2 * nr), lambda b, g: (b, g, 0, 0, 0)),
            pl.BlockSpec((1, 1, nc, 2 * nr, SSD_CHUNK), lambda b, g: (b, g, 0, 0, 0)),
            pl.BlockSpec((1, 1, 2 * nr), lambda b, g: (g, 0, 0)),
            pl.BlockSpec((1, 2 * nr, 1), lambda b, g: (g, 0, 0)),
            pl.BlockSpec((1, 1, 2 * nr), lambda b, g: (g, 0, 0)),
            pl.BlockSpec((1, 2 * nr, 1), lambda b, g: (g, 0, 0)),
            pl.BlockSpec((1, gw), lambda b, g: (0, g)),
            pl.BlockSpec((1, gw), lambda b, g: (0, g)),
        ],
        out_specs=pl.BlockSpec((1, t_len, gw), lambda b, g: (b, 0, g)),
        out_shape=jax.ShapeDtypeStruct((bsz, t_len, SSD_INNER), BF16),
        scratch_shapes=[
            pltpu.VMEM((2, nc, SSD_CHUNK, gw), BF16),
            pltpu.VMEM((2, nc, SSD_CHUNK, gw), F32),
            pltpu.VMEM((2, nc, 1, gw), F32),
            pltpu.VMEM((nc, SSD_CHUNK, gw), F32),
            pltpu.VMEM((2, SSD_STATE, gw), F32),
        ],
        compiler_params=_cparams("parallel", "parallel"),
        name="ssd_scan",
    )(xbc, xbc, xbc, proj, dcol, drow, alog_g.reshape(ng, 1, 2 * nr), alog_g.reshape(ng, 2 * nr, 1),
      dtb_g.reshape(ng, 1, 2 * nr), dtb_g.reshape(ng, 2 * nr, 1),
      jnp.repeat(d_skip.astype(F32), SSD_HEADDIM).reshape(1, SSD_INNER),
      norm_w.astype(F32).reshape(1, SSD_INNER))


def ssd_layer(x, gain, w_in, conv_w, conv_b, a_log, dt_bias, d_skip, norm_w, w_out, bsz, t_len):
    n_main = SSD_INNER + SSD_CONV_CH
    proj = norm_matmul(x, gain, w_in[:, :n_main].astype(BF16), F32)
    dts = norm_matmul(x, gain, _pad_cols(w_in[:, n_main:], 128).astype(BF16), F32)
    proj3 = proj.reshape(bsz, t_len, n_main)
    xbc = conv_silu(proj3, conv_w, conv_b, SSD_INNER, SSD_CONV_CH)
    y = ssd_scan(xbc, proj3, dts, a_log, dt_bias, d_skip, norm_w, bsz, t_len)
    return matmul_residual(x, y.reshape(bsz * t_len, SSD_INNER), w_out.astype(BF16))


def swa_layer(x, gain, w_in, sink, w_out, bsz, t_len):
    proj = norm_matmul(x, gain, w_in.astype(BF16), BF16)
    o = swa_attention(proj.reshape(bsz, t_len, -1), sink, bsz, t_len)
    return matmul_residual(x, o.reshape(bsz * t_len, -1), w_out.astype(BF16))


def na_layer(x, gain, w_in, rpb, w_out, bsz, t_len):
    proj = norm_matmul(x, gain, w_in.astype(BF16), BF16)
    o = na_attention(proj.reshape(bsz, t_len, -1), rpb, bsz, t_len)
    return matmul_residual(x, o.reshape(bsz * t_len, -1), w_out.astype(BF16))


def kernel(x, norm_mix, norm_ffn, norm_final, gdn_w_in, gdn_conv, gdn_a_log, gdn_dt_bias, gdn_norm, gdn_w_out, ssd_w_in, ssd_conv, ssd_conv_b, ssd_a_log, ssd_dt_bias, ssd_d, ssd_norm, ssd_w_out, swa_w_in, swa_sink, swa_w_out, na_w_in, na_rpb, na_w_out, peer_w_q, peer_keys, peer_u, peer_v):
    bsz, t_len, d = x.shape
    depth = norm_mix.shape[0]
    xf = x.reshape(bsz * t_len, d)
    for i in range(depth):
        mixer, j = i % 4, i // 4
        if mixer == 0:
            xf = gdn_layer(xf, norm_mix[i], gdn_w_in[j], gdn_conv[j], gdn_a_log[j], gdn_dt_bias[j],
                           gdn_norm[j], gdn_w_out[j], bsz, t_len)
        elif mixer == 1:
            xf = ssd_layer(xf, norm_mix[i], ssd_w_in[j], ssd_conv[j], ssd_conv_b[j], ssd_a_log[j],
                           ssd_dt_bias[j], ssd_d[j], ssd_norm[j], ssd_w_out[j], bsz, t_len)
        elif mixer == 2:
            xf = swa_layer(xf, norm_mix[i], swa_w_in[j], swa_sink[j], swa_w_out[j], bsz, t_len)
        else:
            xf = na_layer(xf, norm_mix[i], na_w_in[j], na_rpb[j], na_w_out[j], bsz, t_len)
        xf = peer_layer(xf, norm_ffn[i], peer_w_q[i], peer_keys[i], peer_u[i], peer_v[i])
    return final_norm(xf, norm_final).reshape(bsz, t_len, d)
```

```python
import functools
import math

import jax
import jax.numpy as jnp
import numpy as np
from jax import lax
from jax.experimental import pallas as pl
from jax.experimental.pallas import tpu as pltpu

F32 = jnp.float32
BF16 = jnp.bfloat16
NEG = -1e30

D_MODEL = 1024
SEQ = 2048
GRID_W = 64
CONV_K = 5
NORM_EPS = 1e-6

GDN_HEADS = 8
GDN_DK = 128
GDN_CHUNK = 64
GDN_QK = 1024
GDN_V = 1024
GDN_CONV_CH = 3072

SSD_INNER = 2048
SSD_HEADDIM = 64
SSD_HEADS = 32
SSD_GROUPS = 4
SSD_STATE = 128
SSD_CHUNK = 64
SSD_BC = 512
SSD_CONV_CH = 3072

SWA_HEADS = 16
SWA_KV_HEADS = 4
SWA_HEADDIM = 64
SWA_WINDOW = 128
ROPE_THETA = 10000.0

NA_HEADS = 16
NA_HEADDIM = 64
NA_ROWS = 8
NA_COLS = 16
NA_WIDTH = 1024

PEER_HEADS = 8
PEER_NKEYS = 128
PEER_QDIM = 256
PEER_HALF = 128
PEER_TOPK = 16

GDN_PHASE_A_CHUNKS = 8
SSD_PHASE_A_CHUNKS = 8

PEER_EXPERT_ROWS = 8
PEER_Z_CHUNKS = 4

SWA_GROUPS_PER_STAGE = 2

NA_ROW_UNROLL = 4

BF16_SUBLANES = 16
F32_SUBLANES = 8

CONV_ROW_CHUNK = 256

VMEM_LIMIT_BYTES = 52 * 1024 * 1024


def _cparams(*sem, flags=None):
    return pltpu.CompilerParams(dimension_semantics=sem, vmem_limit_bytes=VMEM_LIMIT_BYTES, flags=flags)


def _norm_matmul_body(x_ref, g_ref, w_ref, o_ref, hn_ref):
    @pl.when(pl.program_id(1) == 0)
    def _():
        x = x_ref[...]
        ms = jnp.mean(x * x, axis=-1, keepdims=True)
        hn_ref[...] = (x * lax.rsqrt(ms + NORM_EPS) * g_ref[...]).astype(BF16)

    o_ref[...] = jnp.dot(hn_ref[...], w_ref[...], preferred_element_type=F32).astype(o_ref.dtype)


def norm_matmul(x, gain, w, out_dtype, tm=1024, tn=1024):
    m, d = x.shape
    n = w.shape[1]
    tn = min(tn, n)
    while n % tn:
        tn //= 2
    assert m % tm == 0 and tn % 128 == 0
    return pl.pallas_call(
        _norm_matmul_body,
        grid=(m // tm, n // tn),
        in_specs=[
            pl.BlockSpec((tm, d), lambda i, j: (i, 0)),
            pl.BlockSpec((1, d), lambda i, j: (0, 0)),
            pl.BlockSpec((d, tn), lambda i, j: (0, j)),
        ],
        out_specs=pl.BlockSpec((tm, tn), lambda i, j: (i, j)),
        out_shape=jax.ShapeDtypeStruct((m, n), out_dtype),
        scratch_shapes=[pltpu.VMEM((tm, d), BF16)],
        compiler_params=_cparams("parallel", "arbitrary"),
        name="norm_matmul",
    )(x, gain.reshape(1, d), w)


def _matmul_residual_body(x_ref, y_ref, w_ref, o_ref):
    o_ref[...] = x_ref[...] + jnp.dot(y_ref[...], w_ref[...], preferred_element_type=F32)


def matmul_residual(x, y, w, tm=1024):
    m, d = x.shape
    k = y.shape[1]
    return pl.pallas_call(
        _matmul_residual_body,
        grid=(m // tm,),
        in_specs=[
            pl.BlockSpec((tm, d), lambda i: (i, 0)),
            pl.BlockSpec((tm, k), lambda i: (i, 0)),
            pl.BlockSpec((k, d), lambda i: (0, 0)),
        ],
        out_specs=pl.BlockSpec((tm, d), lambda i: (i, 0)),
        out_shape=jax.ShapeDtypeStruct((m, d), F32),
        compiler_params=_cparams("parallel"),
        name="matmul_residual",
    )(x, y, w)


def _final_norm_body(x_ref, g_ref, o_ref):
    x = x_ref[...]
    ms = jnp.mean(x * x, axis=-1, keepdims=True)
    o_ref[...] = x * lax.rsqrt(ms + NORM_EPS) * g_ref[...]


def final_norm(x, gain, tm=1024):
    m, d = x.shape
    return pl.pallas_call(
        _final_norm_body,
        grid=(m // tm,),
        in_specs=[pl.BlockSpec((tm, d), lambda i: (i, 0)), pl.BlockSpec((1, d), lambda i: (0, 0))],
        out_specs=pl.BlockSpec((tm, d), lambda i: (i, 0)),
        out_shape=jax.ShapeDtypeStruct((m, d), F32),
        compiler_params=_cparams("parallel"),
        name="final_norm",
    )(x, gain.reshape(1, d))


def _top16_desc(s, with_rank):
    work = s
    rank = jnp.full(s.shape, 99.0, F32) if with_rank else None
    vals = []
    for r in range(PEER_TOPK):
        m = jnp.max(work, axis=0, keepdims=True)
        eq = work == m
        if with_rank:
            rank = jnp.where(eq, float(r + 1), rank)
        work = jnp.where(eq, -jnp.inf, work)
        vals.append(m)
    return jnp.concatenate(vals, axis=0), rank


def _peer_route_body(x_ref, g_ref, wqT_ref, keys_ref, hnT_ref, n_ref, e0_ref, r1_ref, e1_ref):
    x = x_ref[...]
    ms = jnp.mean(x * x, axis=-1, keepdims=True)
    hn = x * lax.rsqrt(ms + NORM_EPS) * g_ref[...]
    hnT = hn.T.astype(BF16)
    hnT_ref[...] = hnT
    qT = jnp.dot(wqT_ref[...], hnT, preferred_element_type=F32)
    row8 = lax.broadcasted_iota(jnp.int32, (8, x.shape[0]), 0)
    for h in range(PEER_HEADS):
        s = []
        for p in range(2):
            hp = 2 * h + p
            q_hp = qT[hp * PEER_HALF:(hp + 1) * PEER_HALF, :].astype(BF16)
            s.append(jnp.dot(keys_ref[hp], q_hp, preferred_element_type=F32))
        a0, _ = _top16_desc(s[0], False)
        b, rank1 = _top16_desc(s[1], True)
        cand = [a0[0:1, :] + b]
        for p in range(1, PEER_TOPK):
            cnt = PEER_TOPK // (p + 1)
            c = a0[p:p + 1, :] + b[0:8, :]
            cand.append(c if cnt >= 8 else jnp.where(row8 < cnt, c, -jnp.inf))
        best, _ = _top16_desc(jnp.concatenate(cand, axis=0), False)
        tau = best[PEER_TOPK - 1:PEER_TOPK, :]
        z = jnp.sum(jnp.exp(best - best[0:1, :]), axis=0, keepdims=True)
        n_map = jnp.zeros_like(s[0])
        for p in range(PEER_TOPK):
            n_p = jnp.sum(jnp.where(cand[p] >= tau, 1.0, 0.0), axis=0, keepdims=True)
            n_map = jnp.where(s[0] == a0[p:p + 1, :], n_p, n_map)
        n_ref[h] = n_map
        e0_ref[h] = jnp.exp(s[0] - a0[0:1, :]) * (0.5 / z)
        r1_ref[h] = rank1.astype(BF16)
        e1_ref[h] = jnp.exp(s[1] - b[0:1, :]).astype(BF16)


def peer_route(x, gain, wqT, keys, tm=256):
    m, d = x.shape
    nq = wqT.shape[0]
    tab = jax.ShapeDtypeStruct((PEER_HEADS, PEER_NKEYS, m), F32)
    tab16 = jax.ShapeDtypeStruct((PEER_HEADS, PEER_NKEYS, m), BF16)
    tab_spec = pl.BlockSpec((PEER_HEADS, PEER_NKEYS, tm), lambda i: (0, 0, i))
    return pl.pallas_call(
        _peer_route_body,
        grid=(m // tm,),
        in_specs=[
            pl.BlockSpec((tm, d), lambda i: (i, 0)),
            pl.BlockSpec((1, d), lambda i: (0, 0)),
            pl.BlockSpec((nq, d), lambda i: (0, 0)),
            pl.BlockSpec(keys.shape, lambda i: (0, 0, 0)),
        ],
        out_specs=[pl.BlockSpec((d, tm), lambda i: (0, i)), tab_spec, tab_spec, tab_spec, tab_spec],
        out_shape=[jax.ShapeDtypeStruct((d, m), BF16), tab, tab, tab16, tab16],
        compiler_params=_cparams("parallel"),
        name="peer_route",
    )(x, gain.reshape(1, d), wqT, keys)


def _peer_expert_body(x_ref, hnT_ref, n_ref, e0_ref, r1_ref, e1_ref, u_ref, vT_ref, o_ref,
                      a_s, acc_ref, *, ni, nb):
    j = pl.program_id(1)

    @pl.when(j == 0)
    def _():
        a_s[...] = jnp.zeros_like(a_s)
        acc_ref[...] = jnp.zeros_like(acc_ref)

    tm = a_s.shape[2]
    pk = BF16_SUBLANES
    nk = PEER_NKEYS
    cur = j % 2
    prev = 1 - cur

    ib = jnp.minimum(j, nb - 1)
    zc = ni * nk // PEER_Z_CHUNKS
    z_chunks = [jnp.dot(u_ref[pl.ds(c * zc, zc), :], hnT_ref[...], preferred_element_type=F32)
                for c in range(PEER_Z_CHUNKS)]
    acc_ref[...] += jnp.dot(vT_ref[0], a_s[prev], preferred_element_type=F32)
    for ii in range(ni):
        i = ib * ni + ii
        r0 = ii * nk - (ii * nk // zc) * zc
        z = z_chunks[ii * nk // zc][r0:r0 + nk, :].astype(BF16)
        act = z * (1.0 + lax.erf(z * (1.0 / math.sqrt(2.0))))
        gate = None
        for h in range(PEER_HEADS):
            n_b = jnp.broadcast_to(n_ref[h, pl.ds(i, 1), :], (pk, tm)).astype(BF16)[None]
            e0_b = jnp.broadcast_to(e0_ref[h, pl.ds(i, 1), :], (pk, tm)).astype(BF16)[None]
            r1 = r1_ref[h].reshape(nk // pk, pk, tm)
            e1 = e1_ref[h].reshape(nk // pk, pk, tm)
            term = jnp.where(r1 <= n_b, e1 * e0_b, jnp.zeros((), BF16))
            gate = term if gate is None else gate + term
        a_s[cur, pl.ds(ii * nk, nk), :] = (act.reshape(nk // pk, pk, tm) * gate).reshape(nk, tm)

    @pl.when(j == nb)
    def _():
        o_ref[...] = x_ref[...] + acc_ref[...].T


def peer_experts(x, hnT, n_tab, e0_tab, r1_tab, e1_tab, u, v_tab, tm=512, ni=PEER_EXPERT_ROWS):
    m, d = x.shape
    ne = u.shape[0]
    et = ni * PEER_NKEYS
    nb = ne // et
    vT = jnp.transpose(v_tab.astype(BF16).reshape(nb, et, d), (0, 2, 1))
    tab_spec = pl.BlockSpec((PEER_HEADS, PEER_NKEYS, tm), lambda i, j: (0, 0, i))
    return pl.pallas_call(
        functools.partial(_peer_expert_body, ni=ni, nb=nb),
        grid=(m // tm, nb + 1),
        in_specs=[
            pl.BlockSpec((tm, d), lambda i, j: (i, 0)),
            pl.BlockSpec((d, tm), lambda i, j: (0, i)),
            tab_spec, tab_spec, tab_spec, tab_spec,
            pl.BlockSpec((et, d), lambda i, j: (jnp.minimum(j, nb - 1), 0)),
            pl.BlockSpec((1, d, et), lambda i, j: (jnp.clip(j - 1, 0, nb - 1), 0, 0)),
        ],
        out_specs=pl.BlockSpec((tm, d), lambda i, j: (i, 0)),
        out_shape=jax.ShapeDtypeStruct((m, d), F32),
        scratch_shapes=[pltpu.VMEM((2, et, tm), BF16), pltpu.VMEM((d, tm), F32)],
        compiler_params=_cparams("parallel", "arbitrary"),
        name="peer_experts",
    )(x, hnT, n_tab, e0_tab, r1_tab, e1_tab, u, vT)


def peer_layer(x, gain, w_q, keys, u_tab, v_tab):
    wqT = w_q.T.astype(BF16)
    keys2 = keys.reshape(PEER_HEADS * 2, PEER_NKEYS, PEER_HALF).astype(BF16)
    hnT, n_tab, e0_tab, r1_tab, e1_tab = peer_route(x, gain, wqT, keys2)
    return peer_experts(x, hnT, n_tab, e0_tab, r1_tab, e1_tab, u_tab.astype(BF16), v_tab)


def _rope_lanes(t, cos, sin_signed):
    half = SWA_HEADDIM // 2
    lane = lax.broadcasted_iota(jnp.int32, t.shape, 1)
    first = (lane % SWA_HEADDIM) < half
    partner = jnp.where(first, pltpu.roll(t, 128 - half, 1), pltpu.roll(t, half, 1))
    return t * cos + partner * sin_signed


def _swa_body(q_ref, k_ref, v_ref, cos_ref, sin_ref, sink_ref, o_ref, kp_ref, vp_ref):
    w = SWA_WINDOW
    t_len = q_ref.shape[1]
    nkv = SWA_KV_HEADS * SWA_HEADDIM
    rep = SWA_HEADS // SWA_KV_HEADS
    zeros = jnp.zeros((w, nkv), BF16)
    kp_ref[pl.ds(0, w), :] = zeros
    kp_ref[pl.ds(w + t_len, w), :] = zeros
    vp_ref[pl.ds(0, w), :] = zeros
    vp_ref[pl.ds(w + t_len, w), :] = zeros
    vp_ref[pl.ds(w, t_len), :] = v_ref[0]
    cos_all = cos_ref[...]
    sin_all = sin_ref[...]
    for c in range(nkv // 128):
        kc = k_ref[0, :, c * 128:(c + 1) * 128].astype(F32)
        kp_ref[pl.ds(w, t_len), c * 128:(c + 1) * 128] = _rope_lanes(kc, cos_all, sin_all).astype(BF16)

    row = lax.broadcasted_iota(jnp.int32, (w, 3 * w), 0)
    col = lax.broadcasted_iota(jnp.int32, (w, 3 * w), 1)
    band = (col >= row) & (col <= row + 2 * w)

    def block(n, carry):
        base = pl.multiple_of(n * w, w)
        cos_b = cos_ref[pl.ds(base, w), :]
        sin_b = sin_ref[pl.ds(base, w), :]
        kpos = base - w + col
        valid = band & (kpos >= 0) & (kpos < t_len)
        outs = []
        for g0 in range(0, SWA_KV_HEADS, SWA_GROUPS_PER_STAGE):
            groups = range(g0, g0 + SWA_GROUPS_PER_STAGE)
            scores = []
            for g in groups:
                kw = kp_ref[pl.ds(base, 3 * w), g * SWA_HEADDIM:(g + 1) * SWA_HEADDIM]
                for c in range(g * rep // 2, (g + 1) * rep // 2):
                    qc = q_ref[0, pl.ds(base, w), c * 128:(c + 1) * 128].astype(F32)
                    qc = (_rope_lanes(qc, cos_b, sin_b) * (SWA_HEADDIM ** -0.5)).astype(BF16)
                    for hh in range(2):
                        qh = qc[:, hh * SWA_HEADDIM:(hh + 1) * SWA_HEADDIM]
                        scores.append(lax.dot_general(qh, kw, (((1,), (1,)), ((), ())),
                                                      preferred_element_type=F32))
            probs = []
            for k, s in enumerate(scores):
                s = jnp.where(valid, s, NEG)
                sk = sink_ref[g0 * rep + k]
                m = jnp.maximum(jnp.max(s, axis=-1, keepdims=True), sk)
                p = jnp.exp(s - m)
                inv = 1.0 / (jnp.sum(p, axis=-1, keepdims=True) + jnp.exp(sk - m))
                probs.append((p.astype(BF16), inv))
            for k, (p, inv) in enumerate(probs):
                g = g0 + k // rep
                vw = vp_ref[pl.ds(base, 3 * w), g * SWA_HEADDIM:(g + 1) * SWA_HEADDIM]
                outs.append(jnp.dot(p, vw, preferred_element_type=F32) * inv)
        o_ref[0, pl.ds(base, w), :] = jnp.concatenate(outs, axis=-1).astype(o_ref.dtype)
        return carry

    lax.fori_loop(0, t_len // w, block, 0)


def swa_attention(proj, sink, bsz, t_len):
    half = SWA_HEADDIM // 2
    inv_freq = ROPE_THETA ** (-jnp.arange(half, dtype=F32) / half)
    ang = jnp.arange(t_len, dtype=F32)[:, None] * inv_freq[None, :]
    cos, sin = jnp.cos(ang), jnp.sin(ang)
    cos_t = jnp.tile(jnp.concatenate([cos, cos], axis=-1), (1, 2))
    sin_t = jnp.tile(jnp.concatenate([-sin, sin], axis=-1), (1, 2))
    nq = SWA_HEADS * SWA_HEADDIM
    nkv = SWA_KV_HEADS * SWA_HEADDIM
    return pl.pallas_call(
        _swa_body,
        grid=(bsz,),
        in_specs=[
            pl.BlockSpec((1, t_len, nq), lambda b: (b, 0, 0)),
            pl.BlockSpec((1, t_len, nkv), lambda b: (b, 0, nq // nkv)),
            pl.BlockSpec((1, t_len, nkv), lambda b: (b, 0, nq // nkv + 1)),
            pl.BlockSpec((t_len, 128), lambda b: (0, 0)),
            pl.BlockSpec((t_len, 128), lambda b: (0, 0)),
            pl.BlockSpec(memory_space=pltpu.SMEM),
        ],
        out_specs=pl.BlockSpec((1, t_len, nq), lambda b: (b, 0, 0)),
        out_shape=jax.ShapeDtypeStruct((bsz, t_len, nq), BF16),
        scratch_shapes=[pltpu.VMEM((t_len + 2 * SWA_WINDOW, nkv), BF16),
                        pltpu.VMEM((t_len + 2 * SWA_WINDOW, nkv), BF16)],
        compiler_params=_cparams("parallel"),
        name="swa_attention",
    )(proj, proj, proj, cos_t, sin_t, sink.astype(F32))


def _na_bias_table(rpb):
    qc = np.arange(GRID_W)[:, None]
    kc = np.arange(GRID_W)[None, :]
    cstart = np.clip(qc - NA_COLS // 2, 0, GRID_W - NA_COLS)
    valid = (kc >= cstart) & (kc < cstart + NA_COLS)
    cidx = np.clip(kc - qc + NA_COLS - 1, 0, 2 * NA_COLS - 2)
    onehot = (np.arange(2 * NA_COLS - 1)[:, None, None] == cidx[None]).astype(np.float32)
    toep = jnp.einsum('hrc,cqk->hrqk', rpb.astype(F32), onehot, precision=lax.Precision.HIGHEST)
    toep = jnp.where(valid[None, None], toep, NEG)
    return jnp.concatenate([toep[:, :-1], toep[:, 1:]], axis=-1)


def _na_body(q_ref, k_ref, v_ref, bias_ref, o_ref):
    t_len = q_ref.shape[1]
    rows = t_len // GRID_W
    win = NA_ROWS * GRID_W

    def row_group(gi, carry):
        chains = []
        for rr in range(NA_ROW_UNROLL):
            r = gi * NA_ROW_UNROLL + rr
            rs = jnp.clip(r - NA_ROWS // 2, 0, rows - NA_ROWS)
            d0 = rs - r + NA_ROWS - 1
            qbase = pl.multiple_of(r * GRID_W, GRID_W)
            kbase = pl.multiple_of(rs * GRID_W, GRID_W)
            qr = q_ref[0, pl.ds(qbase, GRID_W), :]
            kw = k_ref[0, pl.ds(kbase, win), :]
            for hh in range(2):
                sl = slice(hh * NA_HEADDIM, (hh + 1) * NA_HEADDIM)
                s = lax.dot_general(qr[:, sl], kw[:, sl], (((1,), (1,)), ((), ())), preferred_element_type=F32)
                chains.append((hh, d0, kbase, s))
        probs = []
        for hh, d0, kbase, s in chains:
            bias = jnp.concatenate([bias_ref[hh, d0 + 2 * c] for c in range(NA_ROWS // 2)], axis=-1)
            s = s * (NA_HEADDIM ** -0.5) + bias
            m = jnp.max(s, axis=-1, keepdims=True)
            p = jnp.exp(s - m)
            probs.append((p.astype(BF16), 1.0 / jnp.sum(p, axis=-1, keepdims=True)))
        outs = []
        for (hh, d0, kbase, s), (p, inv) in zip(chains, probs):
            vw = v_ref[0, pl.ds(kbase, win), hh * NA_HEADDIM:(hh + 1) * NA_HEADDIM]
            outs.append(jnp.dot(p, vw, preferred_element_type=F32) * inv)
        for rr in range(NA_ROW_UNROLL):
            qbase = pl.multiple_of((gi * NA_ROW_UNROLL + rr) * GRID_W, GRID_W)
            o_ref[0, pl.ds(qbase, GRID_W), :] = jnp.concatenate(outs[2 * rr:2 * rr + 2], axis=-1).astype(o_ref.dtype)
        return carry

    lax.fori_loop(0, rows // NA_ROW_UNROLL, row_group, 0)


def na_attention(proj, rpb, bsz, t_len):
    bias = _na_bias_table(rpb)
    npair = NA_HEADS // 2
    return pl.pallas_call(
        _na_body,
        grid=(npair, bsz),
        in_specs=[
            pl.BlockSpec((1, t_len, 128), lambda hp, b: (b, 0, hp)),
            pl.BlockSpec((1, t_len, 128), lambda hp, b: (b, 0, npair + hp)),
            pl.BlockSpec((1, t_len, 128), lambda hp, b: (b, 0, 2 * npair + hp)),
            pl.BlockSpec((2, 2 * NA_ROWS - 2, GRID_W, 2 * GRID_W), lambda hp, b: (hp, 0, 0, 0)),
        ],
        out_specs=pl.BlockSpec((1, t_len, 128), lambda hp, b: (b, 0, hp)),
        out_shape=jax.ShapeDtypeStruct((bsz, t_len, NA_WIDTH), BF16),
        compiler_params=_cparams("parallel", "parallel"),
        name="na_attention",
    )(proj, proj, proj, bias)


def _conv_silu_body(x_ref, w_ref, b_ref, o_ref, *, n_l2):
    t_len, tc = x_ref.shape[1], x_ref.shape[2]
    rc = CONV_ROW_CHUNK
    halo = F32_SUBLANES
    normalise = pl.program_id(1) < n_l2

    def chunk(ci, carry):
        r0 = pl.multiple_of(ci * rc, rc)
        cur = x_ref[0, pl.ds(r0, rc), :].astype(F32)
        lo = pl.multiple_of(jnp.maximum(r0 - halo, 0), halo)
        hi = pl.multiple_of(jnp.minimum(r0 + rc, t_len - halo), halo)
        before = jnp.where(r0 > 0, x_ref[0, pl.ds(lo, halo), :].astype(F32), 0.0)
        after = jnp.where(r0 + rc < t_len, x_ref[0, pl.ds(hi, halo), :].astype(F32), 0.0)
        xx = jnp.concatenate([before, cur, after], axis=0)
        acc = cur * w_ref[CONV_K // 2:CONV_K // 2 + 1, :] + b_ref[...]
        for k in range(CONV_K):
            off = k - CONV_K // 2
            if off == 0:
                continue
            acc = acc + xx[halo + off:halo + off + rc, :] * w_ref[k:k + 1, :]
        y = acc * jax.nn.sigmoid(acc)
        if n_l2 > 0:
            parts = []
            for c in range(tc // 128):
                yc = y[:, c * 128:(c + 1) * 128]
                ss = jnp.sum(yc * yc, axis=-1, keepdims=True)
                parts.append(yc * jnp.where(normalise, lax.rsqrt(ss + NORM_EPS), 1.0))
            y = jnp.concatenate(parts, axis=-1)
        o_ref[0, pl.ds(r0, rc), :] = y.astype(o_ref.dtype)
        return carry

    lax.fori_loop(0, t_len // rc, chunk, 0)


def conv_silu(proj, w, bias, col0, n_ch, n_l2=0, tc=512):
    bsz, t_len, _ = proj.shape
    assert col0 % tc == 0 and n_ch % tc == 0
    c0 = col0 // tc
    return pl.pallas_call(
        functools.partial(_conv_silu_body, n_l2=n_l2),
        grid=(bsz, n_ch // tc),
        in_specs=[
            pl.BlockSpec((1, t_len, tc), lambda b, j: (b, 0, c0 + j)),
            pl.BlockSpec((CONV_K, tc), lambda b, j: (0, j)),
            pl.BlockSpec((1, tc), lambda b, j: (0, j)),
        ],
        out_specs=pl.BlockSpec((1, t_len, tc), lambda b, j: (b, 0, j)),
        out_shape=jax.ShapeDtypeStruct((bsz, t_len, n_ch), BF16),
        compiler_params=_cparams("parallel", "parallel"),
        name="conv_silu",
    )(proj, w.astype(F32), bias.astype(F32).reshape(1, n_ch))


def _softplus(x):
    return jnp.maximum(x, 0.0) + jnp.log1p(jnp.exp(-jnp.abs(x)))


def _bmm(a, b, precision=None):
    return lax.dot_general(a, b, (((2,), (1,)), ((0,), (0,))), precision=precision,
                           preferred_element_type=F32)


def _split_bf16(a):
    hi = a.astype(BF16)
    return hi, (a - hi.astype(F32)).astype(BF16)


def _split3_bf16(a):
    hi = a.astype(BF16)
    rest = a - hi.astype(F32)
    mid = rest.astype(BF16)
    return hi, mid, (rest - mid.astype(F32)).astype(BF16)


def _bmm_split(a, b):
    return _bmm(a[0], b[0]) + _bmm(a[0], b[1]) + _bmm(a[1], b[0])


def _bmm_tn(a, b):
    return lax.dot_general(a, b, (((1,), (1,)), ((0,), (0,))), preferred_element_type=F32)


def _bmm_nt(a, b):
    return lax.dot_general(a, b, (((2,), (2,)), ((0,), (0,))), preferred_element_type=F32)


def _gdn_body(q_ref, k_ref, v_ref, z_ref, gcol_ref, grow_ref, alog_ref, dtb_ref, nw_ref, o_ref,
              m_s, b_s, q_s, o0_s, el_s, o_s):
    c_len = GDN_CHUNK
    t_len = q_ref.shape[1]
    nc = t_len // c_len
    h = pl.program_id(1)
    ii = lax.broadcasted_iota(jnp.int32, (c_len, c_len), 0)
    jj = lax.broadcasted_iota(jnp.int32, (c_len, c_len), 1)
    eye = (ii == jj).astype(F32)
    cg = GDN_PHASE_A_CHUNKS

    def phase_a(gi, carry):
        c0 = pl.multiple_of(gi * cg, cg)
        rows = pl.ds(pl.multiple_of(gi * (cg * c_len), cg * c_len), cg * c_len)
        k3 = k_ref[0, rows, :].reshape(cg, c_len, GDN_DK)
        v3 = v_ref[0, rows, :].reshape(cg, c_len, GDN_DK).astype(F32)
        kf = k3.astype(F32)
        qs = q_ref[0, rows, :].reshape(cg, c_len, GDN_DK).astype(F32) * (GDN_DK ** -0.5)
        gcol = gcol_ref[0, 0, pl.ds(c0, cg)]
        grow = grow_ref[0, 0, pl.ds(c0, cg)]
        qk_raw = _bmm_nt(qs.astype(BF16), k3)
        kb, gc_c, g_last, decay, strict = [], [], [], [], []
        for d in range(2):
            incl = (ii >= jj) if d == 0 else (ii <= jj)
            strict.append((ii > jj) if d == 0 else (ii < jj))
            tri = incl.astype(F32)
            tri_t = ((ii <= jj) if d == 0 else (ii >= jj)).astype(F32)
            neg_a = -jnp.exp(jnp.full((1, 1, 1), alog_ref[d, h], F32))
            dtb = dtb_ref[d, h]
            g_c = neg_a * _softplus(gcol[:, :, d:d + 1] + dtb)
            g_r = neg_a * _softplus(grow[:, d:d + 1, :] + dtb)
            beta_c = jax.nn.sigmoid(gcol[:, :, 2 + d:3 + d])
            gc_c.append(jnp.sum(tri[None] * g_r, axis=2, keepdims=True))
            gc_r = jnp.sum(tri_t[None] * g_c, axis=1, keepdims=True)
            g_last.append(jnp.sum(g_r, axis=2, keepdims=True))
            decay.append(jnp.where(incl[None], jnp.exp(jnp.where(incl[None], gc_c[d] - gc_r, 0.0)), 0.0))
            kb.append((kf * beta_c, v3 * beta_c))
        kk = [_bmm_nt(kb[d][0].astype(BF16), k3) for d in range(2)]
        low = [jnp.where(strict[d][None], kk[d] * decay[d], 0.0) for d in range(2)]
        inv = [eye[None] - low[d] for d in range(2)]
        pw = [_split_bf16(low[d]) for d in range(2)]
        for _ in range(5):
            pw = [_split_bf16(_bmm_split(pw[d], pw[d])) for d in range(2)]
            inv = [inv[d] + _bmm_split(_split_bf16(inv[d]), pw[d]) for d in range(2)]
        rhs = [jnp.concatenate([kb[d][1], kb[d][0] * jnp.exp(gc_c[d])], axis=-1) for d in range(2)]
        sol = [_bmm_split(_split_bf16(inv[d]), _split_bf16(rhs[d])) for d in range(2)]
        for d in range(2):
            ub = sol[d][:, :, :GDN_DK].astype(BF16)
            w = sol[d][:, :, GDN_DK:].astype(BF16)
            qk = (qk_raw * decay[d]).astype(BF16)
            kg = (kf * jnp.exp(g_last[d] - gc_c[d])).astype(BF16)
            m_s[d, pl.ds(c0, cg)] = _bmm_tn(kg, w).astype(BF16)
            b_s[d, pl.ds(c0, cg)] = _bmm_tn(kg, ub)
            q_s[d, pl.ds(c0, cg)] = (qs * jnp.exp(gc_c[d]) - _bmm(qk, w)).astype(BF16)
            o0_s[d, pl.ds(c0, cg)] = _bmm(qk, ub)
            el_s[d, pl.ds(c0, cg)] = jnp.broadcast_to(jnp.exp(g_last[d]), (cg, 1, GDN_DK))
        return carry

    lax.fori_loop(0, nc // cg, phase_a, 0)

    def step(t, carry):
        cs = (t, nc - 1 - t)
        sb = [carry[d].astype(BF16) for d in range(2)]
        ms = [jnp.dot(m_s[d, cs[d]], sb[d], preferred_element_type=F32) for d in range(2)]
        os_ = [jnp.dot(q_s[d, cs[d]], sb[d], preferred_element_type=F32) for d in range(2)]
        for d in range(2):
            o_s[d, cs[d]] = os_[d] + o0_s[d, cs[d]]
        return tuple(carry[d] * el_s[d, cs[d]] - ms[d] + b_s[d, cs[d]] for d in range(2))

    s0 = jnp.zeros((GDN_DK, GDN_DK), F32)
    lax.fori_loop(0, nc, step, (s0, s0))

    o = (o_s[0] + o_s[1]).reshape(t_len, GDN_DK)
    ms = jnp.mean(o * o, axis=-1, keepdims=True)
    z = z_ref[0]
    y = o * lax.rsqrt(ms + NORM_EPS) * nw_ref[...] * (z * jax.nn.sigmoid(z))
    o_ref[0] = y.astype(o_ref.dtype)


def gdn_scan(qkv, proj, gates, a_log, dt_bias, norm_w, bsz, t_len):
    nc = t_len // GDN_CHUNK
    nh = GDN_HEADS
    g4 = gates[:, :4 * nh].reshape(bsz, nc, GDN_CHUNK, 4, nh)
    gcol = jnp.transpose(g4, (0, 4, 1, 2, 3))
    grow = jnp.transpose(g4, (0, 4, 1, 3, 2))
    dk = GDN_DK
    nq = GDN_QK // dk
    return pl.pallas_call(
        _gdn_body,
        grid=(bsz, nh),
        in_specs=[
            pl.BlockSpec((1, t_len, dk), lambda b, h: (b, 0, h)),
            pl.BlockSpec((1, t_len, dk), lambda b, h: (b, 0, nq + h)),
            pl.BlockSpec((1, t_len, dk), lambda b, h: (b, 0, 2 * nq + h)),
            pl.BlockSpec((1, t_len, dk), lambda b, h: (b, 0, 3 * nq + h)),
            pl.BlockSpec((1, 1, nc, GDN_CHUNK, 4), lambda b, h: (b, h, 0, 0, 0)),
            pl.BlockSpec((1, 1, nc, 4, GDN_CHUNK), lambda b, h: (b, h, 0, 0, 0)),
            pl.BlockSpec(memory_space=pltpu.SMEM),
            pl.BlockSpec(memory_space=pltpu.SMEM),
            pl.BlockSpec((1, dk), lambda b, h: (0, 0)),
        ],
        out_specs=pl.BlockSpec((1, t_len, dk), lambda b, h: (b, 0, h)),
        out_shape=jax.ShapeDtypeStruct((bsz, t_len, GDN_V), BF16),
        scratch_shapes=[
            pltpu.VMEM((2, nc, dk, dk), BF16),
            pltpu.VMEM((2, nc, dk, dk), F32),
            pltpu.VMEM((2, nc, GDN_CHUNK, dk), BF16),
            pltpu.VMEM((2, nc, GDN_CHUNK, dk), F32),
            pltpu.VMEM((2, nc, 1, dk), F32),
            pltpu.VMEM((2, nc, GDN_CHUNK, dk), F32),
        ],
        compiler_params=_cparams("parallel", "parallel"),
        name="gdn_scan",
    )(qkv, qkv, qkv, proj, gcol, grow, a_log.astype(F32), dt_bias.astype(F32),
      norm_w.astype(F32).reshape(1, dk))


def _pad_cols(w, n):
    return jnp.pad(w, ((0, 0), (0, n - w.shape[1])))


def gdn_layer(x, gain, w_in, conv_w, a_log, dt_bias, norm_w, w_out, bsz, t_len):
    n_main = GDN_CONV_CH + GDN_V
    proj = norm_matmul(x, gain, w_in[:, :n_main].astype(BF16), F32)
    gates = norm_matmul(x, gain, _pad_cols(w_in[:, n_main:], 128).astype(BF16), F32)
    proj3 = proj.reshape(bsz, t_len, n_main)
    qkv = conv_silu(proj3, conv_w, jnp.zeros((GDN_CONV_CH,), F32), 0, GDN_CONV_CH,
                    n_l2=2 * GDN_QK // 512)
    y = gdn_scan(qkv, proj3, gates, a_log, dt_bias, norm_w, bsz, t_len)
    return matmul_residual(x, y.reshape(bsz * t_len, GDN_V), w_out.astype(BF16))


def _ssd_body(xs_ref, b_ref, c_ref, z_ref, dcol_ref, drow_ref, alogc_ref, alogr_ref, dtbc_ref, dtbr_ref, dskip_ref,
              nw_ref, o_ref, xw_s, ea_s, el_s, y_s, st_s):
    assert SSD_CHUNK == SSD_HEADDIM
    c_len = SSD_CHUNK
    t_len = xs_ref.shape[1]
    nc = t_len // c_len
    nr = SSD_HEADS // SSD_GROUPS
    hp = SSD_HEADDIM
    g = pl.program_id(1)
    ii = lax.broadcasted_iota(jnp.int32, (c_len, c_len), 0)
    jj = lax.broadcasted_iota(jnp.int32, (c_len, c_len), 1)
    cg = SSD_PHASE_A_CHUNKS

    def phase_a(gi, carry):
        c0 = pl.multiple_of(gi * cg, cg)
        rows = pl.ds(pl.multiple_of(gi * (cg * c_len), cg * c_len), cg * c_len)
        x3 = xs_ref[0, rows, :].reshape(cg, c_len, nr * hp)
        b3 = b_ref[0, rows, :].reshape(cg, c_len, SSD_STATE)
        c3 = c_ref[0, rows, :].reshape(cg, c_len, SSD_STATE)
        dcol = dcol_ref[0, 0, pl.ds(c0, cg)]
        drow = drow_ref[0, 0, pl.ds(c0, cg)]
        cb = _bmm_nt(c3, b3)
        dt_c = _softplus(dcol + dtbc_ref[0])
        dt_r = _softplus(drow + dtbr_ref[0])
        da_c = dt_c * -jnp.exp(alogc_ref[0])
        da_r = dt_r * -jnp.exp(alogr_ref[0])
        tri_f = jnp.broadcast_to((ii >= jj).astype(BF16)[None], (cg, c_len, c_len))
        tri_b = jnp.broadcast_to((ii <= jj).astype(BF16)[None], (cg, c_len, c_len))
        fwd_c = lax.broadcasted_iota(jnp.int32, (1, 1, 2 * nr), 2) < nr
        fwd_r = lax.broadcasted_iota(jnp.int32, (1, 2 * nr, 1), 1) < nr
        da_c3, da_r3 = _split3_bf16(da_c), _split3_bf16(da_r)
        ac_c = jnp.where(fwd_c, sum(_bmm(tri_f, p) for p in da_c3), sum(_bmm(tri_b, p) for p in da_c3))
        ac_r = jnp.where(fwd_r, sum(_bmm(p, tri_b) for p in da_r3), sum(_bmm(p, tri_f) for p in da_r3))
        ydiag = None
        for d in range(2):
            incl = (ii >= jj) if d == 0 else (ii <= jj)
            sel = (lax.broadcasted_iota(jnp.int32, (2 * nr, nr * hp), 1) // hp + d * nr
                   == lax.broadcasted_iota(jnp.int32, (2 * nr, nr * hp), 0)).astype(BF16)

            def spread(cols, pieces):
                parts = pieces(cols.reshape(cg * c_len, 2 * nr))
                return sum(jnp.dot(p, sel, preferred_element_type=F32) for p in parts).reshape(cg, c_len, nr * hp)

            dt_w = spread(dt_c, _split_bf16)
            ac_w = spread(ac_c, _split3_bf16)
            last = ac_w[:, c_len - 1:c_len, :] if d == 0 else ac_w[:, 0:1, :]
            xr = x3.astype(F32) * dt_w
            xw_s[d, pl.ds(c0, cg)] = (xr * jnp.exp(last - ac_w)).astype(BF16)
            ea_s[d, pl.ds(c0, cg)] = jnp.exp(ac_w)
            el_s[d, pl.ds(c0, cg)] = jnp.exp(last)
            xr = xr.astype(BF16)
            yd = []
            for r in range(nr):
                col = d * nr + r
                diff = ac_w[:, :, r * hp:r * hp + c_len] - ac_r[:, col:col + 1, :]
                seg = jnp.where(incl[None], jnp.exp(jnp.where(incl[None], diff, 0.0)), 0.0)
                yd.append(_bmm((cb * seg).astype(BF16), xr[:, :, r * hp:(r + 1) * hp]))
            yd = jnp.concatenate(yd, axis=-1)
            ydiag = yd if ydiag is None else ydiag + yd
        y_s[pl.ds(c0, cg)] = ydiag
        return carry

    lax.fori_loop(0, nc // cg, phase_a, 0)

    st_s[...] = jnp.zeros_like(st_s)

    def step(t, carry):
        for d in range(2):
            c = t if d == 0 else nc - 1 - t
            base = pl.multiple_of(c * c_len, c_len)
            cc = c_ref[0, pl.ds(base, c_len), :]
            bb = b_ref[0, pl.ds(base, c_len), :]
            st = st_s[d]
            y_s[c] += jnp.dot(cc, st.astype(BF16), preferred_element_type=F32) * ea_s[d, c]
            st_s[d] = st * el_s[d, c] + lax.dot_general(bb, xw_s[d, c], (((0,), (0,)), ((), ())),
                                                        preferred_element_type=F32)
        return carry

    lax.fori_loop(0, nc, step, 0)

    xs = xs_ref[0].astype(F32)
    z = z_ref[0]
    y = (y_s[...].reshape(t_len, nr * hp) + dskip_ref[...] * xs) * (z * jax.nn.sigmoid(z))
    ms = jnp.mean(y * y, axis=-1, keepdims=True)
    o_ref[0] = (y * lax.rsqrt(ms + NORM_EPS) * nw_ref[...]).astype(o_ref.dtype)


def ssd_scan(xbc, proj, dts, a_log, dt_bias, d_skip, norm_w, bsz, t_len):
    nc = t_len // SSD_CHUNK
    ng = SSD_GROUPS
    nr = SSD_HEADS // ng
    gw = nr * SSD_HEADDIM
    d6 = dts[:, :2 * SSD_HEADS].reshape(bsz, nc, SSD_CHUNK, 2, ng, nr)
    dcol = jnp.transpose(d6, (0, 4, 1, 2, 3, 5)).reshape(bsz, ng, nc, SSD_CHUNK, 2 * nr)
    drow = jnp.transpose(d6, (0, 4, 1, 3, 5, 2)).reshape(bsz, ng, nc, 2 * nr, SSD_CHUNK)
    nb0 = SSD_INNER // SSD_STATE
    dtb_g = jnp.transpose(dt_bias.astype(F32).reshape(2, ng, nr), (1, 0, 2)).reshape(ng, 2 * nr)
    alog_g = jnp.transpose(a_log.astype(F32).reshape(2, ng, nr), (1, 0, 2)).reshape(ng, 2 * nr)
    return pl.pallas_call(
        _ssd_body,
        grid=(bsz, ng),
        in_specs=[
            pl.BlockSpec((1, t_len, gw), lambda b, g: (b, 0, g)),
            pl.BlockSpec((1, t_len, SSD_STATE), lambda b, g: (b, 0, nb0 + g)),
            pl.BlockSpec((1, t_len, SSD_STATE), lambda b, g: (b, 0, nb0 + ng + g)),
            pl.BlockSpec((1, t_len, gw), lambda b, g: (b, 0, g)),
            pl.BlockSpec((1, 1, nc, SSD_CHUNK, 2 * nr), lambda b, g: (b, g, 0, 0, 0)),
            pl.BlockSpec((1, 1, nc, 2 * nr, SSD_CHUNK), lambda b, g: (b, g, 0, 0, 0)),
            pl.BlockSpec((1, 1, 2 * nr), lambda b, g: (g, 0, 0)),
            pl.BlockSpec((1, 2 * nr, 1), lambda b, g: (g, 0, 0)),
            pl.BlockSpec((1, 1, 2 * nr), lambda b, g: (g, 0, 0)),
            pl.BlockSpec((1, 2 * nr, 1), lambda b, g: (g, 0, 0)),
            pl.BlockSpec((1, gw), lambda b, g: (0, g)),
            pl.BlockSpec((1, gw), lambda b, g: (0, g)),
        ],
        out_specs=pl.BlockSpec((1, t_len, gw), lambda b, g: (b, 0, g)),
        out_shape=jax.ShapeDtypeStruct((bsz, t_len, SSD_INNER), BF16),
        scratch_shapes=[
            pltpu.VMEM((2, nc, SSD_CHUNK, gw), BF16),
            pltpu.VMEM((2, nc, SSD_CHUNK, gw), F32),
            pltpu.VMEM((2, nc, 1, gw), F32),
            pltpu.VMEM((nc, SSD_CHUNK, gw), F32),
            pltpu.VMEM((2, SSD_STATE, gw), F32),
        ],
        compiler_params=_cparams("parallel", "parallel"),
        name="ssd_scan",
    )(xbc, xbc, xbc, proj, dcol, drow, alog_g.reshape(ng, 1, 2 * nr), alog_g.reshape(ng, 2 * nr, 1),
      dtb_g.reshape(ng, 1, 2 * nr), dtb_g.reshape(ng, 2 * nr, 1),
      jnp.repeat(d_skip.astype(F32), SSD_HEADDIM).reshape(1, SSD_INNER),
      norm_w.astype(F32).reshape(1, SSD_INNER))


def ssd_layer(x, gain, w_in, conv_w, conv_b, a_log, dt_bias, d_skip, norm_w, w_out, bsz, t_len):
    n_main = SSD_INNER + SSD_CONV_CH
    proj = norm_matmul(x, gain, w_in[:, :n_main].astype(BF16), F32)
    dts = norm_matmul(x, gain, _pad_cols(w_in[:, n_main:], 128).astype(BF16), F32)
    proj3 = proj.reshape(bsz, t_len, n_main)
    xbc = conv_silu(proj3, conv_w, conv_b, SSD_INNER, SSD_CONV_CH)
    y = ssd_scan(xbc, proj3, dts, a_log, dt_bias, d_skip, norm_w, bsz, t_len)
    return matmul_residual(x, y.reshape(bsz * t_len, SSD_INNER), w_out.astype(BF16))


def swa_layer(x, gain, w_in, sink, w_out, bsz, t_len):
    proj = norm_matmul(x, gain, w_in.astype(BF16), BF16)
    o = swa_attention(proj.reshape(bsz, t_len, -1), sink, bsz, t_len)
    return matmul_residual(x, o.reshape(bsz * t_len, -1), w_out.astype(BF16))


def na_layer(x, gain, w_in, rpb, w_out, bsz, t_len):
    proj = norm_matmul(x, gain, w_in.astype(BF16), BF16)
    o = na_attention(proj.reshape(bsz, t_len, -1), rpb, bsz, t_len)
    return matmul_residual(x, o.reshape(bsz * t_len, -1), w_out.astype(BF16))


def kernel(x, norm_mix, norm_ffn, norm_final, gdn_w_in, gdn_conv, gdn_a_log, gdn_dt_bias, gdn_norm, gdn_w_out, ssd_w_in, ssd_conv, ssd_conv_b, ssd_a_log, ssd_dt_bias, ssd_d, ssd_norm, ssd_w_out, swa_w_in, swa_sink, swa_w_out, na_w_in, na_rpb, na_w_out, peer_w_q, peer_keys, peer_u, peer_v):
    bsz, t_len, d = x.shape
    depth = norm_mix.shape[0]
    xf = x.reshape(bsz * t_len, d)
    for i in range(depth):
        mixer, j = i % 4, i // 4
        if mixer == 0:
            xf = gdn_layer(xf, norm_mix[i], gdn_w_in[j], gdn_conv[j], gdn_a_log[j], gdn_dt_bias[j],
                           gdn_norm[j], gdn_w_out[j], bsz, t_len)
        elif mixer == 1:
            xf = ssd_layer(xf, norm_mix[i], ssd_w_in[j], ssd_conv[j], ssd_conv_b[j], ssd_a_log[j],
                           ssd_dt_bias[j], ssd_d[j], ssd_norm[j], ssd_w_out[j], bsz, t_len)
        elif mixer == 2:
            xf = swa_layer(xf, norm_mix[i], swa_w_in[j], swa_sink[j], swa_w_out[j], bsz, t_len)
        else:
            xf = na_layer(xf, norm_mix[i], na_w_in[j], na_rpb[j], na_w_out[j], bsz, t_len)
        xf = peer_layer(xf, norm_ffn[i], peer_w_q[i], peer_keys[i], peer_u[i], peer_v[i])
    return final_norm(xf, norm_final).reshape(bsz, t_len, d)
```

```python
import functools
import math

import jax
import jax.numpy as jnp
import numpy as np
from jax import lax
from jax.experimental import pallas as pl
from jax.experimental.pallas import tpu as pltpu

F32 = jnp.float32
BF16 = jnp.bfloat16
NEG = -1e30

D_MODEL = 1024
SEQ = 2048
GRID_W = 64
CONV_K = 5
NORM_EPS = 1e-6

GDN_HEADS = 8
GDN_DK = 128
GDN_CHUNK = 64
GDN_QK = 1024
GDN_V = 1024
GDN_CONV_CH = 3072

SSD_INNER = 2048
SSD_HEADDIM = 64
SSD_HEADS = 32
SSD_GROUPS = 4
SSD_STATE = 128
SSD_CHUNK = 64
SSD_BC = 512
SSD_CONV_CH = 3072

SWA_HEADS = 16
SWA_KV_HEADS = 4
SWA_HEADDIM = 64
SWA_WINDOW = 128
ROPE_THETA = 10000.0

NA_HEADS = 16
NA_HEADDIM = 64
NA_ROWS = 8
NA_COLS = 16
NA_WIDTH = 1024

PEER_HEADS = 8
PEER_NKEYS = 128
PEER_QDIM = 256
PEER_HALF = 128
PEER_TOPK = 16

GDN_PHASE_A_CHUNKS = 8
SSD_PHASE_A_CHUNKS = 8

PEER_EXPERT_ROWS = 8
PEER_Z_CHUNKS = 4

SWA_GROUPS_PER_STAGE = 2

NA_ROW_UNROLL = 4

BF16_SUBLANES = 16
F32_SUBLANES = 8

CONV_ROW_CHUNK = 256

VMEM_LIMIT_BYTES = 52 * 1024 * 1024


def _cparams(*sem, flags=None):
    return pltpu.CompilerParams(dimension_semantics=sem, vmem_limit_bytes=VMEM_LIMIT_BYTES, flags=flags)


def _norm_matmul_body(x_ref, g_ref, w_ref, o_ref, hn_ref):
    @pl.when(pl.program_id(1) == 0)
    def _():
        x = x_ref[...]
        ms = jnp.mean(x * x, axis=-1, keepdims=True)
        hn_ref[...] = (x * lax.rsqrt(ms + NORM_EPS) * g_ref[...]).astype(BF16)

    o_ref[...] = jnp.dot(hn_ref[...], w_ref[...], preferred_element_type=F32).astype(o_ref.dtype)


def norm_matmul(x, gain, w, out_dtype, tm=1024, tn=1024):
    m, d = x.shape
    n = w.shape[1]
    tn = min(tn, n)
    while n % tn:
        tn //= 2
    assert m % tm == 0 and tn % 128 == 0
    return pl.pallas_call(
        _norm_matmul_body,
        grid=(m // tm, n // tn),
        in_specs=[
            pl.BlockSpec((tm, d), lambda i, j: (i, 0)),
            pl.BlockSpec((1, d), lambda i, j: (0, 0)),
            pl.BlockSpec((d, tn), lambda i, j: (0, j)),
        ],
        out_specs=pl.BlockSpec((tm, tn), lambda i, j: (i, j)),
        out_shape=jax.ShapeDtypeStruct((m, n), out_dtype),
        scratch_shapes=[pltpu.VMEM((tm, d), BF16)],
        compiler_params=_cparams("parallel", "arbitrary"),
        name="norm_matmul",
    )(x, gain.reshape(1, d), w)


def _matmul_residual_body(x_ref, y_ref, w_ref, o_ref):
    o_ref[...] = x_ref[...] + jnp.dot(y_ref[...], w_ref[...], preferred_element_type=F32)


def matmul_residual(x, y, w, tm=1024):
    m, d = x.shape
    k = y.shape[1]
    return pl.pallas_call(
        _matmul_residual_body,
        grid=(m // tm,),
        in_specs=[
            pl.BlockSpec((tm, d), lambda i: (i, 0)),
            pl.BlockSpec((tm, k), lambda i: (i, 0)),
            pl.BlockSpec((k, d), lambda i: (0, 0)),
        ],
        out_specs=pl.BlockSpec((tm, d), lambda i: (i, 0)),
        out_shape=jax.ShapeDtypeStruct((m, d), F32),
        compiler_params=_cparams("parallel"),
        name="matmul_residual",
    )(x, y, w)


def _final_norm_body(x_ref, g_ref, o_ref):
    x = x_ref[...]
    ms = jnp.mean(x * x, axis=-1, keepdims=True)
    o_ref[...] = x * lax.rsqrt(ms + NORM_EPS) * g_ref[...]


def final_norm(x, gain, tm=1024):
    m, d = x.shape
    return pl.pallas_call(
        _final_norm_body,
        grid=(m // tm,),
        in_specs=[pl.BlockSpec((tm, d), lambda i: (i, 0)), pl.BlockSpec((1, d), lambda i: (0, 0))],
        out_specs=pl.BlockSpec((tm, d), lambda i: (i, 0)),
        out_shape=jax.ShapeDtypeStruct((m, d), F32),
        compiler_params=_cparams("parallel"),
        name="final_norm",
    )(x, gain.reshape(1, d))


def _top16_desc(s, with_rank):
    work = s
    rank = jnp.full(s.shape, 99.0, F32) if with_rank else None
    vals = []
    for r in range(PEER_TOPK):
        m = jnp.max(work, axis=0, keepdims=True)
        eq = work == m
        if with_rank:
            rank = jnp.where(eq, float(r + 1), rank)
        work = jnp.where(eq, -jnp.inf, work)
        vals.append(m)
    return jnp.concatenate(vals, axis=0), rank


def _peer_route_body(x_ref, g_ref, wqT_ref, keys_ref, hnT_ref, n_ref, e0_ref, r1_ref, e1_ref):
    x = x_ref[...]
    ms = jnp.mean(x * x, axis=-1, keepdims=True)
    hn = x * lax.rsqrt(ms + NORM_EPS) * g_ref[...]
    hnT = hn.T.astype(BF16)
    hnT_ref[...] = hnT
    qT = jnp.dot(wqT_ref[...], hnT, preferred_element_type=F32)
    row8 = lax.broadcasted_iota(jnp.int32, (8, x.shape[0]), 0)
    for h in range(PEER_HEADS):
        s = []
        for p in range(2):
            hp = 2 * h + p
            q_hp = qT[hp * PEER_HALF:(hp + 1) * PEER_HALF, :].astype(BF16)
            s.append(jnp.dot(keys_ref[hp], q_hp, preferred_element_type=F32))
        a0, _ = _top16_desc(s[0], False)
        b, rank1 = _top16_desc(s[1], True)
        cand = [a0[0:1, :] + b]
        for p in range(1, PEER_TOPK):
            cnt = PEER_TOPK // (p + 1)
            c = a0[p:p + 1, :] + b[0:8, :]
            cand.append(c if cnt >= 8 else jnp.where(row8 < cnt, c, -jnp.inf))
        best, _ = _top16_desc(jnp.concatenate(cand, axis=0), False)
        tau = best[PEER_TOPK - 1:PEER_TOPK, :]
        z = jnp.sum(jnp.exp(best - best[0:1, :]), axis=0, keepdims=True)
        n_map = jnp.zeros_like(s[0])
        for p in range(PEER_TOPK):
            n_p = jnp.sum(jnp.where(cand[p] >= tau, 1.0, 0.0), axis=0, keepdims=True)
            n_map = jnp.where(s[0] == a0[p:p + 1, :], n_p, n_map)
        n_ref[h] = n_map
        e0_ref[h] = jnp.exp(s[0] - a0[0:1, :]) * (0.5 / z)
        r1_ref[h] = rank1.astype(BF16)
        e1_ref[h] = jnp.exp(s[1] - b[0:1, :]).astype(BF16)


def peer_route(x, gain, wqT, keys, tm=256):
    m, d = x.shape
    nq = wqT.shape[0]
    tab = jax.ShapeDtypeStruct((PEER_HEADS, PEER_NKEYS, m), F32)
    tab16 = jax.ShapeDtypeStruct((PEER_HEADS, PEER_NKEYS, m), BF16)
    tab_spec = pl.BlockSpec((PEER_HEADS, PEER_NKEYS, tm), lambda i: (0, 0, i))
    return pl.pallas_call(
        _peer_route_body,
        grid=(m // tm,),
        in_specs=[
            pl.BlockSpec((tm, d), lambda i: (i, 0)),
            pl.BlockSpec((1, d), lambda i: (0, 0)),
            pl.BlockSpec((nq, d), lambda i: (0, 0)),
            pl.BlockSpec(keys.shape, lambda i: (0, 0, 0)),
        ],
        out_specs=[pl.BlockSpec((d, tm), lambda i: (0, i)), tab_spec, tab_spec, tab_spec, tab_spec],
        out_shape=[jax.ShapeDtypeStruct((d, m), BF16), tab, tab, tab16, tab16],
        compiler_params=_cparams("parallel"),
        name="peer_route",
    )(x, gain.reshape(1, d), wqT, keys)


def _peer_expert_body(x_ref, hnT_ref, n_ref, e0_ref, r1_ref, e1_ref, u_ref, vT_ref, o_ref,
                      a_s, acc_ref, *, ni, nb):
    j = pl.program_id(1)

    @pl.when(j == 0)
    def _():
        a_s[...] = jnp.zeros_like(a_s)
        acc_ref[...] = jnp.zeros_like(acc_ref)

    tm = a_s.shape[2]
    pk = BF16_SUBLANES
    nk = PEER_NKEYS
    cur = j % 2
    prev = 1 - cur

    ib = jnp.minimum(j, nb - 1)
    zc = ni * nk // PEER_Z_CHUNKS
    z_chunks = [jnp.dot(u_ref[pl.ds(c * zc, zc), :], hnT_ref[...], preferred_element_type=F32)
                for c in range(PEER_Z_CHUNKS)]
    acc_ref[...] += jnp.dot(vT_ref[0], a_s[prev], preferred_element_type=F32)
    for ii in range(ni):
        i = ib * ni + ii
        r0 = ii * nk - (ii * nk // zc) * zc
        z = z_chunks[ii * nk // zc][r0:r0 + nk, :].astype(BF16)
        act = z * (1.0 + lax.erf(z * (1.0 / math.sqrt(2.0))))
        gate = None
        for h in range(PEER_HEADS):
            n_b = jnp.broadcast_to(n_ref[h, pl.ds(i, 1), :], (pk, tm)).astype(BF16)[None]
            e0_b = jnp.broadcast_to(e0_ref[h, pl.ds(i, 1), :], (pk, tm)).astype(BF16)[None]
            r1 = r1_ref[h].reshape(nk // pk, pk, tm)
            e1 = e1_ref[h].reshape(nk // pk, pk, tm)
            term = jnp.where(r1 <= n_b, e1 * e0_b, jnp.zeros((), BF16))
            gate = term if gate is None else gate + term
        a_s[cur, pl.ds(ii * nk, nk), :] = (act.reshape(nk // pk, pk, tm) * gate).reshape(nk, tm)

    @pl.when(j == nb)
    def _():
        o_ref[...] = x_ref[...] + acc_ref[...].T


def peer_experts(x, hnT, n_tab, e0_tab, r1_tab, e1_tab, u, v_tab, tm=512, ni=PEER_EXPERT_ROWS):
    m, d = x.shape
    ne = u.shape[0]
    et = ni * PEER_NKEYS
    nb = ne // et
    vT = jnp.transpose(v_tab.astype(BF16).reshape(nb, et, d), (0, 2, 1))
    tab_spec = pl.BlockSpec((PEER_HEADS, PEER_NKEYS, tm), lambda i, j: (0, 0, i))
    return pl.pallas_call(
        functools.partial(_peer_expert_body, ni=ni, nb=nb),
        grid=(m // tm, nb + 1),
        in_specs=[
            pl.BlockSpec((tm, d), lambda i, j: (i, 0)),
            pl.BlockSpec((d, tm), lambda i, j: (0, i)),
            tab_spec, tab_spec, tab_spec, tab_spec,
            pl.BlockSpec((et, d), lambda i, j: (jnp.minimum(j, nb - 1), 0)),
            pl.BlockSpec((1, d, et), lambda i, j: (jnp.clip(j - 1, 0, nb - 1), 0, 0)),
        ],
        out_specs=pl.BlockSpec((tm, d), lambda i, j: (i, 0)),
        out_shape=jax.ShapeDtypeStruct((m, d), F32),
        scratch_shapes=[pltpu.VMEM((2, et, tm), BF16), pltpu.VMEM((d, tm), F32)],
        compiler_params=_cparams("parallel", "arbitrary"),
        name="peer_experts",
    )(x, hnT, n_tab, e0_tab, r1_tab, e1_tab, u, vT)


def peer_layer(x, gain, w_q, keys, u_tab, v_tab):
    wqT = w_q.T.astype(BF16)
    keys2 = keys.reshape(PEER_HEADS * 2, PEER_NKEYS, PEER_HALF).astype(BF16)
    hnT, n_tab, e0_tab, r1_tab, e1_tab = peer_route(x, gain, wqT, keys2)
    return peer_experts(x, hnT, n_tab, e0_tab, r1_tab, e1_tab, u_tab.astype(BF16), v_tab)


def _rope_lanes(t, cos, sin_signed):
    half = SWA_HEADDIM // 2
    lane = lax.broadcasted_iota(jnp.int32, t.shape, 1)
    first = (lane % SWA_HEADDIM) < half
    partner = jnp.where(first, pltpu.roll(t, 128 - half, 1), pltpu.roll(t, half, 1))
    return t * cos + partner * sin_signed


def _swa_body(q_ref, k_ref, v_ref, cos_ref, sin_ref, sink_ref, o_ref, kp_ref, vp_ref):
    w = SWA_WINDOW
    t_len = q_ref.shape[1]
    nkv = SWA_KV_HEADS * SWA_HEADDIM
    rep = SWA_HEADS // SWA_KV_HEADS
    zeros = jnp.zeros((w, nkv), BF16)
    kp_ref[pl.ds(0, w), :] = zeros
    kp_ref[pl.ds(w + t_len, w), :] = zeros
    vp_ref[pl.ds(0, w), :] = zeros
    vp_ref[pl.ds(w + t_len, w), :] = zeros
    vp_ref[pl.ds(w, t_len), :] = v_ref[0]
    cos_all = cos_ref[...]
    sin_all = sin_ref[...]
    for c in range(nkv // 128):
        kc = k_ref[0, :, c * 128:(c + 1) * 128].astype(F32)
        kp_ref[pl.ds(w, t_len), c * 128:(c + 1) * 128] = _rope_lanes(kc, cos_all, sin_all).astype(BF16)

    row = lax.broadcasted_iota(jnp.int32, (w, 3 * w), 0)
    col = lax.broadcasted_iota(jnp.int32, (w, 3 * w), 1)
    band = (col >= row) & (col <= row + 2 * w)

    def block(n, carry):
        base = pl.multiple_of(n * w, w)
        cos_b = cos_ref[pl.ds(base, w), :]
        sin_b = sin_ref[pl.ds(base, w), :]
        kpos = base - w + col
        valid = band & (kpos >= 0) & (kpos < t_len)
        outs = []
        for g0 in range(0, SWA_KV_HEADS, SWA_GROUPS_PER_STAGE):
            groups = range(g0, g0 + SWA_GROUPS_PER_STAGE)
            scores = []
            for g in groups:
                kw = kp_ref[pl.ds(base, 3 * w), g * SWA_HEADDIM:(g + 1) * SWA_HEADDIM]
                for c in range(g * rep // 2, (g + 1) * rep // 2):
                    qc = q_ref[0, pl.ds(base, w), c * 128:(c + 1) * 128].astype(F32)
                    qc = (_rope_lanes(qc, cos_b, sin_b) * (SWA_HEADDIM ** -0.5)).astype(BF16)
                    for hh in range(2):
                        qh = qc[:, hh * SWA_HEADDIM:(hh + 1) * SWA_HEADDIM]
                        scores.append(lax.dot_general(qh, kw, (((1,), (1,)), ((), ())),
                                                      preferred_element_type=F32))
            probs = []
            for k, s in enumerate(scores):
                s = jnp.where(valid, s, NEG)
                sk = sink_ref[g0 * rep + k]
                m = jnp.maximum(jnp.max(s, axis=-1, keepdims=True), sk)
                p = jnp.exp(s - m)
                inv = 1.0 / (jnp.sum(p, axis=-1, keepdims=True) + jnp.exp(sk - m))
                probs.append((p.astype(BF16), inv))
            for k, (p, inv) in enumerate(probs):
                g = g0 + k // rep
                vw = vp_ref[pl.ds(base, 3 * w), g * SWA_HEADDIM:(g + 1) * SWA_HEADDIM]
                outs.append(jnp.dot(p, vw, preferred_element_type=F32) * inv)
        o_ref[0, pl.ds(base, w), :] = jnp.concatenate(outs, axis=-1).astype(o_ref.dtype)
        return carry

    lax.fori_loop(0, t_len // w, block, 0)


def swa_attention(proj, sink, bsz, t_len):
    half = SWA_HEADDIM // 2
    inv_freq = ROPE_THETA ** (-jnp.arange(half, dtype=F32) / half)
    ang = jnp.arange(t_len, dtype=F32)[:, None] * inv_freq[None, :]
    cos, sin = jnp.cos(ang), jnp.sin(ang)
    cos_t = jnp.tile(jnp.concatenate([cos, cos], axis=-1), (1, 2))
    sin_t = jnp.tile(jnp.concatenate([-sin, sin], axis=-1), (1, 2))
    nq = SWA_HEADS * SWA_HEADDIM
    nkv = SWA_KV_HEADS * SWA_HEADDIM
    return pl.pallas_call(
        _swa_body,
        grid=(bsz,),
        in_specs=[
            pl.BlockSpec((1, t_len, nq), lambda b: (b, 0, 0)),
            pl.BlockSpec((1, t_len, nkv), lambda b: (b, 0, nq // nkv)),
            pl.BlockSpec((1, t_len, nkv), lambda b: (b, 0, nq // nkv + 1)),
            pl.BlockSpec((t_len, 128), lambda b: (0, 0)),
            pl.BlockSpec((t_len, 128), lambda b: (0, 0)),
            pl.BlockSpec(memory_space=pltpu.SMEM),
        ],
        out_specs=pl.BlockSpec((1, t_len, nq), lambda b: (b, 0, 0)),
        out_shape=jax.ShapeDtypeStruct((bsz, t_len, nq), BF16),
        scratch_shapes=[pltpu.VMEM((t_len + 2 * SWA_WINDOW, nkv), BF16),
                        pltpu.VMEM((t_len + 2 * SWA_WINDOW, nkv), BF16)],
        compiler_params=_cparams("parallel"),
        name="swa_attention",
    )(proj, proj, proj, cos_t, sin_t, sink.astype(F32))


def _na_bias_table(rpb):
    qc = np.arange(GRID_W)[:, None]
    kc = np.arange(GRID_W)[None, :]
    cstart = np.clip(qc - NA_COLS // 2, 0, GRID_W - NA_COLS)
    valid = (kc >= cstart) & (kc < cstart + NA_COLS)
    cidx = np.clip(kc - qc + NA_COLS - 1, 0, 2 * NA_COLS - 2)
    onehot = (np.arange(2 * NA_COLS - 1)[:, None, None] == cidx[None]).astype(np.float32)
    toep = jnp.einsum('hrc,cqk->hrqk', rpb.astype(F32), onehot, precision=lax.Precision.HIGHEST)
    toep = jnp.where(valid[None, None], toep, NEG)
    return jnp.concatenate([toep[:, :-1], toep[:, 1:]], axis=-1)


def _na_body(q_ref, k_ref, v_ref, bias_ref, o_ref):
    t_len = q_ref.shape[1]
    rows = t_len // GRID_W
    win = NA_ROWS * GRID_W

    def row_group(gi, carry):
        first = lax.broadcasted_iota(jnp.int32, (GRID_W, 2 * NA_HEADDIM), 1) < NA_HEADDIM
        chains = []
        for rr in range(NA_ROW_UNROLL):
            r = gi * NA_ROW_UNROLL + rr
            rs = jnp.clip(r - NA_ROWS // 2, 0, rows - NA_ROWS)
            d0 = rs - r + NA_ROWS - 1
            qbase = pl.multiple_of(r * GRID_W, GRID_W)
            kbase = pl.multiple_of(rs * GRID_W, GRID_W)
            qr = q_ref[0, pl.ds(qbase, GRID_W), :]
            kw = k_ref[0, pl.ds(kbase, win), :]
            for hh in range(2):
                q_h = jnp.where(first if hh == 0 else jnp.logical_not(first), qr, jnp.zeros((), BF16))
                s = lax.dot_general(q_h, kw, (((1,), (1,)), ((), ())), preferred_element_type=F32)
                chains.append((hh, d0, kbase, s))
        probs = []
        for hh, d0, kbase, s in chains:
            bias = jnp.concatenate([bias_ref[hh, d0 + 2 * c] for c in range(NA_ROWS // 2)], axis=-1)
            s = s * (NA_HEADDIM ** -0.5) + bias
            m = jnp.max(s, axis=-1, keepdims=True)
            p = jnp.exp(s - m)
            probs.append((p.astype(BF16), 1.0 / jnp.sum(p, axis=-1, keepdims=True)))
        outs = []
        for (hh, d0, kbase, s), (p, inv) in zip(chains, probs):
            vw = v_ref[0, pl.ds(kbase, win), :]
            outs.append(jnp.dot(p, vw, preferred_element_type=F32) * inv)
        for rr in range(NA_ROW_UNROLL):
            qbase = pl.multiple_of((gi * NA_ROW_UNROLL + rr) * GRID_W, GRID_W)
            o_ref[0, pl.ds(qbase, GRID_W), :] = jnp.where(first, outs[2 * rr], outs[2 * rr + 1]).astype(o_ref.dtype)
        return carry

    lax.fori_loop(0, rows // NA_ROW_UNROLL, row_group, 0)


def na_attention(proj, rpb, bsz, t_len):
    bias = _na_bias_table(rpb)
    npair = NA_HEADS // 2
    return pl.pallas_call(
        _na_body,
        grid=(npair, bsz),
        in_specs=[
            pl.BlockSpec((1, t_len, 128), lambda hp, b: (b, 0, hp)),
            pl.BlockSpec((1, t_len, 128), lambda hp, b: (b, 0, npair + hp)),
            pl.BlockSpec((1, t_len, 128), lambda hp, b: (b, 0, 2 * npair + hp)),
            pl.BlockSpec((2, 2 * NA_ROWS - 2, GRID_W, 2 * GRID_W), lambda hp, b: (hp, 0, 0, 0)),
        ],
        out_specs=pl.BlockSpec((1, t_len, 128), lambda hp, b: (b, 0, hp)),
        out_shape=jax.ShapeDtypeStruct((bsz, t_len, NA_WIDTH), BF16),
        compiler_params=_cparams("parallel", "parallel"),
        name="na_attention",
    )(proj, proj, proj, bias)


def _conv_silu_body(x_ref, w_ref, b_ref, o_ref, *, n_l2):
    t_len, tc = x_ref.shape[1], x_ref.shape[2]
    rc = CONV_ROW_CHUNK
    halo = F32_SUBLANES
    normalise = pl.program_id(1) < n_l2

    def chunk(ci, carry):
        r0 = pl.multiple_of(ci * rc, rc)
        cur = x_ref[0, pl.ds(r0, rc), :].astype(F32)
        lo = pl.multiple_of(jnp.maximum(r0 - halo, 0), halo)
        hi = pl.multiple_of(jnp.minimum(r0 + rc, t_len - halo), halo)
        before = jnp.where(r0 > 0, x_ref[0, pl.ds(lo, halo), :].astype(F32), 0.0)
        after = jnp.where(r0 + rc < t_len, x_ref[0, pl.ds(hi, halo), :].astype(F32), 0.0)
        xx = jnp.concatenate([before, cur, after], axis=0)
        acc = cur * w_ref[CONV_K // 2:CONV_K // 2 + 1, :] + b_ref[...]
        for k in range(CONV_K):
            off = k - CONV_K // 2
            if off == 0:
                continue
            acc = acc + xx[halo + off:halo + off + rc, :] * w_ref[k:k + 1, :]
        y = acc * jax.nn.sigmoid(acc)
        if n_l2 > 0:
            parts = []
            for c in range(tc // 128):
                yc = y[:, c * 128:(c + 1) * 128]
                ss = jnp.sum(yc * yc, axis=-1, keepdims=True)
                parts.append(yc * jnp.where(normalise, lax.rsqrt(ss + NORM_EPS), 1.0))
            y = jnp.concatenate(parts, axis=-1)
        o_ref[0, pl.ds(r0, rc), :] = y.astype(o_ref.dtype)
        return carry

    lax.fori_loop(0, t_len // rc, chunk, 0)


def conv_silu(proj, w, bias, col0, n_ch, n_l2=0, tc=512):
    bsz, t_len, _ = proj.shape
    assert col0 % tc == 0 and n_ch % tc == 0
    c0 = col0 // tc
    return pl.pallas_call(
        functools.partial(_conv_silu_body, n_l2=n_l2),
        grid=(bsz, n_ch // tc),
        in_specs=[
            pl.BlockSpec((1, t_len, tc), lambda b, j: (b, 0, c0 + j)),
            pl.BlockSpec((CONV_K, tc), lambda b, j: (0, j)),
            pl.BlockSpec((1, tc), lambda b, j: (0, j)),
        ],
        out_specs=pl.BlockSpec((1, t_len, tc), lambda b, j: (b, 0, j)),
        out_shape=jax.ShapeDtypeStruct((bsz, t_len, n_ch), BF16),
        compiler_params=_cparams("parallel", "parallel"),
        name="conv_silu",
    )(proj, w.astype(F32), bias.astype(F32).reshape(1, n_ch))


def _softplus(x):
    return jnp.maximum(x, 0.0) + jnp.log1p(jnp.exp(-jnp.abs(x)))


def _bmm(a, b, precision=None):
    return lax.dot_general(a, b, (((2,), (1,)), ((0,), (0,))), precision=precision,
                           preferred_element_type=F32)


def _split_bf16(a):
    hi = a.astype(BF16)
    return hi, (a - hi.astype(F32)).astype(BF16)


def _split3_bf16(a):
    hi = a.astype(BF16)
    rest = a - hi.astype(F32)
    mid = rest.astype(BF16)
    return hi, mid, (rest - mid.astype(F32)).astype(BF16)


def _bmm_split(a, b):
    return _bmm(a[0], b[0]) + _bmm(a[0], b[1]) + _bmm(a[1], b[0])


def _bmm_tn(a, b):
    return lax.dot_general(a, b, (((1,), (1,)), ((0,), (0,))), preferred_element_type=F32)


def _bmm_nt(a, b):
    return lax.dot_general(a, b, (((2,), (2,)), ((0,), (0,))), preferred_element_type=F32)


def _gdn_body(q_ref, k_ref, v_ref, z_ref, gcol_ref, grow_ref, alog_ref, dtb_ref, nw_ref, o_ref,
              m_s, b_s, q_s, o0_s, el_s, o_s):
    c_len = GDN_CHUNK
    t_len = q_ref.shape[1]
    nc = t_len // c_len
    h = pl.program_id(1)
    ii = lax.broadcasted_iota(jnp.int32, (c_len, c_len), 0)
    jj = lax.broadcasted_iota(jnp.int32, (c_len, c_len), 1)
    eye = (ii == jj).astype(F32)
    cg = GDN_PHASE_A_CHUNKS

    def phase_a(gi, carry):
        c0 = pl.multiple_of(gi * cg, cg)
        rows = pl.ds(pl.multiple_of(gi * (cg * c_len), cg * c_len), cg * c_len)
        k3 = k_ref[0, rows, :].reshape(cg, c_len, GDN_DK)
        v3 = v_ref[0, rows, :].reshape(cg, c_len, GDN_DK).astype(F32)
        kf = k3.astype(F32)
        qs = q_ref[0, rows, :].reshape(cg, c_len, GDN_DK).astype(F32) * (GDN_DK ** -0.5)
        gcol = gcol_ref[0, 0, pl.ds(c0, cg)]
        grow = grow_ref[0, 0, pl.ds(c0, cg)]
        qk_raw = _bmm_nt(qs.astype(BF16), k3)
        kb, gc_c, g_last, decay, strict = [], [], [], [], []
        for d in range(2):
            incl = (ii >= jj) if d == 0 else (ii <= jj)
            strict.append((ii > jj) if d == 0 else (ii < jj))
            tri = incl.astype(F32)
            tri_t = ((ii <= jj) if d == 0 else (ii >= jj)).astype(F32)
            neg_a = -jnp.exp(jnp.full((1, 1, 1), alog_ref[d, h], F32))
            dtb = dtb_ref[d, h]
            g_c = neg_a * _softplus(gcol[:, :, d:d + 1] + dtb)
            g_r = neg_a * _softplus(grow[:, d:d + 1, :] + dtb)
            beta_c = jax.nn.sigmoid(gcol[:, :, 2 + d:3 + d])
            gc_c.append(jnp.sum(tri[None] * g_r, axis=2, keepdims=True))
            gc_r = jnp.sum(tri_t[None] * g_c, axis=1, keepdims=True)
            g_last.append(jnp.sum(g_r, axis=2, keepdims=True))
            decay.append(jnp.where(incl[None], jnp.exp(jnp.where(incl[None], gc_c[d] - gc_r, 0.0)), 0.0))
            kb.append((kf * beta_c, v3 * beta_c))
        kk = [_bmm_nt(kb[d][0].astype(BF16), k3) for d in range(2)]
        low = [jnp.where(strict[d][None], kk[d] * decay[d], 0.0) for d in range(2)]
        inv = [eye[None] - low[d] for d in range(2)]
        pw = [_split_bf16(low[d]) for d in range(2)]
        for _ in range(5):
            pw = [_split_bf16(_bmm_split(pw[d], pw[d])) for d in range(2)]
            inv = [inv[d] + _bmm_split(_split_bf16(inv[d]), pw[d]) for d in range(2)]
        rhs = [jnp.concatenate([kb[d][1], kb[d][0] * jnp.exp(gc_c[d])], axis=-1) for d in range(2)]
        sol = [_bmm_split(_split_bf16(inv[d]), _split_bf16(rhs[d])) for d in range(2)]
        for d in range(2):
            ub = sol[d][:, :, :GDN_DK].astype(BF16)
            w = sol[d][:, :, GDN_DK:].astype(BF16)
            qk = (qk_raw * decay[d]).astype(BF16)
            kg = (kf * jnp.exp(g_last[d] - gc_c[d])).astype(BF16)
            m_s[d, pl.ds(c0, cg)] = _bmm_tn(kg, w).astype(BF16)
            b_s[d, pl.ds(c0, cg)] = _bmm_tn(kg, ub)
            q_s[d, pl.ds(c0, cg)] = (qs * jnp.exp(gc_c[d]) - _bmm(qk, w)).astype(BF16)
            o0_s[d, pl.ds(c0, cg)] = _bmm(qk, ub)
            el_s[d, pl.ds(c0, cg)] = jnp.broadcast_to(jnp.exp(g_last[d]), (cg, 1, GDN_DK))
        return carry

    lax.fori_loop(0, nc // cg, phase_a, 0)

    def step(t, carry):
        cs = (t, nc - 1 - t)
        sb = [carry[d].astype(BF16) for d in range(2)]
        ms = [jnp.dot(m_s[d, cs[d]], sb[d], preferred_element_type=F32) for d in range(2)]
        os_ = [jnp.dot(q_s[d, cs[d]], sb[d], preferred_element_type=F32) for d in range(2)]
        for d in range(2):
            o_s[d, cs[d]] = os_[d] + o0_s[d, cs[d]]
        return tuple(carry[d] * el_s[d, cs[d]] - ms[d] + b_s[d, cs[d]] for d in range(2))

    s0 = jnp.zeros((GDN_DK, GDN_DK), F32)
    lax.fori_loop(0, nc, step, (s0, s0))

    o = (o_s[0] + o_s[1]).reshape(t_len, GDN_DK)
    ms = jnp.mean(o * o, axis=-1, keepdims=True)
    z = z_ref[0]
    y = o * lax.rsqrt(ms + NORM_EPS) * nw_ref[...] * (z * jax.nn.sigmoid(z))
    o_ref[0] = y.astype(o_ref.dtype)


def gdn_scan(qkv, proj, gates, a_log, dt_bias, norm_w, bsz, t_len):
    nc = t_len // GDN_CHUNK
    nh = GDN_HEADS
    g4 = gates[:, :4 * nh].reshape(bsz, nc, GDN_CHUNK, 4, nh)
    gcol = jnp.transpose(g4, (0, 4, 1, 2, 3))
    grow = jnp.transpose(g4, (0, 4, 1, 3, 2))
    dk = GDN_DK
    nq = GDN_QK // dk
    return pl.pallas_call(
        _gdn_body,
        grid=(bsz, nh),
        in_specs=[
            pl.BlockSpec((1, t_len, dk), lambda b, h: (b, 0, h)),
            pl.BlockSpec((1, t_len, dk), lambda b, h: (b, 0, nq + h)),
            pl.BlockSpec((1, t_len, dk), lambda b, h: (b, 0, 2 * nq + h)),
            pl.BlockSpec((1, t_len, dk), lambda b, h: (b, 0, 3 * nq + h)),
            pl.BlockSpec((1, 1, nc, GDN_CHUNK, 4), lambda b, h: (b, h, 0, 0, 0)),
            pl.BlockSpec((1, 1, nc, 4, GDN_CHUNK), lambda b, h: (b, h, 0, 0, 0)),
            pl.BlockSpec(memory_space=pltpu.SMEM),
            pl.BlockSpec(memory_space=pltpu.SMEM),
            pl.BlockSpec((1, dk), lambda b, h: (0, 0)),
        ],
        out_specs=pl.BlockSpec((1, t_len, dk), lambda b, h: (b, 0, h)),
        out_shape=jax.ShapeDtypeStruct((bsz, t_len, GDN_V), BF16),
        scratch_shapes=[
            pltpu.VMEM((2, nc, dk, dk), BF16),
            pltpu.VMEM((2, nc, dk, dk), F32),
            pltpu.VMEM((2, nc, GDN_CHUNK, dk), BF16),
            pltpu.VMEM((2, nc, GDN_CHUNK, dk), F32),
            pltpu.VMEM((2, nc, 1, dk), F32),
            pltpu.VMEM((2, nc, GDN_CHUNK, dk), F32),
        ],
        compiler_params=_cparams("parallel", "parallel"),
        name="gdn_scan",
    )(qkv, qkv, qkv, proj, gcol, grow, a_log.astype(F32), dt_bias.astype(F32),
      norm_w.astype(F32).reshape(1, dk))


def _pad_cols(w, n):
    return jnp.pad(w, ((0, 0), (0, n - w.shape[1])))


def gdn_layer(x, gain, w_in, conv_w, a_log, dt_bias, norm_w, w_out, bsz, t_len):
    n_main = GDN_CONV_CH + GDN_V
    proj = norm_matmul(x, gain, w_in[:, :n_main].astype(BF16), F32)
    gates = norm_matmul(x, gain, _pad_cols(w_in[:, n_main:], 128).astype(BF16), F32)
    proj3 = proj.reshape(bsz, t_len, n_main)
    qkv = conv_silu(proj3, conv_w, jnp.zeros((GDN_CONV_CH,), F32), 0, GDN_CONV_CH,
                    n_l2=2 * GDN_QK // 512)
    y = gdn_scan(qkv, proj3, gates, a_log, dt_bias, norm_w, bsz, t_len)
    return matmul_residual(x, y.reshape(bsz * t_len, GDN_V), w_out.astype(BF16))


def _ssd_body(xs_ref, b_ref, c_ref, z_ref, dcol_ref, drow_ref, alogc_ref, alogr_ref, dtbc_ref, dtbr_ref, dskip_ref,
              nw_ref, o_ref, xw_s, ea_s, el_s, y_s, st_s):
    assert SSD_CHUNK == SSD_HEADDIM
    c_len = SSD_CHUNK
    t_len = xs_ref.shape[1]
    nc = t_len // c_len
    nr = SSD_HEADS // SSD_GROUPS
    hp = SSD_HEADDIM
    g = pl.program_id(1)
    ii = lax.broadcasted_iota(jnp.int32, (c_len, c_len), 0)
    jj = lax.broadcasted_iota(jnp.int32, (c_len, c_len), 1)
    cg = SSD_PHASE_A_CHUNKS

    def phase_a(gi, carry):
        c0 = pl.multiple_of(gi * cg, cg)
        rows = pl.ds(pl.multiple_of(gi * (cg * c_len), cg * c_len), cg * c_len)
        x3 = xs_ref[0, rows, :].reshape(cg, c_len, nr * hp)
        b3 = b_ref[0, rows, :].reshape(cg, c_len, SSD_STATE)
        c3 = c_ref[0, rows, :].reshape(cg, c_len, SSD_STATE)
        dcol = dcol_ref[0, 0, pl.ds(c0, cg)]
        drow = drow_ref[0, 0, pl.ds(c0, cg)]
        cb = _bmm_nt(c3, b3)
        dt_c = _softplus(dcol + dtbc_ref[0])
        dt_r = _softplus(drow + dtbr_ref[0])
        da_c = dt_c * -jnp.exp(alogc_ref[0])
        da_r = dt_r * -jnp.exp(alogr_ref[0])
        tri_f = jnp.broadcast_to((ii >= jj).astype(BF16)[None], (cg, c_len, c_len))
        tri_b = jnp.broadcast_to((ii <= jj).astype(BF16)[None], (cg, c_len, c_len))
        fwd_c = lax.broadcasted_iota(jnp.int32, (1, 1, 2 * nr), 2) < nr
        fwd_r = lax.broadcasted_iota(jnp.int32, (1, 2 * nr, 1), 1) < nr
        da_c3, da_r3 = _split3_bf16(da_c), _split3_bf16(da_r)
        ac_c = jnp.where(fwd_c, sum(_bmm(tri_f, p) for p in da_c3), sum(_bmm(tri_b, p) for p in da_c3))
        ac_r = jnp.where(fwd_r, sum(_bmm(p, tri_b) for p in da_r3), sum(_bmm(p, tri_f) for p in da_r3))
        ydiag = None
        for d in range(2):
            incl = (ii >= jj) if d == 0 else (ii <= jj)
            sel = (lax.broadcasted_iota(jnp.int32, (2 * nr, nr * hp), 1) // hp + d * nr
                   == lax.broadcasted_iota(jnp.int32, (2 * nr, nr * hp), 0)).astype(BF16)

            def spread(cols, pieces):
                parts = pieces(cols.reshape(cg * c_len, 2 * nr))
                return sum(jnp.dot(p, sel, preferred_element_type=F32) for p in parts).reshape(cg, c_len, nr * hp)

            dt_w = spread(dt_c, _split_bf16)
            ac_w = spread(ac_c, _split3_bf16)
            last = ac_w[:, c_len - 1:c_len, :] if d == 0 else ac_w[:, 0:1, :]
            xr = x3.astype(F32) * dt_w
            xw_s[d, pl.ds(c0, cg)] = (xr * jnp.exp(last - ac_w)).astype(BF16)
            ea_s[d, pl.ds(c0, cg)] = jnp.exp(ac_w)
            el_s[d, pl.ds(c0, cg)] = jnp.exp(last)
            xr = xr.astype(BF16)
            yd = []
            for r in range(nr):
                col = d * nr + r
                diff = ac_w[:, :, r * hp:r * hp + c_len] - ac_r[:, col:col + 1, :]
                seg = jnp.where(incl[None], jnp.exp(jnp.where(incl[None], diff, 0.0)), 0.0)
                yd.append(_bmm((cb * seg).astype(BF16), xr[:, :, r * hp:(r + 1) * hp]))
            yd = jnp.concatenate(yd, axis=-1)
            ydiag = yd if ydiag is None else ydiag + yd
        y_s[pl.ds(c0, cg)] = ydiag
        return carry

    lax.fori_loop(0, nc // cg, phase_a, 0)

    st_s[...] = jnp.zeros_like(st_s)

    def step(t, carry):
        for d in range(2):
            c = t if d == 0 else nc - 1 - t
            base = pl.multiple_of(c * c_len, c_len)
            cc = c_ref[0, pl.ds(base, c_len), :]
            bb = b_ref[0, pl.ds(base, c_len), :]
            st = st_s[d]
            y_s[c] += jnp.dot(cc, st.astype(BF16), preferred_element_type=F32) * ea_s[d, c]
            st_s[d] = st * el_s[d, c] + lax.dot_general(bb, xw_s[d, c], (((0,), (0,)), ((), ())),
                                                        preferred_element_type=F32)
        return carry

    lax.fori_loop(0, nc, step, 0)

    xs = xs_ref[0].astype(F32)
    z = z_ref[0]
    y = (y_s[...].reshape(t_len, nr * hp) + dskip_ref[...] * xs) * (z * jax.nn.sigmoid(z))
    ms = jnp.mean(y * y, axis=-1, keepdims=True)
    o_ref[0] = (y * lax.rsqrt(ms + NORM_EPS) * nw_ref[...]).astype(o_ref.dtype)


def ssd_scan(xbc, proj, dts, a_log, dt_bias, d_skip, norm_w, bsz, t_len):
    nc = t_len // SSD_CHUNK
    ng = SSD_GROUPS
    nr = SSD_HEADS // ng
    gw = nr * SSD_HEADDIM
    d6 = dts[:, :2 * SSD_HEADS].reshape(bsz, nc, SSD_CHUNK, 2, ng, nr)
    dcol = jnp.transpose(d6, (0, 4, 1, 2, 3, 5)).reshape(bsz, ng, nc, SSD_CHUNK, 2 * nr)
    drow = jnp.transpose(d6, (0, 4, 1, 3, 5, 2)).reshape(bsz, ng, nc, 2 * nr, SSD_CHUNK)
    nb0 = SSD_INNER // SSD_STATE
    dtb_g = jnp.transpose(dt_bias.astype(F32).reshape(2, ng, nr), (1, 0, 2)).reshape(ng, 2 * nr)
    alog_g = jnp.transpose(a_log.astype(F32).reshape(2, ng, nr), (1, 0, 2)).reshape(ng, 2 * nr)
    return pl.pallas_call(
        _ssd_body,
        grid=(bsz, ng),
        in_specs=[
            pl.BlockSpec((1, t_len, gw), lambda b, g: (b, 0, g)),
            pl.BlockSpec((1, t_len, SSD_STATE), lambda b, g: (b, 0, nb0 + g)),
            pl.BlockSpec((1, t_len, SSD_STATE), lambda b, g: (b, 0, nb0 + ng + g)),
            pl.BlockSpec((1, t_len, gw), lambda b, g: (b, 0, g)),
            pl.BlockSpec((1, 1, nc, SSD_CHUNK, 2 * nr), lambda b, g: (b, g, 0, 0, 0)),
            pl.BlockSpec((1, 1, nc, 2 * nr, SSD_CHUNK), lambda b, g: (b, g, 0, 0, 0)),
            pl.BlockSpec((1, 1, 2 * nr), lambda b, g: (g, 0, 0)),
            pl.BlockSpec((1, 2 * nr, 1), lambda b, g: (g, 0, 0)),
            pl.BlockSpec((1, 1, 2 * nr), lambda b, g: (g, 0, 0)),
            pl.BlockSpec((1, 2 * nr, 1), lambda b, g: (g, 0, 0)),
            pl.BlockSpec((1, gw), lambda b, g: (0, g)),
            pl.BlockSpec((1, gw), lambda b, g: (0, g)),
        ],
        out_specs=pl.BlockSpec((1, t_len, gw), lambda b, g: (b, 0, g)),
        out_shape=jax.ShapeDtypeStruct((bsz, t_len, SSD_INNER), BF16),
        scratch_shapes=[
            pltpu.VMEM((2, nc, SSD_CHUNK, gw), BF16),
            pltpu.VMEM((2, nc, SSD_CHUNK, gw), F32),
            pltpu.VMEM((2, nc, 1, gw), F32),
            pltpu.VMEM((nc, SSD_CHUNK, gw), F32),
            pltpu.VMEM((2, SSD_STATE, gw), F32),
        ],
        compiler_params=_cparams("parallel", "parallel"),
        name="ssd_scan",
    )(xbc, xbc, xbc, proj, dcol, drow, alog_g.reshape(ng, 1, 2 * nr), alog_g.reshape(ng, 2 * nr, 1),
      dtb_g.reshape(ng, 1, 2 * nr), dtb_g.reshape(ng, 2 * nr, 1),
      jnp.repeat(d_skip.astype(F32), SSD_HEADDIM).reshape(1, SSD_INNER),
      norm_w.astype(F32).reshape(1, SSD_INNER))


def ssd_layer(x, gain, w_in, conv_w, conv_b, a_log, dt_bias, d_skip, norm_w, w_out, bsz, t_len):
    n_main = SSD_INNER + SSD_CONV_CH
    proj = norm_matmul(x, gain, w_in[:, :n_main].astype(BF16), F32)
    dts = norm_matmul(x, gain, _pad_cols(w_in[:, n_main:], 128).astype(BF16), F32)
    proj3 = proj.reshape(bsz, t_len, n_main)
    xbc = conv_silu(proj3, conv_w, conv_b, SSD_INNER, SSD_CONV_CH)
    y = ssd_scan(xbc, proj3, dts, a_log, dt_bias, d_skip, norm_w, bsz, t_len)
    return matmul_residual(x, y.reshape(bsz * t_len, SSD_INNER), w_out.astype(BF16))


def swa_layer(x, gain, w_in, sink, w_out, bsz, t_len):
    proj = norm_matmul(x, gain, w_in.astype(BF16), BF16)
    o = swa_attention(proj.reshape(bsz, t_len, -1), sink, bsz, t_len)
    return matmul_residual(x, o.reshape(bsz * t_len, -1), w_out.astype(BF16))


def na_layer(x, gain, w_in, rpb, w_out, bsz, t_len):
    proj = norm_matmul(x, gain, w_in.astype(BF16), BF16)
    o = na_attention(proj.reshape(bsz, t_len, -1), rpb, bsz, t_len)
    return matmul_residual(x, o.reshape(bsz * t_len, -1), w_out.astype(BF16))


def kernel(x, norm_mix, norm_ffn, norm_final, gdn_w_in, gdn_conv, gdn_a_log, gdn_dt_bias, gdn_norm, gdn_w_out, ssd_w_in, ssd_conv, ssd_conv_b, ssd_a_log, ssd_dt_bias, ssd_d, ssd_norm, ssd_w_out, swa_w_in, swa_sink, swa_w_out, na_w_in, na_rpb, na_w_out, peer_w_q, peer_keys, peer_u, peer_v):
    bsz, t_len, d = x.shape
    depth = norm_mix.shape[0]
    xf = x.reshape(bsz * t_len, d)
    for i in range(depth):
        mixer, j = i % 4, i // 4
        if mixer == 0:
            xf = gdn_layer(xf, norm_mix[i], gdn_w_in[j], gdn_conv[j], gdn_a_log[j], gdn_dt_bias[j],
                           gdn_norm[j], gdn_w_out[j], bsz, t_len)
        elif mixer == 1:
            xf = ssd_layer(xf, norm_mix[i], ssd_w_in[j], ssd_conv[j], ssd_conv_b[j], ssd_a_log[j],
                           ssd_dt_bias[j], ssd_d[j], ssd_norm[j], ssd_w_out[j], bsz, t_len)
        elif mixer == 2:
            xf = swa_layer(xf, norm_mix[i], swa_w_in[j], swa_sink[j], swa_w_out[j], bsz, t_len)
        else:
            xf = na_layer(xf, norm_mix[i], na_w_in[j], na_rpb[j], na_w_out[j], bsz, t_len)
        xf = peer_layer(xf, norm_ffn[i], peer_w_q[i], peer_keys[i], peer_u[i], peer_v[i])
    return final_norm(xf, norm_final).reshape(bsz, t_len, d)
```

```python
import functools
import math

import jax
import jax.numpy as jnp
import numpy as np
from jax import lax
from jax.experimental import pallas as pl
from jax.experimental.pallas import tpu as pltpu

F32 = jnp.float32
BF16 = jnp.bfloat16
NEG = -1e30

D_MODEL = 1024
SEQ = 2048
GRID_W = 64
CONV_K = 5
NORM_EPS = 1e-6

GDN_HEADS = 8
GDN_DK = 128
GDN_CHUNK = 64
GDN_QK = 1024
GDN_V = 1024
GDN_CONV_CH = 3072

SSD_INNER = 2048
SSD_HEADDIM = 64
SSD_HEADS = 32
SSD_GROUPS = 4
SSD_STATE = 128
SSD_CHUNK = 64
SSD_BC = 512
SSD_CONV_CH = 3072

SWA_HEADS = 16
SWA_KV_HEADS = 4
SWA_HEADDIM = 64
SWA_WINDOW = 128
ROPE_THETA = 10000.0

NA_HEADS = 16
NA_HEADDIM = 64
NA_ROWS = 8
NA_COLS = 16
NA_WIDTH = 1024

PEER_HEADS = 8
PEER_NKEYS = 128
PEER_QDIM = 256
PEER_HALF = 128
PEER_TOPK = 16

GDN_PHASE_A_CHUNKS = 8
SSD_PHASE_A_CHUNKS = 8

PEER_EXPERT_ROWS = 8
PEER_Z_CHUNKS = 4

SWA_GROUPS_PER_STAGE = 2

NA_ROW_UNROLL = 4

LANES = 128
BF16_SUBLANES = 16
F32_SUBLANES = 8

CONV_ROW_CHUNK = 256

VMEM_LIMIT_BYTES = 52 * 1024 * 1024


def _cparams(*sem, flags=None):
    return pltpu.CompilerParams(dimension_semantics=sem, vmem_limit_bytes=VMEM_LIMIT_BYTES, flags=flags)


def _norm_matmul_body(x_ref, g_ref, w_ref, o_ref, hn_ref):
    @pl.when(pl.program_id(1) == 0)
    def _():
        x = x_ref[...]
        ms = jnp.mean(x * x, axis=-1, keepdims=True)
        hn_ref[...] = (x * lax.rsqrt(ms + NORM_EPS) * g_ref[...]).astype(BF16)

    o_ref[...] = jnp.dot(hn_ref[...], w_ref[...], preferred_element_type=F32).astype(o_ref.dtype)


def norm_matmul(x, gain, w, out_dtype, tm=1024, tn=1024):
    m, d = x.shape
    n = w.shape[1]
    tn = min(tn, n)
    while n % tn:
        tn //= 2
    assert m % tm == 0 and tn % LANES == 0
    return pl.pallas_call(
        _norm_matmul_body,
        grid=(m // tm, n // tn),
        in_specs=[
            pl.BlockSpec((tm, d), lambda i, j: (i, 0)),
            pl.BlockSpec((1, d), lambda i, j: (0, 0)),
            pl.BlockSpec((d, tn), lambda i, j: (0, j)),
        ],
        out_specs=pl.BlockSpec((tm, tn), lambda i, j: (i, j)),
        out_shape=jax.ShapeDtypeStruct((m, n), out_dtype),
        scratch_shapes=[pltpu.VMEM((tm, d), BF16)],
        compiler_params=_cparams("parallel", "arbitrary"),
        name="norm_matmul",
    )(x, gain.reshape(1, d), w)


def _matmul_residual_body(x_ref, y_ref, w_ref, o_ref):
    o_ref[...] = x_ref[...] + jnp.dot(y_ref[...], w_ref[...], preferred_element_type=F32)


def matmul_residual(x, y, w, tm=1024):
    m, d = x.shape
    k = y.shape[1]
    return pl.pallas_call(
        _matmul_residual_body,
        grid=(m // tm,),
        in_specs=[
            pl.BlockSpec((tm, d), lambda i: (i, 0)),
            pl.BlockSpec((tm, k), lambda i: (i, 0)),
            pl.BlockSpec((k, d), lambda i: (0, 0)),
        ],
        out_specs=pl.BlockSpec((tm, d), lambda i: (i, 0)),
        out_shape=jax.ShapeDtypeStruct((m, d), F32),
        compiler_params=_cparams("parallel"),
        name="matmul_residual",
    )(x, y, w)


def _final_norm_body(x_ref, g_ref, o_ref):
    x = x_ref[...]
    ms = jnp.mean(x * x, axis=-1, keepdims=True)
    o_ref[...] = x * lax.rsqrt(ms + NORM_EPS) * g_ref[...]


def final_norm(x, gain, tm=1024):
    m, d = x.shape
    return pl.pallas_call(
        _final_norm_body,
        grid=(m // tm,),
        in_specs=[pl.BlockSpec((tm, d), lambda i: (i, 0)), pl.BlockSpec((1, d), lambda i: (0, 0))],
        out_specs=pl.BlockSpec((tm, d), lambda i: (i, 0)),
        out_shape=jax.ShapeDtypeStruct((m, d), F32),
        compiler_params=_cparams("parallel"),
        name="final_norm",
    )(x, gain.reshape(1, d))


def _top16_desc(s, with_rank):
    work = s
    rank = jnp.full(s.shape, 99.0, F32) if with_rank else None
    vals = []
    for r in range(PEER_TOPK):
        m = jnp.max(work, axis=0, keepdims=True)
        eq = work == m
        if with_rank:
            rank = jnp.where(eq, float(r + 1), rank)
        work = jnp.where(eq, -jnp.inf, work)
        vals.append(m)
    return jnp.concatenate(vals, axis=0), rank


def _peer_route_body(x_ref, g_ref, wqT_ref, keys_ref, hnT_ref, n_ref, e0_ref, r1_ref, e1_ref):
    x = x_ref[...]
    ms = jnp.mean(x * x, axis=-1, keepdims=True)
    hn = x * lax.rsqrt(ms + NORM_EPS) * g_ref[...]
    hnT = hn.T.astype(BF16)
    hnT_ref[...] = hnT
    qT = jnp.dot(wqT_ref[...], hnT, preferred_element_type=F32)
    row8 = lax.broadcasted_iota(jnp.int32, (8, x.shape[0]), 0)
    for h in range(PEER_HEADS):
        s = []
        for p in range(2):
            hp = 2 * h + p
            q_hp = qT[hp * PEER_HALF:(hp + 1) * PEER_HALF, :].astype(BF16)
            s.append(jnp.dot(keys_ref[hp], q_hp, preferred_element_type=F32))
        a0, _ = _top16_desc(s[0], False)
        b, rank1 = _top16_desc(s[1], True)
        cand = [a0[0:1, :] + b]
        for p in range(1, PEER_TOPK):
            cnt = PEER_TOPK // (p + 1)
            c = a0[p:p + 1, :] + b[0:8, :]
            cand.append(c if cnt >= 8 else jnp.where(row8 < cnt, c, -jnp.inf))
        best, _ = _top16_desc(jnp.concatenate(cand, axis=0), False)
        tau = best[PEER_TOPK - 1:PEER_TOPK, :]
        z = jnp.sum(jnp.exp(best - best[0:1, :]), axis=0, keepdims=True)
        n_map = jnp.zeros_like(s[0])
        for p in range(PEER_TOPK):
            n_p = jnp.sum(jnp.where(cand[p] >= tau, 1.0, 0.0), axis=0, keepdims=True)
            n_map = jnp.where(s[0] == a0[p:p + 1, :], n_p, n_map)
        n_ref[h] = n_map
        e0_ref[h] = jnp.exp(s[0] - a0[0:1, :]) * (0.5 / z)
        r1_ref[h] = rank1.astype(BF16)
        e1_ref[h] = jnp.exp(s[1] - b[0:1, :]).astype(BF16)


def peer_route(x, gain, wqT, keys, tm=256):
    m, d = x.shape
    nq = wqT.shape[0]
    tab = jax.ShapeDtypeStruct((PEER_HEADS, PEER_NKEYS, m), F32)
    tab16 = jax.ShapeDtypeStruct((PEER_HEADS, PEER_NKEYS, m), BF16)
    tab_spec = pl.BlockSpec((PEER_HEADS, PEER_NKEYS, tm), lambda i: (0, 0, i))
    return pl.pallas_call(
        _peer_route_body,
        grid=(m // tm,),
        in_specs=[
            pl.BlockSpec((tm, d), lambda i: (i, 0)),
            pl.BlockSpec((1, d), lambda i: (0, 0)),
            pl.BlockSpec((nq, d), lambda i: (0, 0)),
            pl.BlockSpec(keys.shape, lambda i: (0, 0, 0)),
        ],
        out_specs=[pl.BlockSpec((d, tm), lambda i: (0, i)), tab_spec, tab_spec, tab_spec, tab_spec],
        out_shape=[jax.ShapeDtypeStruct((d, m), BF16), tab, tab, tab16, tab16],
        compiler_params=_cparams("parallel"),
        name="peer_route",
    )(x, gain.reshape(1, d), wqT, keys)


def _peer_expert_body(x_ref, hnT_ref, n_ref, e0_ref, r1_ref, e1_ref, u_ref, vT_ref, o_ref,
                      a_s, acc_ref, *, ni, nb):
    j = pl.program_id(1)

    @pl.when(j == 0)
    def _():
        a_s[...] = jnp.zeros_like(a_s)
        acc_ref[...] = jnp.zeros_like(acc_ref)

    tm = a_s.shape[2]
    pk = BF16_SUBLANES
    nk = PEER_NKEYS
    cur = j % 2
    prev = 1 - cur

    ib = jnp.minimum(j, nb - 1)
    zc = ni * nk // PEER_Z_CHUNKS
    z_chunks = [jnp.dot(u_ref[pl.ds(c * zc, zc), :], hnT_ref[...], preferred_element_type=F32)
                for c in range(PEER_Z_CHUNKS)]
    acc_ref[...] += jnp.dot(vT_ref[0], a_s[prev], preferred_element_type=F32)
    for ii in range(ni):
        i = ib * ni + ii
        r0 = ii * nk - (ii * nk // zc) * zc
        z = z_chunks[ii * nk // zc][r0:r0 + nk, :].astype(BF16)
        act = z * (1.0 + lax.erf(z * (1.0 / math.sqrt(2.0))))
        gate = None
        for h in range(PEER_HEADS):
            n_b = jnp.broadcast_to(n_ref[h, pl.ds(i, 1), :], (pk, tm)).astype(BF16)[None]
            e0_b = jnp.broadcast_to(e0_ref[h, pl.ds(i, 1), :], (pk, tm)).astype(BF16)[None]
            r1 = r1_ref[h].reshape(nk // pk, pk, tm)
            e1 = e1_ref[h].reshape(nk // pk, pk, tm)
            term = jnp.where(r1 <= n_b, e1 * e0_b, jnp.zeros((), BF16))
            gate = term if gate is None else gate + term
        a_s[cur, pl.ds(ii * nk, nk), :] = (act.reshape(nk // pk, pk, tm) * gate).reshape(nk, tm)

    @pl.when(j == nb)
    def _():
        o_ref[...] = x_ref[...] + acc_ref[...].T


def peer_experts(x, hnT, n_tab, e0_tab, r1_tab, e1_tab, u, v_tab, tm=512, ni=PEER_EXPERT_ROWS):
    m, d = x.shape
    ne = u.shape[0]
    et = ni * PEER_NKEYS
    nb = ne // et
    vT = jnp.transpose(v_tab.astype(BF16).reshape(nb, et, d), (0, 2, 1))
    tab_spec = pl.BlockSpec((PEER_HEADS, PEER_NKEYS, tm), lambda i, j: (0, 0, i))
    return pl.pallas_call(
        functools.partial(_peer_expert_body, ni=ni, nb=nb),
        grid=(m // tm, nb + 1),
        in_specs=[
            pl.BlockSpec((tm, d), lambda i, j: (i, 0)),
            pl.BlockSpec((d, tm), lambda i, j: (0, i)),
            tab_spec, tab_spec, tab_spec, tab_spec,
            pl.BlockSpec((et, d), lambda i, j: (jnp.minimum(j, nb - 1), 0)),
            pl.BlockSpec((1, d, et), lambda i, j: (jnp.clip(j - 1, 0, nb - 1), 0, 0)),
        ],
        out_specs=pl.BlockSpec((tm, d), lambda i, j: (i, 0)),
        out_shape=jax.ShapeDtypeStruct((m, d), F32),
        scratch_shapes=[pltpu.VMEM((2, et, tm), BF16), pltpu.VMEM((d, tm), F32)],
        compiler_params=_cparams("parallel", "arbitrary"),
        name="peer_experts",
    )(x, hnT, n_tab, e0_tab, r1_tab, e1_tab, u, vT)


def peer_layer(x, gain, w_q, keys, u_tab, v_tab):
    wqT = w_q.T.astype(BF16)
    keys2 = keys.reshape(PEER_HEADS * 2, PEER_NKEYS, PEER_HALF).astype(BF16)
    hnT, n_tab, e0_tab, r1_tab, e1_tab = peer_route(x, gain, wqT, keys2)
    return peer_experts(x, hnT, n_tab, e0_tab, r1_tab, e1_tab, u_tab.astype(BF16), v_tab)


def _rope_lanes(t, cos, sin_signed):
    half = SWA_HEADDIM // 2
    lane = lax.broadcasted_iota(jnp.int32, t.shape, 1)
    first = (lane % SWA_HEADDIM) < half
    partner = jnp.where(first, pltpu.roll(t, LANES - half, 1), pltpu.roll(t, half, 1))
    return t * cos + partner * sin_signed


def _swa_body(q_ref, k_ref, v_ref, cos_ref, sin_ref, sink_ref, o_ref, kp_ref, vp_ref):
    w = SWA_WINDOW
    t_len = q_ref.shape[1]
    nkv = SWA_KV_HEADS * SWA_HEADDIM
    rep = SWA_HEADS // SWA_KV_HEADS
    zeros = jnp.zeros((w, nkv), BF16)
    kp_ref[pl.ds(0, w), :] = zeros
    kp_ref[pl.ds(w + t_len, w), :] = zeros
    vp_ref[pl.ds(0, w), :] = zeros
    vp_ref[pl.ds(w + t_len, w), :] = zeros
    vp_ref[pl.ds(w, t_len), :] = v_ref[0]
    cos_all = cos_ref[...]
    sin_all = sin_ref[...]
    for c in range(nkv // LANES):
        kc = k_ref[0, :, c * LANES:(c + 1) * LANES].astype(F32)
        kp_ref[pl.ds(w, t_len), c * LANES:(c + 1) * LANES] = _rope_lanes(kc, cos_all, sin_all).astype(BF16)

    row = lax.broadcasted_iota(jnp.int32, (w, 3 * w), 0)
    col = lax.broadcasted_iota(jnp.int32, (w, 3 * w), 1)
    band = (col >= row) & (col <= row + 2 * w)

    def block(n, carry):
        base = pl.multiple_of(n * w, w)
        cos_b = cos_ref[pl.ds(base, w), :]
        sin_b = sin_ref[pl.ds(base, w), :]
        kpos = base - w + col
        valid = band & (kpos >= 0) & (kpos < t_len)
        outs = []
        for g0 in range(0, SWA_KV_HEADS, SWA_GROUPS_PER_STAGE):
            groups = range(g0, g0 + SWA_GROUPS_PER_STAGE)
            scores = []
            for g in groups:
                kw = kp_ref[pl.ds(base, 3 * w), g * SWA_HEADDIM:(g + 1) * SWA_HEADDIM]
                for c in range(g * rep // 2, (g + 1) * rep // 2):
                    qc = q_ref[0, pl.ds(base, w), c * LANES:(c + 1) * LANES].astype(F32)
                    qc = (_rope_lanes(qc, cos_b, sin_b) * (SWA_HEADDIM ** -0.5)).astype(BF16)
                    for hh in range(2):
                        qh = qc[:, hh * SWA_HEADDIM:(hh + 1) * SWA_HEADDIM]
                        scores.append(lax.dot_general(qh, kw, (((1,), (1,)), ((), ())),
                                                      preferred_element_type=F32))
            probs = []
            for k, s in enumerate(scores):
                s = jnp.where(valid, s, NEG)
                sk = sink_ref[g0 * rep + k]
                m = jnp.maximum(jnp.max(s, axis=-1, keepdims=True), sk)
                p = jnp.exp(s - m)
                inv = 1.0 / (jnp.sum(p, axis=-1, keepdims=True) + jnp.exp(sk - m))
                probs.append((p.astype(BF16), inv))
            for k, (p, inv) in enumerate(probs):
                g = g0 + k // rep
                vw = vp_ref[pl.ds(base, 3 * w), g * SWA_HEADDIM:(g + 1) * SWA_HEADDIM]
                outs.append(jnp.dot(p, vw, preferred_element_type=F32) * inv)
        o_ref[0, pl.ds(base, w), :] = jnp.concatenate(outs, axis=-1).astype(o_ref.dtype)
        return carry

    lax.fori_loop(0, t_len // w, block, 0)


def swa_attention(proj, sink, bsz, t_len):
    half = SWA_HEADDIM // 2
    inv_freq = ROPE_THETA ** (-jnp.arange(half, dtype=F32) / half)
    ang = jnp.arange(t_len, dtype=F32)[:, None] * inv_freq[None, :]
    cos, sin = jnp.cos(ang), jnp.sin(ang)
    cos_t = jnp.tile(jnp.concatenate([cos, cos], axis=-1), (1, 2))
    sin_t = jnp.tile(jnp.concatenate([-sin, sin], axis=-1), (1, 2))
    nq = SWA_HEADS * SWA_HEADDIM
    nkv = SWA_KV_HEADS * SWA_HEADDIM
    return pl.pallas_call(
        _swa_body,
        grid=(bsz,),
        in_specs=[
            pl.BlockSpec((1, t_len, nq), lambda b: (b, 0, 0)),
            pl.BlockSpec((1, t_len, nkv), lambda b: (b, 0, nq // nkv)),
            pl.BlockSpec((1, t_len, nkv), lambda b: (b, 0, nq // nkv + 1)),
            pl.BlockSpec((t_len, LANES), lambda b: (0, 0)),
            pl.BlockSpec((t_len, LANES), lambda b: (0, 0)),
            pl.BlockSpec(memory_space=pltpu.SMEM),
        ],
        out_specs=pl.BlockSpec((1, t_len, nq), lambda b: (b, 0, 0)),
        out_shape=jax.ShapeDtypeStruct((bsz, t_len, nq), BF16),
        scratch_shapes=[pltpu.VMEM((t_len + 2 * SWA_WINDOW, nkv), BF16),
                        pltpu.VMEM((t_len + 2 * SWA_WINDOW, nkv), BF16)],
        compiler_params=_cparams("parallel"),
        name="swa_attention",
    )(proj, proj, proj, cos_t, sin_t, sink.astype(F32))


def _na_bias_table(rpb):
    qc = np.arange(GRID_W)[:, None]
    kc = np.arange(GRID_W)[None, :]
    cstart = np.clip(qc - NA_COLS // 2, 0, GRID_W - NA_COLS)
    valid = (kc >= cstart) & (kc < cstart + NA_COLS)
    cidx = np.clip(kc - qc + NA_COLS - 1, 0, 2 * NA_COLS - 2)
    onehot = (np.arange(2 * NA_COLS - 1)[:, None, None] == cidx[None]).astype(np.float32)
    toep = jnp.einsum('hrc,cqk->hrqk', rpb.astype(F32), onehot, precision=lax.Precision.HIGHEST)
    toep = jnp.where(valid[None, None], toep, NEG)
    return jnp.concatenate([toep[:, :-1], toep[:, 1:]], axis=-1)


def _na_body(q_ref, k_ref, v_ref, bias_ref, o_ref):
    t_len = q_ref.shape[1]
    rows = t_len // GRID_W
    win = NA_ROWS * GRID_W

    def row_group(gi, carry):
        first = lax.broadcasted_iota(jnp.int32, (GRID_W, 2 * NA_HEADDIM), 1) < NA_HEADDIM
        chains = []
        for rr in range(NA_ROW_UNROLL):
            r = gi * NA_ROW_UNROLL + rr
            rs = jnp.clip(r - NA_ROWS // 2, 0, rows - NA_ROWS)
            d0 = rs - r + NA_ROWS - 1
            qbase = pl.multiple_of(r * GRID_W, GRID_W)
            kbase = pl.multiple_of(rs * GRID_W, GRID_W)
            qr = q_ref[0, pl.ds(qbase, GRID_W), :]
            kw = k_ref[0, pl.ds(kbase, win), :]
            for hh in range(2):
                q_h = jnp.where(first if hh == 0 else jnp.logical_not(first), qr, jnp.zeros((), BF16))
                s = lax.dot_general(q_h, kw, (((1,), (1,)), ((), ())), preferred_element_type=F32)
                chains.append((hh, d0, kbase, s))
        probs = []
        for hh, d0, kbase, s in chains:
            bias = jnp.concatenate([bias_ref[hh, d0 + 2 * c] for c in range(NA_ROWS // 2)], axis=-1)
            s = s * (NA_HEADDIM ** -0.5) + bias
            m = jnp.max(s, axis=-1, keepdims=True)
            p = jnp.exp(s - m)
            probs.append((p.astype(BF16), 1.0 / jnp.sum(p, axis=-1, keepdims=True)))
        outs = []
        for (hh, d0, kbase, s), (p, inv) in zip(chains, probs):
            vw = v_ref[0, pl.ds(kbase, win), :]
            outs.append(jnp.dot(p, vw, preferred_element_type=F32) * inv)
        for rr in range(NA_ROW_UNROLL):
            qbase = pl.multiple_of((gi * NA_ROW_UNROLL + rr) * GRID_W, GRID_W)
            o_ref[0, pl.ds(qbase, GRID_W), :] = jnp.where(first, outs[2 * rr], outs[2 * rr + 1]).astype(o_ref.dtype)
        return carry

    lax.fori_loop(0, rows // NA_ROW_UNROLL, row_group, 0)


def na_attention(proj, rpb, bsz, t_len):
    bias = _na_bias_table(rpb)
    npair = NA_HEADS // 2
    return pl.pallas_call(
        _na_body,
        grid=(npair, bsz),
        in_specs=[
            pl.BlockSpec((1, t_len, LANES), lambda hp, b: (b, 0, hp)),
            pl.BlockSpec((1, t_len, LANES), lambda hp, b: (b, 0, npair + hp)),
            pl.BlockSpec((1, t_len, LANES), lambda hp, b: (b, 0, 2 * npair + hp)),
            pl.BlockSpec((2, 2 * NA_ROWS - 2, GRID_W, 2 * GRID_W), lambda hp, b: (hp, 0, 0, 0)),
        ],
        out_specs=pl.BlockSpec((1, t_len, LANES), lambda hp, b: (b, 0, hp)),
        out_shape=jax.ShapeDtypeStruct((bsz, t_len, NA_WIDTH), BF16),
        compiler_params=_cparams("parallel", "parallel"),
        name="na_attention",
    )(proj, proj, proj, bias)


def _conv_silu_body(x_ref, w_ref, b_ref, o_ref, *, n_l2):
    t_len, tc = x_ref.shape[1], x_ref.shape[2]
    rc = CONV_ROW_CHUNK
    halo = F32_SUBLANES
    normalise = pl.program_id(1) < n_l2

    def chunk(ci, carry):
        r0 = pl.multiple_of(ci * rc, rc)
        cur = x_ref[0, pl.ds(r0, rc), :].astype(F32)
        lo = pl.multiple_of(jnp.maximum(r0 - halo, 0), halo)
        hi = pl.multiple_of(jnp.minimum(r0 + rc, t_len - halo), halo)
        before = jnp.where(r0 > 0, x_ref[0, pl.ds(lo, halo), :].astype(F32), 0.0)
        after = jnp.where(r0 + rc < t_len, x_ref[0, pl.ds(hi, halo), :].astype(F32), 0.0)
        xx = jnp.concatenate([before, cur, after], axis=0)
        acc = cur * w_ref[CONV_K // 2:CONV_K // 2 + 1, :] + b_ref[...]
        for k in range(CONV_K):
            off = k - CONV_K // 2
            if off == 0:
                continue
            acc = acc + xx[halo + off:halo + off + rc, :] * w_ref[k:k + 1, :]
        y = acc * jax.nn.sigmoid(acc)
        if n_l2 > 0:
            parts = []
            for c in range(tc // LANES):
                yc = y[:, c * LANES:(c + 1) * LANES]
                ss = jnp.sum(yc * yc, axis=-1, keepdims=True)
                parts.append(yc * jnp.where(normalise, lax.rsqrt(ss + NORM_EPS), 1.0))
            y = jnp.concatenate(parts, axis=-1)
        o_ref[0, pl.ds(r0, rc), :] = y.astype(o_ref.dtype)
        return carry

    lax.fori_loop(0, t_len // rc, chunk, 0)


def conv_silu(proj, w, bias, col0, n_ch, n_l2=0, tc=512):
    bsz, t_len, _ = proj.shape
    assert col0 % tc == 0 and n_ch % tc == 0
    c0 = col0 // tc
    return pl.pallas_call(
        functools.partial(_conv_silu_body, n_l2=n_l2),
        grid=(bsz, n_ch // tc),
        in_specs=[
            pl.BlockSpec((1, t_len, tc), lambda b, j: (b, 0, c0 + j)),
            pl.BlockSpec((CONV_K, tc), lambda b, j: (0, j)),
            pl.BlockSpec((1, tc), lambda b, j: (0, j)),
        ],
        out_specs=pl.BlockSpec((1, t_len, tc), lambda b, j: (b, 0, j)),
        out_shape=jax.ShapeDtypeStruct((bsz, t_len, n_ch), BF16),
        compiler_params=_cparams("parallel", "parallel"),
        name="conv_silu",
    )(proj, w.astype(F32), bias.astype(F32).reshape(1, n_ch))


def _softplus(x):
    return jnp.maximum(x, 0.0) + jnp.log1p(jnp.exp(-jnp.abs(x)))


def _bmm(a, b, precision=None):
    return lax.dot_general(a, b, (((2,), (1,)), ((0,), (0,))), precision=precision,
                           preferred_element_type=F32)


def _split_bf16(a):
    hi = a.astype(BF16)
    return hi, (a - hi.astype(F32)).astype(BF16)


def _split3_bf16(a):
    hi = a.astype(BF16)
    rest = a - hi.astype(F32)
    mid = rest.astype(BF16)
    return hi, mid, (rest - mid.astype(F32)).astype(BF16)


def _bmm_split(a, b):
    return _bmm(a[0], b[0]) + _bmm(a[0], b[1]) + _bmm(a[1], b[0])


def _bmm_tn(a, b):
    return lax.dot_general(a, b, (((1,), (1,)), ((0,), (0,))), preferred_element_type=F32)


def _bmm_nt(a, b):
    return lax.dot_general(a, b, (((2,), (2,)), ((0,), (0,))), preferred_element_type=F32)


def _gdn_body(q_ref, k_ref, v_ref, z_ref, gcol_ref, grow_ref, alog_ref, dtb_ref, nw_ref, o_ref,
              m_s, b_s, q_s, o0_s, el_s, o_s):
    c_len = GDN_CHUNK
    t_len = q_ref.shape[1]
    nc = t_len // c_len
    h = pl.program_id(1)
    ii = lax.broadcasted_iota(jnp.int32, (c_len, c_len), 0)
    jj = lax.broadcasted_iota(jnp.int32, (c_len, c_len), 1)
    eye = (ii == jj).astype(F32)
    cg = GDN_PHASE_A_CHUNKS

    def phase_a(gi, carry):
        c0 = pl.multiple_of(gi * cg, cg)
        rows = pl.ds(pl.multiple_of(gi * (cg * c_len), cg * c_len), cg * c_len)
        k3 = k_ref[0, rows, :].reshape(cg, c_len, GDN_DK)
        v3 = v_ref[0, rows, :].reshape(cg, c_len, GDN_DK).astype(F32)
        kf = k3.astype(F32)
        qs = q_ref[0, rows, :].reshape(cg, c_len, GDN_DK).astype(F32) * (GDN_DK ** -0.5)
        gcol = gcol_ref[0, 0, pl.ds(c0, cg)]
        grow = grow_ref[0, 0, pl.ds(c0, cg)]
        qk_raw = _bmm_nt(qs.astype(BF16), k3)
        kb, gc_c, g_last, decay, strict = [], [], [], [], []
        for d in range(2):
            incl = (ii >= jj) if d == 0 else (ii <= jj)
            strict.append((ii > jj) if d == 0 else (ii < jj))
            tri = incl.astype(F32)
            tri_t = ((ii <= jj) if d == 0 else (ii >= jj)).astype(F32)
            neg_a = -jnp.exp(jnp.full((1, 1, 1), alog_ref[d, h], F32))
            dtb = dtb_ref[d, h]
            g_c = neg_a * _softplus(gcol[:, :, d:d + 1] + dtb)
            g_r = neg_a * _softplus(grow[:, d:d + 1, :] + dtb)
            beta_c = jax.nn.sigmoid(gcol[:, :, 2 + d:3 + d])
            gc_c.append(jnp.sum(tri[None] * g_r, axis=2, keepdims=True))
            gc_r = jnp.sum(tri_t[None] * g_c, axis=1, keepdims=True)
            g_last.append(jnp.sum(g_r, axis=2, keepdims=True))
            decay.append(jnp.where(incl[None], jnp.exp(jnp.where(incl[None], gc_c[d] - gc_r, 0.0)), 0.0))
            kb.append((kf * beta_c, v3 * beta_c))
        kk = [_bmm_nt(kb[d][0].astype(BF16), k3) for d in range(2)]
        low = [jnp.where(strict[d][None], kk[d] * decay[d], 0.0) for d in range(2)]
        inv = [eye[None] - low[d] for d in range(2)]
        pw = [_split_bf16(low[d]) for d in range(2)]
        for _ in range(5):
            pw = [_split_bf16(_bmm_split(pw[d], pw[d])) for d in range(2)]
            inv = [inv[d] + _bmm_split(_split_bf16(inv[d]), pw[d]) for d in range(2)]
        rhs = [jnp.concatenate([kb[d][1], kb[d][0] * jnp.exp(gc_c[d])], axis=-1) for d in range(2)]
        sol = [_bmm_split(_split_bf16(inv[d]), _split_bf16(rhs[d])) for d in range(2)]
        for d in range(2):
            ub = sol[d][:, :, :GDN_DK].astype(BF16)
            w = sol[d][:, :, GDN_DK:].astype(BF16)
            qk = (qk_raw * decay[d]).astype(BF16)
            kg = (kf * jnp.exp(g_last[d] - gc_c[d])).astype(BF16)
            m_s[d, pl.ds(c0, cg)] = _bmm_tn(kg, w).astype(BF16)
            b_s[d, pl.ds(c0, cg)] = _bmm_tn(kg, ub)
            q_s[d, pl.ds(c0, cg)] = (qs * jnp.exp(gc_c[d]) - _bmm(qk, w)).astype(BF16)
            o0_s[d, pl.ds(c0, cg)] = _bmm(qk, ub)
            el_s[d, pl.ds(c0, cg)] = jnp.broadcast_to(jnp.exp(g_last[d]), (cg, 1, GDN_DK))
        return carry

    lax.fori_loop(0, nc // cg, phase_a, 0)

    def step(t, carry):
        cs = (t, nc - 1 - t)
        sb = [carry[d].astype(BF16) for d in range(2)]
        ms = [jnp.dot(m_s[d, cs[d]], sb[d], preferred_element_type=F32) for d in range(2)]
        os_ = [jnp.dot(q_s[d, cs[d]], sb[d], preferred_element_type=F32) for d in range(2)]
        for d in range(2):
            o_s[d, cs[d]] = os_[d] + o0_s[d, cs[d]]
        return tuple(carry[d] * el_s[d, cs[d]] - ms[d] + b_s[d, cs[d]] for d in range(2))

    s0 = jnp.zeros((GDN_DK, GDN_DK), F32)
    lax.fori_loop(0, nc, step, (s0, s0))

    o = (o_s[0] + o_s[1]).reshape(t_len, GDN_DK)
    ms = jnp.mean(o * o, axis=-1, keepdims=True)
    z = z_ref[0]
    y = o * lax.rsqrt(ms + NORM_EPS) * nw_ref[...] * (z * jax.nn.sigmoid(z))
    o_ref[0] = y.astype(o_ref.dtype)


def gdn_scan(qkv, proj, gates, a_log, dt_bias, norm_w, bsz, t_len):
    nc = t_len // GDN_CHUNK
    nh = GDN_HEADS
    g4 = gates[:, :4 * nh].reshape(bsz, nc, GDN_CHUNK, 4, nh)
    gcol = jnp.transpose(g4, (0, 4, 1, 2, 3))
    grow = jnp.transpose(g4, (0, 4, 1, 3, 2))
    dk = GDN_DK
    nq = GDN_QK // dk
    return pl.pallas_call(
        _gdn_body,
        grid=(bsz, nh),
        in_specs=[
            pl.BlockSpec((1, t_len, dk), lambda b, h: (b, 0, h)),
            pl.BlockSpec((1, t_len, dk), lambda b, h: (b, 0, nq + h)),
            pl.BlockSpec((1, t_len, dk), lambda b, h: (b, 0, 2 * nq + h)),
            pl.BlockSpec((1, t_len, dk), lambda b, h: (b, 0, 3 * nq + h)),
            pl.BlockSpec((1, 1, nc, GDN_CHUNK, 4), lambda b, h: (b, h, 0, 0, 0)),
            pl.BlockSpec((1, 1, nc, 4, GDN_CHUNK), lambda b, h: (b, h, 0, 0, 0)),
            pl.BlockSpec(memory_space=pltpu.SMEM),
            pl.BlockSpec(memory_space=pltpu.SMEM),
            pl.BlockSpec((1, dk), lambda b, h: (0, 0)),
        ],
        out_specs=pl.BlockSpec((1, t_len, dk), lambda b, h: (b, 0, h)),
        out_shape=jax.ShapeDtypeStruct((bsz, t_len, GDN_V), BF16),
        scratch_shapes=[
            pltpu.VMEM((2, nc, dk, dk), BF16),
            pltpu.VMEM((2, nc, dk, dk), F32),
            pltpu.VMEM((2, nc, GDN_CHUNK, dk), BF16),
            pltpu.VMEM((2, nc, GDN_CHUNK, dk), F32),
            pltpu.VMEM((2, nc, 1, dk), F32),
            pltpu.VMEM((2, nc, GDN_CHUNK, dk), F32),
        ],
        compiler_params=_cparams("parallel", "parallel"),
        name="gdn_scan",
    )(qkv, qkv, qkv, proj, gcol, grow, a_log.astype(F32), dt_bias.astype(F32),
      norm_w.astype(F32).reshape(1, dk))


def _pad_cols(w, n):
    return jnp.pad(w, ((0, 0), (0, n - w.shape[1])))


def gdn_layer(x, gain, w_in, conv_w, a_log, dt_bias, norm_w, w_out, bsz, t_len):
    n_main = GDN_CONV_CH + GDN_V
    proj = norm_matmul(x, gain, w_in[:, :n_main].astype(BF16), F32)
    gates = norm_matmul(x, gain, _pad_cols(w_in[:, n_main:], LANES).astype(BF16), F32)
    proj3 = proj.reshape(bsz, t_len, n_main)
    qkv = conv_silu(proj3, conv_w, jnp.zeros((GDN_CONV_CH,), F32), 0, GDN_CONV_CH,
                    n_l2=2 * GDN_QK // 512)
    y = gdn_scan(qkv, proj3, gates, a_log, dt_bias, norm_w, bsz, t_len)
    return matmul_residual(x, y.reshape(bsz * t_len, GDN_V), w_out.astype(BF16))


def _ssd_body(xs_ref, b_ref, c_ref, z_ref, dcol_ref, drow_ref, alogc_ref, alogr_ref, dtbc_ref, dtbr_ref, dskip_ref,
              nw_ref, o_ref, xw_s, ea_s, el_s, y_s, st_s):
    assert SSD_CHUNK == SSD_HEADDIM
    c_len = SSD_CHUNK
    t_len = xs_ref.shape[1]
    nc = t_len // c_len
    nr = SSD_HEADS // SSD_GROUPS
    hp = SSD_HEADDIM
    g = pl.program_id(1)
    ii = lax.broadcasted_iota(jnp.int32, (c_len, c_len), 0)
    jj = lax.broadcasted_iota(jnp.int32, (c_len, c_len), 1)
    cg = SSD_PHASE_A_CHUNKS

    def phase_a(gi, carry):
        c0 = pl.multiple_of(gi * cg, cg)
        rows = pl.ds(pl.multiple_of(gi * (cg * c_len), cg * c_len), cg * c_len)
        x3 = xs_ref[0, rows, :].reshape(cg, c_len, nr * hp)
        b3 = b_ref[0, rows, :].reshape(cg, c_len, SSD_STATE)
        c3 = c_ref[0, rows, :].reshape(cg, c_len, SSD_STATE)
        dcol = dcol_ref[0, 0, pl.ds(c0, cg)]
        drow = drow_ref[0, 0, pl.ds(c0, cg)]
        cb = _bmm_nt(c3, b3)
        dt_c = _softplus(dcol + dtbc_ref[0])
        dt_r = _softplus(drow + dtbr_ref[0])
        da_c = dt_c * -jnp.exp(alogc_ref[0])
        da_r = dt_r * -jnp.exp(alogr_ref[0])
        tri_f = jnp.broadcast_to((ii >= jj).astype(BF16)[None], (cg, c_len, c_len))
        tri_b = jnp.broadcast_to((ii <= jj).astype(BF16)[None], (cg, c_len, c_len))
        fwd_c = lax.broadcasted_iota(jnp.int32, (1, 1, 2 * nr), 2) < nr
        fwd_r = lax.broadcasted_iota(jnp.int32, (1, 2 * nr, 1), 1) < nr
        da_c3, da_r3 = _split3_bf16(da_c), _split3_bf16(da_r)
        ac_c = jnp.where(fwd_c, sum(_bmm(tri_f, p) for p in da_c3), sum(_bmm(tri_b, p) for p in da_c3))
        ac_r = jnp.where(fwd_r, sum(_bmm(p, tri_b) for p in da_r3), sum(_bmm(p, tri_f) for p in da_r3))
        ydiag = None
        for d in range(2):
            incl = (ii >= jj) if d == 0 else (ii <= jj)
            sel = (lax.broadcasted_iota(jnp.int32, (2 * nr, nr * hp), 1) // hp + d * nr
                   == lax.broadcasted_iota(jnp.int32, (2 * nr, nr * hp), 0)).astype(BF16)

            def spread(cols, pieces):
                parts = pieces(cols.reshape(cg * c_len, 2 * nr))
                return sum(jnp.dot(p, sel, preferred_element_type=F32) for p in parts).reshape(cg, c_len, nr * hp)

            dt_w = spread(dt_c, _split_bf16)
            ac_w = spread(ac_c, _split3_bf16)
            last = ac_w[:, c_len - 1:c_len, :] if d == 0 else ac_w[:, 0:1, :]
            xr = x3.astype(F32) * dt_w
            xw_s[d, pl.ds(c0, cg)] = (xr * jnp.exp(last - ac_w)).astype(BF16)
            ea_s[d, pl.ds(c0, cg)] = jnp.exp(ac_w)
            el_s[d, pl.ds(c0, cg)] = jnp.exp(last)
            xr = xr.astype(BF16)
            yd = []
            for r in range(nr):
                col = d * nr + r
                diff = ac_w[:, :, r * hp:r * hp + c_len] - ac_r[:, col:col + 1, :]
                seg = jnp.where(incl[None], jnp.exp(jnp.where(incl[None], diff, 0.0)), 0.0)
                yd.append(_bmm((cb * seg).astype(BF16), xr[:, :, r * hp:(r + 1) * hp]))
            yd = jnp.concatenate(yd, axis=-1)
            ydiag = yd if ydiag is None else ydiag + yd
        y_s[pl.ds(c0, cg)] = ydiag
        return carry

    lax.fori_loop(0, nc // cg, phase_a, 0)

    st_s[...] = jnp.zeros_like(st_s)

    def step(t, carry):
        for d in range(2):
            c = t if d == 0 else nc - 1 - t
            base = pl.multiple_of(c * c_len, c_len)
            cc = c_ref[0, pl.ds(base, c_len), :]
            bb = b_ref[0, pl.ds(base, c_len), :]
            st = st_s[d]
            y_s[c] += jnp.dot(cc, st.astype(BF16), preferred_element_type=F32) * ea_s[d, c]
            st_s[d] = st * el_s[d, c] + lax.dot_general(bb, xw_s[d, c], (((0,), (0,)), ((), ())),
                                                        preferred_element_type=F32)
        return carry

    lax.fori_loop(0, nc, step, 0)

    xs = xs_ref[0].astype(F32)
    z = z_ref[0]
    y = (y_s[...].reshape(t_len, nr * hp) + dskip_ref[...] * xs) * (z * jax.nn.sigmoid(z))
    ms = jnp.mean(y * y, axis=-1, keepdims=True)
    o_ref[0] = (y * lax.rsqrt(ms + NORM_EPS) * nw_ref[...]).astype(o_ref.dtype)


def ssd_scan(xbc, proj, dts, a_log, dt_bias, d_skip, norm_w, bsz, t_len):
    nc = t_len // SSD_CHUNK
    ng = SSD_GROUPS
    nr = SSD_HEADS // ng
    gw = nr * SSD_HEADDIM
    d6 = dts[:, :2 * SSD_HEADS].reshape(bsz, nc, SSD_CHUNK, 2, ng, nr)
    dcol = jnp.transpose(d6, (0, 4, 1, 2, 3, 5)).reshape(bsz, ng, nc, SSD_CHUNK, 2 * nr)
    drow = jnp.transpose(d6, (0, 4, 1, 3, 5, 2)).reshape(bsz, ng, nc, 2 * nr, SSD_CHUNK)
    nb0 = SSD_INNER // SSD_STATE
    dtb_g = jnp.transpose(dt_bias.astype(F32).reshape(2, ng, nr), (1, 0, 2)).reshape(ng, 2 * nr)
    alog_g = jnp.transpose(a_log.astype(F32).reshape(2, ng, nr), (1, 0, 2)).reshape(ng, 2 * nr)
    return pl.pallas_call(
        _ssd_body,
        grid=(bsz, ng),
        in_specs=[
            pl.BlockSpec((1, t_len, gw), lambda b, g: (b, 0, g)),
            pl.BlockSpec((1, t_len, SSD_STATE), lambda b, g: (b, 0, nb0 + g)),
            pl.BlockSpec((1, t_len, SSD_STATE), lambda b, g: (b, 0, nb0 + ng + g)),
            pl.BlockSpec((1, t_len, gw), lambda b, g: (b, 0, g)),
            pl.BlockSpec((1, 1, nc, SSD_CHUNK, 2 * nr), lambda b, g: (b, g, 0, 0, 0)),
            pl.BlockSpec((1, 1, nc, 2 * nr, SSD_CHUNK), lambda b, g: (b, g, 0, 0, 0)),
            pl.BlockSpec((1, 1, 2 * nr), lambda b, g: (g, 0, 0)),
            pl.BlockSpec((1, 2 * nr, 1), lambda b, g: (g, 0, 0)),
            pl.BlockSpec((1, 1, 2 * nr), lambda b, g: (g, 0, 0)),
            pl.BlockSpec((1, 2 * nr, 1), lambda b, g: (g, 0, 0)),
            pl.BlockSpec((1, gw), lambda b, g: (0, g)),
            pl.BlockSpec((1, gw), lambda b, g: (0, g)),
        ],
        out_specs=pl.BlockSpec((1, t_len, gw), lambda b, g: (b, 0, g)),
        out_shape=jax.ShapeDtypeStruct((bsz, t_len, SSD_INNER), BF16),
        scratch_shapes=[
            pltpu.VMEM((2, nc, SSD_CHUNK, gw), BF16),
            pltpu.VMEM((2, nc, SSD_CHUNK, gw), F32),
            pltpu.VMEM((2, nc, 1, gw), F32),
            pltpu.VMEM((nc, SSD_CHUNK, gw), F32),
            pltpu.VMEM((2, SSD_STATE, gw), F32),
        ],
        compiler_params=_cparams("parallel", "parallel"),
        name="ssd_scan",
    )(xbc, xbc, xbc, proj, dcol, drow, alog_g.reshape(ng, 1, 2 * nr), alog_g.reshape(ng, 2 * nr, 1),
      dtb_g.reshape(ng, 1, 2 * nr), dtb_g.reshape(ng, 2 * nr, 1),
      jnp.repeat(d_skip.astype(F32), SSD_HEADDIM).reshape(1, SSD_INNER),
      norm_w.astype(F32).reshape(1, SSD_INNER))


def ssd_layer(x, gain, w_in, conv_w, conv_b, a_log, dt_bias, d_skip, norm_w, w_out, bsz, t_len):
    n_main = SSD_INNER + SSD_CONV_CH
    proj = norm_matmul(x, gain, w_in[:, :n_main].astype(BF16), F32)
    dts = norm_matmul(x, gain, _pad_cols(w_in[:, n_main:], LANES).astype(BF16), F32)
    proj3 = proj.reshape(bsz, t_len, n_main)
    xbc = conv_silu(proj3, conv_w, conv_b, SSD_INNER, SSD_CONV_CH)
    y = ssd_scan(xbc, proj3, dts, a_log, dt_bias, d_skip, norm_w, bsz, t_len)
    return matmul_residual(x, y.reshape(bsz * t_len, SSD_INNER), w_out.astype(BF16))


def swa_layer(x, gain, w_in, sink, w_out, bsz, t_len):
    proj = norm_matmul(x, gain, w_in.astype(BF16), BF16)
    o = swa_attention(proj.reshape(bsz, t_len, -1), sink, bsz, t_len)
    return matmul_residual(x, o.reshape(bsz * t_len, -1), w_out.astype(BF16))


def na_layer(x, gain, w_in, rpb, w_out, bsz, t_len):
    proj = norm_matmul(x, gain, w_in.astype(BF16), BF16)
    o = na_attention(proj.reshape(bsz, t_len, -1), rpb, bsz, t_len)
    return matmul_residual(x, o.reshape(bsz * t_len, -1), w_out.astype(BF16))


def kernel(x, norm_mix, norm_ffn, norm_final, gdn_w_in, gdn_conv, gdn_a_log, gdn_dt_bias, gdn_norm, gdn_w_out, ssd_w_in, ssd_conv, ssd_conv_b, ssd_a_log, ssd_dt_bias, ssd_d, ssd_norm, ssd_w_out, swa_w_in, swa_sink, swa_w_out, na_w_in, na_rpb, na_w_out, peer_w_q, peer_keys, peer_u, peer_v):
    bsz, t_len, d = x.shape
    depth = norm_mix.shape[0]
    xf = x.reshape(bsz * t_len, d)
    for i in range(depth):
        mixer, j = i % 4, i // 4
        if mixer == 0:
            xf = gdn_layer(xf, norm_mix[i], gdn_w_in[j], gdn_conv[j], gdn_a_log[j], gdn_dt_bias[j],
                           gdn_norm[j], gdn_w_out[j], bsz, t_len)
        elif mixer == 1:
            xf = ssd_layer(xf, norm_mix[i], ssd_w_in[j], ssd_conv[j], ssd_conv_b[j], ssd_a_log[j],
                           ssd_dt_bias[j], ssd_d[j], ssd_norm[j], ssd_w_out[j], bsz, t_len)
        elif mixer == 2:
            xf = swa_layer(xf, norm_mix[i], swa_w_in[j], swa_sink[j], swa_w_out[j], bsz, t_len)
        else:
            xf = na_layer(xf, norm_mix[i], na_w_in[j], na_rpb[j], na_w_out[j], bsz, t_len)
        xf = peer_layer(xf, norm_ffn[i], peer_w_q[i], peer_keys[i], peer_u[i], peer_v[i])
    return final_norm(xf, norm_final).reshape(bsz, t_len, d)
```

```python
import functools
import math

import jax
import jax.numpy as jnp
import numpy as np
from jax import lax
from jax.experimental import pallas as pl
from jax.experimental.pallas import tpu as pltpu

F32 = jnp.float32
BF16 = jnp.bfloat16
NEG = -1e30

D_MODEL = 1024
SEQ = 2048
GRID_W = 64
CONV_K = 5
NORM_EPS = 1e-6

GDN_HEADS = 8
GDN_DK = 128
GDN_CHUNK = 64
GDN_QK = 1024
GDN_V = 1024
GDN_CONV_CH = 3072

SSD_INNER = 2048
SSD_HEADDIM = 64
SSD_HEADS = 32
SSD_GROUPS = 4
SSD_STATE = 128
SSD_CHUNK = 64
SSD_BC = 512
SSD_CONV_CH = 3072

SWA_HEADS = 16
SWA_KV_HEADS = 4
SWA_HEADDIM = 64
SWA_WINDOW = 128
ROPE_THETA = 10000.0

NA_HEADS = 16
NA_HEADDIM = 64
NA_ROWS = 8
NA_COLS = 16
NA_WIDTH = 1024

PEER_HEADS = 8
PEER_NKEYS = 128
PEER_QDIM = 256
PEER_HALF = 128
PEER_TOPK = 16

GDN_PHASE_A_CHUNKS = 8
SSD_PHASE_A_CHUNKS = 8

PEER_EXPERT_ROWS = 8
PEER_Z_CHUNKS = 4

SWA_GROUPS_PER_STAGE = 2

NA_ROW_UNROLL = 4

LANES = 128
BF16_SUBLANES = 16
F32_SUBLANES = 8

CONV_ROW_CHUNK = 256

VMEM_LIMIT_BYTES = 52 * 1024 * 1024


def _cparams(*sem, flags=None):
    return pltpu.CompilerParams(dimension_semantics=sem, vmem_limit_bytes=VMEM_LIMIT_BYTES, flags=flags)


def _norm_matmul_body(x_ref, g_ref, w_ref, o_ref, hn_ref):
    @pl.when(pl.program_id(1) == 0)
    def _():
        x = x_ref[...]
        ms = jnp.mean(x * x, axis=-1, keepdims=True)
        hn_ref[...] = (x * lax.rsqrt(ms + NORM_EPS) * g_ref[...]).astype(BF16)

    o_ref[...] = jnp.dot(hn_ref[...], w_ref[...], preferred_element_type=F32).astype(o_ref.dtype)


def norm_matmul(x, gain, w, out_dtype, tm=1024, tn=1024):
    m, d = x.shape
    n = w.shape[1]
    tn = min(tn, n)
    while n % tn:
        tn //= 2
    assert m % tm == 0 and tn % LANES == 0
    return pl.pallas_call(
        _norm_matmul_body,
        grid=(m // tm, n // tn),
        in_specs=[
            pl.BlockSpec((tm, d), lambda i, j: (i, 0)),
            pl.BlockSpec((1, d), lambda i, j: (0, 0)),
            pl.BlockSpec((d, tn), lambda i, j: (0, j)),
        ],
        out_specs=pl.BlockSpec((tm, tn), lambda i, j: (i, j)),
        out_shape=jax.ShapeDtypeStruct((m, n), out_dtype),
        scratch_shapes=[pltpu.VMEM((tm, d), BF16)],
        compiler_params=_cparams("parallel", "arbitrary"),
        name="norm_matmul",
    )(x, gain.reshape(1, d), w)


def _matmul_residual_body(x_ref, y_ref, w_ref, o_ref):
    o_ref[...] = x_ref[...] + jnp.dot(y_ref[...], w_ref[...], preferred_element_type=F32)


def matmul_residual(x, y, w, tm=1024):
    m, d = x.shape
    k = y.shape[1]
    return pl.pallas_call(
        _matmul_residual_body,
        grid=(m // tm,),
        in_specs=[
            pl.BlockSpec((tm, d), lambda i: (i, 0)),
            pl.BlockSpec((tm, k), lambda i: (i, 0)),
            pl.BlockSpec((k, d), lambda i: (0, 0)),
        ],
        out_specs=pl.BlockSpec((tm, d), lambda i: (i, 0)),
        out_shape=jax.ShapeDtypeStruct((m, d), F32),
        compiler_params=_cparams("parallel"),
        name="matmul_residual",
    )(x, y, w)


def _final_norm_body(x_ref, g_ref, o_ref):
    x = x_ref[...]
    ms = jnp.mean(x * x, axis=-1, keepdims=True)
    o_ref[...] = x * lax.rsqrt(ms + NORM_EPS) * g_ref[...]


def final_norm(x, gain, tm=1024):
    m, d = x.shape
    return pl.pallas_call(
        _final_norm_body,
        grid=(m // tm,),
        in_specs=[pl.BlockSpec((tm, d), lambda i: (i, 0)), pl.BlockSpec((1, d), lambda i: (0, 0))],
        out_specs=pl.BlockSpec((tm, d), lambda i: (i, 0)),
        out_shape=jax.ShapeDtypeStruct((m, d), F32),
        compiler_params=_cparams("parallel"),
        name="final_norm",
    )(x, gain.reshape(1, d))


def _top16_desc(s, with_rank):
    work = s
    rank = jnp.full(s.shape, 99.0, F32) if with_rank else None
    vals = []
    for r in range(PEER_TOPK):
        m = jnp.max(work, axis=0, keepdims=True)
        eq = work == m
        if with_rank:
            rank = jnp.where(eq, float(r + 1), rank)
        work = jnp.where(eq, -jnp.inf, work)
        vals.append(m)
    return jnp.concatenate(vals, axis=0), rank


def _peer_route_body(x_ref, g_ref, wqT_ref, keys_ref, hnT_ref, n_ref, e0_ref, r1_ref, e1_ref):
    x = x_ref[...]
    ms = jnp.mean(x * x, axis=-1, keepdims=True)
    hn = x * lax.rsqrt(ms + NORM_EPS) * g_ref[...]
    hnT = hn.T.astype(BF16)
    hnT_ref[...] = hnT
    qT = jnp.dot(wqT_ref[...], hnT, preferred_element_type=F32)
    row8 = lax.broadcasted_iota(jnp.int32, (8, x.shape[0]), 0)
    for h in range(PEER_HEADS):
        s = []
        for p in range(2):
            hp = 2 * h + p
            q_hp = qT[hp * PEER_HALF:(hp + 1) * PEER_HALF, :].astype(BF16)
            s.append(jnp.dot(keys_ref[hp], q_hp, preferred_element_type=F32))
        a0, _ = _top16_desc(s[0], False)
        b, rank1 = _top16_desc(s[1], True)
        half = PEER_TOPK // 2
        cols = [a0 + b[0:1, :], a0[0:half, :] + b[1:2, :]]
        for q in range(2, half):
            cols.append(jnp.where(row8 < PEER_TOPK // (q + 1), a0[0:half, :] + b[q:q + 1, :], -jnp.inf))
        tail = a0[0:1, :] + b[half:, :]
        best, _ = _top16_desc(jnp.concatenate(cols + [tail], axis=0), False)
        tau = best[PEER_TOPK - 1:PEER_TOPK, :]
        z = jnp.sum(jnp.exp(best - best[0:1, :]), axis=0, keepdims=True)
        n_lo = sum(jnp.where(c[0:half, :] >= tau, 1.0, 0.0) for c in cols)
        n_tail = jnp.sum(jnp.where(tail >= tau, 1.0, 0.0), axis=0, keepdims=True)
        n_lo = n_lo + jnp.where(row8 == 0, n_tail, 0.0)
        n_sel = jnp.concatenate([n_lo, jnp.where(cols[0][half:, :] >= tau, 1.0, 0.0)], axis=0)
        n_map = jnp.zeros_like(s[0])
        for p in range(PEER_TOPK):
            n_map = jnp.where(s[0] == a0[p:p + 1, :], n_sel[p:p + 1, :], n_map)
        n_ref[h] = n_map
        e0_ref[h] = jnp.exp(s[0] - a0[0:1, :]) * (0.5 / z)
        r1_ref[h] = rank1.astype(BF16)
        e1_ref[h] = jnp.exp(s[1] - b[0:1, :]).astype(BF16)


def peer_route(x, gain, wqT, keys, tm=256):
    m, d = x.shape
    nq = wqT.shape[0]
    tab = jax.ShapeDtypeStruct((PEER_HEADS, PEER_NKEYS, m), F32)
    tab16 = jax.ShapeDtypeStruct((PEER_HEADS, PEER_NKEYS, m), BF16)
    tab_spec = pl.BlockSpec((PEER_HEADS, PEER_NKEYS, tm), lambda i: (0, 0, i))
    return pl.pallas_call(
        _peer_route_body,
        grid=(m // tm,),
        in_specs=[
            pl.BlockSpec((tm, d), lambda i: (i, 0)),
            pl.BlockSpec((1, d), lambda i: (0, 0)),
            pl.BlockSpec((nq, d), lambda i: (0, 0)),
            pl.BlockSpec(keys.shape, lambda i: (0, 0, 0)),
        ],
        out_specs=[pl.BlockSpec((d, tm), lambda i: (0, i)), tab_spec, tab_spec, tab_spec, tab_spec],
        out_shape=[jax.ShapeDtypeStruct((d, m), BF16), tab, tab, tab16, tab16],
        compiler_params=_cparams("parallel"),
        name="peer_route",
    )(x, gain.reshape(1, d), wqT, keys)


def _peer_expert_body(x_ref, hnT_ref, n_ref, e0_ref, r1_ref, e1_ref, u_ref, vT_ref, o_ref,
                      a_s, acc_ref, *, ni, nb):
    j = pl.program_id(1)

    @pl.when(j == 0)
    def _():
        a_s[...] = jnp.zeros_like(a_s)
        acc_ref[...] = jnp.zeros_like(acc_ref)

    tm = a_s.shape[2]
    pk = BF16_SUBLANES
    nk = PEER_NKEYS
    cur = j % 2
    prev = 1 - cur

    ib = jnp.minimum(j, nb - 1)
    zc = ni * nk // PEER_Z_CHUNKS
    z_chunks = [jnp.dot(u_ref[pl.ds(c * zc, zc), :], hnT_ref[...], preferred_element_type=F32)
                for c in range(PEER_Z_CHUNKS)]
    acc_ref[...] += jnp.dot(vT_ref[0], a_s[prev], preferred_element_type=F32)
    for ii in range(ni):
        i = ib * ni + ii
        r0 = ii * nk - (ii * nk // zc) * zc
        z = z_chunks[ii * nk // zc][r0:r0 + nk, :].astype(BF16)
        act = z * (1.0 + lax.erf(z * (1.0 / math.sqrt(2.0))))
        gate = None
        for h in range(PEER_HEADS):
            n_b = jnp.broadcast_to(n_ref[h, pl.ds(i, 1), :], (pk, tm)).astype(BF16)[None]
            e0_b = jnp.broadcast_to(e0_ref[h, pl.ds(i, 1), :], (pk, tm)).astype(BF16)[None]
            r1 = r1_ref[h].reshape(nk // pk, pk, tm)
            e1 = e1_ref[h].reshape(nk // pk, pk, tm)
            term = jnp.where(r1 <= n_b, e1 * e0_b, jnp.zeros((), BF16))
            gate = term if gate is None else gate + term
        a_s[cur, pl.ds(ii * nk, nk), :] = (act.reshape(nk // pk, pk, tm) * gate).reshape(nk, tm)

    @pl.when(j == nb)
    def _():
        o_ref[...] = x_ref[...] + acc_ref[...].T


def peer_experts(x, hnT, n_tab, e0_tab, r1_tab, e1_tab, u, v_tab, tm=512, ni=PEER_EXPERT_ROWS):
    m, d = x.shape
    ne = u.shape[0]
    et = ni * PEER_NKEYS
    nb = ne // et
    vT = jnp.transpose(v_tab.astype(BF16).reshape(nb, et, d), (0, 2, 1))
    tab_spec = pl.BlockSpec((PEER_HEADS, PEER_NKEYS, tm), lambda i, j: (0, 0, i))
    return pl.pallas_call(
        functools.partial(_peer_expert_body, ni=ni, nb=nb),
        grid=(m // tm, nb + 1),
        in_specs=[
            pl.BlockSpec((tm, d), lambda i, j: (i, 0)),
            pl.BlockSpec((d, tm), lambda i, j: (0, i)),
            tab_spec, tab_spec, tab_spec, tab_spec,
            pl.BlockSpec((et, d), lambda i, j: (jnp.minimum(j, nb - 1), 0)),
            pl.BlockSpec((1, d, et), lambda i, j: (jnp.clip(j - 1, 0, nb - 1), 0, 0)),
        ],
        out_specs=pl.BlockSpec((tm, d), lambda i, j: (i, 0)),
        out_shape=jax.ShapeDtypeStruct((m, d), F32),
        scratch_shapes=[pltpu.VMEM((2, et, tm), BF16), pltpu.VMEM((d, tm), F32)],
        compiler_params=_cparams("parallel", "arbitrary"),
        name="peer_experts",
    )(x, hnT, n_tab, e0_tab, r1_tab, e1_tab, u, vT)


def peer_layer(x, gain, w_q, keys, u_tab, v_tab):
    wqT = w_q.T.astype(BF16)
    keys2 = keys.reshape(PEER_HEADS * 2, PEER_NKEYS, PEER_HALF).astype(BF16)
    hnT, n_tab, e0_tab, r1_tab, e1_tab = peer_route(x, gain, wqT, keys2)
    return peer_experts(x, hnT, n_tab, e0_tab, r1_tab, e1_tab, u_tab.astype(BF16), v_tab)


def _rope_lanes(t, cos, sin_signed):
    half = SWA_HEADDIM // 2
    lane = lax.broadcasted_iota(jnp.int32, t.shape, 1)
    first = (lane % SWA_HEADDIM) < half
    partner = jnp.where(first, pltpu.roll(t, LANES - half, 1), pltpu.roll(t, half, 1))
    return t * cos + partner * sin_signed


def _swa_body(q_ref, k_ref, v_ref, cos_ref, sin_ref, sink_ref, o_ref, kp_ref, vp_ref):
    w = SWA_WINDOW
    t_len = q_ref.shape[1]
    nkv = SWA_KV_HEADS * SWA_HEADDIM
    rep = SWA_HEADS // SWA_KV_HEADS
    zeros = jnp.zeros((w, nkv), BF16)
    kp_ref[pl.ds(0, w), :] = zeros
    kp_ref[pl.ds(w + t_len, w), :] = zeros
    vp_ref[pl.ds(0, w), :] = zeros
    vp_ref[pl.ds(w + t_len, w), :] = zeros
    vp_ref[pl.ds(w, t_len), :] = v_ref[0]
    cos_all = cos_ref[...]
    sin_all = sin_ref[...]
    for c in range(nkv // LANES):
        kc = k_ref[0, :, c * LANES:(c + 1) * LANES].astype(F32)
        kp_ref[pl.ds(w, t_len), c * LANES:(c + 1) * LANES] = _rope_lanes(kc, cos_all, sin_all).astype(BF16)

    row = lax.broadcasted_iota(jnp.int32, (w, 3 * w), 0)
    col = lax.broadcasted_iota(jnp.int32, (w, 3 * w), 1)
    band = (col >= row) & (col <= row + 2 * w)

    def block(n, carry):
        base = pl.multiple_of(n * w, w)
        cos_b = cos_ref[pl.ds(base, w), :]
        sin_b = sin_ref[pl.ds(base, w), :]
        kpos = base - w + col
        valid = band & (kpos >= 0) & (kpos < t_len)
        outs = []
        for g0 in range(0, SWA_KV_HEADS, SWA_GROUPS_PER_STAGE):
            groups = range(g0, g0 + SWA_GROUPS_PER_STAGE)
            scores = []
            for g in groups:
                kw = kp_ref[pl.ds(base, 3 * w), g * SWA_HEADDIM:(g + 1) * SWA_HEADDIM]
                for c in range(g * rep // 2, (g + 1) * rep // 2):
                    qc = q_ref[0, pl.ds(base, w), c * LANES:(c + 1) * LANES].astype(F32)
                    qc = (_rope_lanes(qc, cos_b, sin_b) * (SWA_HEADDIM ** -0.5)).astype(BF16)
                    for hh in range(2):
                        qh = qc[:, hh * SWA_HEADDIM:(hh + 1) * SWA_HEADDIM]
                        scores.append(lax.dot_general(qh, kw, (((1,), (1,)), ((), ())),
                                                      preferred_element_type=F32))
            probs = []
            for k, s in enumerate(scores):
                s = jnp.where(valid, s, NEG)
                sk = sink_ref[g0 * rep + k]
                m = jnp.maximum(jnp.max(s, axis=-1, keepdims=True), sk)
                p = jnp.exp(s - m)
                inv = 1.0 / (jnp.sum(p, axis=-1, keepdims=True) + jnp.exp(sk - m))
                probs.append((p.astype(BF16), inv))
            for k, (p, inv) in enumerate(probs):
                g = g0 + k // rep
                vw = vp_ref[pl.ds(base, 3 * w), g * SWA_HEADDIM:(g + 1) * SWA_HEADDIM]
                outs.append(jnp.dot(p, vw, preferred_element_type=F32) * inv)
        o_ref[0, pl.ds(base, w), :] = jnp.concatenate(outs, axis=-1).astype(o_ref.dtype)
        return carry

    lax.fori_loop(0, t_len // w, block, 0)


def swa_attention(proj, sink, bsz, t_len):
    half = SWA_HEADDIM // 2
    inv_freq = ROPE_THETA ** (-jnp.arange(half, dtype=F32) / half)
    ang = jnp.arange(t_len, dtype=F32)[:, None] * inv_freq[None, :]
    cos, sin = jnp.cos(ang), jnp.sin(ang)
    cos_t = jnp.tile(jnp.concatenate([cos, cos], axis=-1), (1, 2))
    sin_t = jnp.tile(jnp.concatenate([-sin, sin], axis=-1), (1, 2))
    nq = SWA_HEADS * SWA_HEADDIM
    nkv = SWA_KV_HEADS * SWA_HEADDIM
    return pl.pallas_call(
        _swa_body,
        grid=(bsz,),
        in_specs=[
            pl.BlockSpec((1, t_len, nq), lambda b: (b, 0, 0)),
            pl.BlockSpec((1, t_len, nkv), lambda b: (b, 0, nq // nkv)),
            pl.BlockSpec((1, t_len, nkv), lambda b: (b, 0, nq // nkv + 1)),
            pl.BlockSpec((t_len, LANES), lambda b: (0, 0)),
            pl.BlockSpec((t_len, LANES), lambda b: (0, 0)),
            pl.BlockSpec(memory_space=pltpu.SMEM),
        ],
        out_specs=pl.BlockSpec((1, t_len, nq), lambda b: (b, 0, 0)),
        out_shape=jax.ShapeDtypeStruct((bsz, t_len, nq), BF16),
        scratch_shapes=[pltpu.VMEM((t_len + 2 * SWA_WINDOW, nkv), BF16),
                        pltpu.VMEM((t_len + 2 * SWA_WINDOW, nkv), BF16)],
        compiler_params=_cparams("parallel"),
        name="swa_attention",
    )(proj, proj, proj, cos_t, sin_t, sink.astype(F32))


def _na_bias_table(rpb):
    qc = np.arange(GRID_W)[:, None]
    kc = np.arange(GRID_W)[None, :]
    cstart = np.clip(qc - NA_COLS // 2, 0, GRID_W - NA_COLS)
    valid = (kc >= cstart) & (kc < cstart + NA_COLS)
    cidx = np.clip(kc - qc + NA_COLS - 1, 0, 2 * NA_COLS - 2)
    onehot = (np.arange(2 * NA_COLS - 1)[:, None, None] == cidx[None]).astype(np.float32)
    toep = jnp.einsum('hrc,cqk->hrqk', rpb.astype(F32), onehot, precision=lax.Precision.HIGHEST)
    toep = jnp.where(valid[None, None], toep, NEG)
    return jnp.concatenate([toep[:, :-1], toep[:, 1:]], axis=-1)


def _na_body(q_ref, k_ref, v_ref, bias_ref, o_ref):
    t_len = q_ref.shape[1]
    rows = t_len // GRID_W
    win = NA_ROWS * GRID_W

    def row_group(gi, carry):
        first = lax.broadcasted_iota(jnp.int32, (GRID_W, 2 * NA_HEADDIM), 1) < NA_HEADDIM
        chains = []
        for rr in range(NA_ROW_UNROLL):
            r = gi * NA_ROW_UNROLL + rr
            rs = jnp.clip(r - NA_ROWS // 2, 0, rows - NA_ROWS)
            d0 = rs - r + NA_ROWS - 1
            qbase = pl.multiple_of(r * GRID_W, GRID_W)
            kbase = pl.multiple_of(rs * GRID_W, GRID_W)
            qr = q_ref[0, pl.ds(qbase, GRID_W), :]
            kw = k_ref[0, pl.ds(kbase, win), :]
            for hh in range(2):
                q_h = jnp.where(first if hh == 0 else jnp.logical_not(first), qr, jnp.zeros((), BF16))
                s = lax.dot_general(q_h, kw, (((1,), (1,)), ((), ())), preferred_element_type=F32)
                chains.append((hh, d0, kbase, s))
        probs = []
        for hh, d0, kbase, s in chains:
            bias = jnp.concatenate([bias_ref[hh, d0 + 2 * c] for c in range(NA_ROWS // 2)], axis=-1)
            s = s * (NA_HEADDIM ** -0.5) + bias
            m = jnp.max(s, axis=-1, keepdims=True)
            p = jnp.exp(s - m)
            probs.append((p.astype(BF16), 1.0 / jnp.sum(p, axis=-1, keepdims=True)))
        outs = []
        for (hh, d0, kbase, s), (p, inv) in zip(chains, probs):
            vw = v_ref[0, pl.ds(kbase, win), :]
            outs.append(jnp.dot(p, vw, preferred_element_type=F32) * inv)
        for rr in range(NA_ROW_UNROLL):
            qbase = pl.multiple_of((gi * NA_ROW_UNROLL + rr) * GRID_W, GRID_W)
            o_ref[0, pl.ds(qbase, GRID_W), :] = jnp.where(first, outs[2 * rr], outs[2 * rr + 1]).astype(o_ref.dtype)
        return carry

    lax.fori_loop(0, rows // NA_ROW_UNROLL, row_group, 0)


def na_attention(proj, rpb, bsz, t_len):
    bias = _na_bias_table(rpb)
    npair = NA_HEADS // 2
    return pl.pallas_call(
        _na_body,
        grid=(npair, bsz),
        in_specs=[
            pl.BlockSpec((1, t_len, LANES), lambda hp, b: (b, 0, hp)),
            pl.BlockSpec((1, t_len, LANES), lambda hp, b: (b, 0, npair + hp)),
            pl.BlockSpec((1, t_len, LANES), lambda hp, b: (b, 0, 2 * npair + hp)),
            pl.BlockSpec((2, 2 * NA_ROWS - 2, GRID_W, 2 * GRID_W), lambda hp, b: (hp, 0, 0, 0)),
        ],
        out_specs=pl.BlockSpec((1, t_len, LANES), lambda hp, b: (b, 0, hp)),
        out_shape=jax.ShapeDtypeStruct((bsz, t_len, NA_WIDTH), BF16),
        compiler_params=_cparams("parallel", "parallel"),
        name="na_attention",
    )(proj, proj, proj, bias)


def _conv_silu_body(x_ref, w_ref, b_ref, o_ref, *, n_l2):
    t_len, tc = x_ref.shape[1], x_ref.shape[2]
    rc = CONV_ROW_CHUNK
    halo = F32_SUBLANES
    normalise = pl.program_id(1) < n_l2

    def chunk(ci, carry):
        r0 = pl.multiple_of(ci * rc, rc)
        cur = x_ref[0, pl.ds(r0, rc), :].astype(F32)
        lo = pl.multiple_of(jnp.maximum(r0 - halo, 0), halo)
        hi = pl.multiple_of(jnp.minimum(r0 + rc, t_len - halo), halo)
        before = jnp.where(r0 > 0, x_ref[0, pl.ds(lo, halo), :].astype(F32), 0.0)
        after = jnp.where(r0 + rc < t_len, x_ref[0, pl.ds(hi, halo), :].astype(F32), 0.0)
        xx = jnp.concatenate([before, cur, after], axis=0)
        acc = cur * w_ref[CONV_K // 2:CONV_K // 2 + 1, :] + b_ref[...]
        for k in range(CONV_K):
            off = k - CONV_K // 2
            if off == 0:
                continue
            acc = acc + xx[halo + off:halo + off + rc, :] * w_ref[k:k + 1, :]
        y = acc * jax.nn.sigmoid(acc)
        if n_l2 > 0:
            parts = []
            for c in range(tc // LANES):
                yc = y[:, c * LANES:(c + 1) * LANES]
                ss = jnp.sum(yc * yc, axis=-1, keepdims=True)
                parts.append(yc * jnp.where(normalise, lax.rsqrt(ss + NORM_EPS), 1.0))
            y = jnp.concatenate(parts, axis=-1)
        o_ref[0, pl.ds(r0, rc), :] = y.astype(o_ref.dtype)
        return carry

    lax.fori_loop(0, t_len // rc, chunk, 0)


def conv_silu(proj, w, bias, col0, n_ch, n_l2=0, tc=512):
    bsz, t_len, _ = proj.shape
    assert col0 % tc == 0 and n_ch % tc == 0
    c0 = col0 // tc
    return pl.pallas_call(
        functools.partial(_conv_silu_body, n_l2=n_l2),
        grid=(bsz, n_ch // tc),
        in_specs=[
            pl.BlockSpec((1, t_len, tc), lambda b, j: (b, 0, c0 + j)),
            pl.BlockSpec((CONV_K, tc), lambda b, j: (0, j)),
            pl.BlockSpec((1, tc), lambda b, j: (0, j)),
        ],
        out_specs=pl.BlockSpec((1, t_len, tc), lambda b, j: (b, 0, j)),
        out_shape=jax.ShapeDtypeStruct((bsz, t_len, n_ch), BF16),
        compiler_params=_cparams("parallel", "parallel"),
        name="conv_silu",
    )(proj, w.astype(F32), bias.astype(F32).reshape(1, n_ch))


def _softplus(x):
    return jnp.maximum(x, 0.0) + jnp.log1p(jnp.exp(-jnp.abs(x)))


def _bmm(a, b, precision=None):
    return lax.dot_general(a, b, (((2,), (1,)), ((0,), (0,))), precision=precision,
                           preferred_element_type=F32)


def _split_bf16(a):
    hi = a.astype(BF16)
    return hi, (a - hi.astype(F32)).astype(BF16)


def _split3_bf16(a):
    hi = a.astype(BF16)
    rest = a - hi.astype(F32)
    mid = rest.astype(BF16)
    return hi, mid, (rest - mid.astype(F32)).astype(BF16)


def _bmm_split(a, b):
    return _bmm(a[0], b[0]) + _bmm(a[0], b[1]) + _bmm(a[1], b[0])


def _bmm_tn(a, b):
    return lax.dot_general(a, b, (((1,), (1,)), ((0,), (0,))), preferred_element_type=F32)


def _bmm_nt(a, b):
    return lax.dot_general(a, b, (((2,), (2,)), ((0,), (0,))), preferred_element_type=F32)


def _gdn_body(q_ref, k_ref, v_ref, z_ref, gcol_ref, grow_ref, alog_ref, dtb_ref, nw_ref, o_ref,
              m_s, b_s, q_s, o0_s, el_s, o_s):
    c_len = GDN_CHUNK
    t_len = q_ref.shape[1]
    nc = t_len // c_len
    h = pl.program_id(1)
    ii = lax.broadcasted_iota(jnp.int32, (c_len, c_len), 0)
    jj = lax.broadcasted_iota(jnp.int32, (c_len, c_len), 1)
    eye = (ii == jj).astype(F32)
    cg = GDN_PHASE_A_CHUNKS

    def phase_a(gi, carry):
        c0 = pl.multiple_of(gi * cg, cg)
        rows = pl.ds(pl.multiple_of(gi * (cg * c_len), cg * c_len), cg * c_len)
        k3 = k_ref[0, rows, :].reshape(cg, c_len, GDN_DK)
        v3 = v_ref[0, rows, :].reshape(cg, c_len, GDN_DK).astype(F32)
        kf = k3.astype(F32)
        qs = q_ref[0, rows, :].reshape(cg, c_len, GDN_DK).astype(F32) * (GDN_DK ** -0.5)
        gcol = gcol_ref[0, 0, pl.ds(c0, cg)]
        grow = grow_ref[0, 0, pl.ds(c0, cg)]
        qk_raw = _bmm_nt(qs.astype(BF16), k3)
        kb, gc_c, g_last, decay, strict = [], [], [], [], []
        for d in range(2):
            incl = (ii >= jj) if d == 0 else (ii <= jj)
            strict.append((ii > jj) if d == 0 else (ii < jj))
            tri = incl.astype(F32)
            tri_t = ((ii <= jj) if d == 0 else (ii >= jj)).astype(F32)
            neg_a = -jnp.exp(jnp.full((1, 1, 1), alog_ref[d, h], F32))
            dtb = dtb_ref[d, h]
            g_c = neg_a * _softplus(gcol[:, :, d:d + 1] + dtb)
            g_r = neg_a * _softplus(grow[:, d:d + 1, :] + dtb)
            beta_c = jax.nn.sigmoid(gcol[:, :, 2 + d:3 + d])
            gc_c.append(jnp.sum(tri[None] * g_r, axis=2, keepdims=True))
            gc_r = jnp.sum(tri_t[None] * g_c, axis=1, keepdims=True)
            g_last.append(jnp.sum(g_r, axis=2, keepdims=True))
            decay.append(jnp.where(incl[None], jnp.exp(jnp.where(incl[None], gc_c[d] - gc_r, 0.0)), 0.0))
            kb.append((kf * beta_c, v3 * beta_c))
        kk = [_bmm_nt(kb[d][0].astype(BF16), k3) for d in range(2)]
        low = [jnp.where(strict[d][None], kk[d] * decay[d], 0.0) for d in range(2)]
        inv = [eye[None] - low[d] for d in range(2)]
        pw = [_split_bf16(low[d]) for d in range(2)]
        for _ in range(5):
            pw = [_split_bf16(_bmm_split(pw[d], pw[d])) for d in range(2)]
            inv = [inv[d] + _bmm_split(_split_bf16(inv[d]), pw[d]) for d in range(2)]
        rhs = [jnp.concatenate([kb[d][1], kb[d][0] * jnp.exp(gc_c[d])], axis=-1) for d in range(2)]
        sol = [_bmm_split(_split_bf16(inv[d]), _split_bf16(rhs[d])) for d in range(2)]
        for d in range(2):
            ub = sol[d][:, :, :GDN_DK].astype(BF16)
            w = sol[d][:, :, GDN_DK:].astype(BF16)
            qk = (qk_raw * decay[d]).astype(BF16)
            kg = (kf * jnp.exp(g_last[d] - gc_c[d])).astype(BF16)
            m_s[d, pl.ds(c0, cg)] = _bmm_tn(kg, w).astype(BF16)
            b_s[d, pl.ds(c0, cg)] = _bmm_tn(kg, ub)
            q_s[d, pl.ds(c0, cg)] = (qs * jnp.exp(gc_c[d]) - _bmm(qk, w)).astype(BF16)
            o0_s[d, pl.ds(c0, cg)] = _bmm(qk, ub)
            el_s[d, pl.ds(c0, cg)] = jnp.broadcast_to(jnp.exp(g_last[d]), (cg, 1, GDN_DK))
        return carry

    lax.fori_loop(0, nc // cg, phase_a, 0)

    def step(t, carry):
        cs = (t, nc - 1 - t)
        sb = [carry[d].astype(BF16) for d in range(2)]
        ms = [jnp.dot(m_s[d, cs[d]], sb[d], preferred_element_type=F32) for d in range(2)]
        os_ = [jnp.dot(q_s[d, cs[d]], sb[d], preferred_element_type=F32) for d in range(2)]
        for d in range(2):
            o_s[d, cs[d]] = os_[d] + o0_s[d, cs[d]]
        return tuple(carry[d] * el_s[d, cs[d]] - ms[d] + b_s[d, cs[d]] for d in range(2))

    s0 = jnp.zeros((GDN_DK, GDN_DK), F32)
    lax.fori_loop(0, nc, step, (s0, s0))

    o = (o_s[0] + o_s[1]).reshape(t_len, GDN_DK)
    ms = jnp.mean(o * o, axis=-1, keepdims=True)
    z = z_ref[0]
    y = o * lax.rsqrt(ms + NORM_EPS) * nw_ref[...] * (z * jax.nn.sigmoid(z))
    o_ref[0] = y.astype(o_ref.dtype)


def gdn_scan(qkv, proj, gates, a_log, dt_bias, norm_w, bsz, t_len):
    nc = t_len // GDN_CHUNK
    nh = GDN_HEADS
    g4 = gates[:, :4 * nh].reshape(bsz, nc, GDN_CHUNK, 4, nh)
    gcol = jnp.transpose(g4, (0, 4, 1, 2, 3))
    grow = jnp.transpose(g4, (0, 4, 1, 3, 2))
    dk = GDN_DK
    nq = GDN_QK // dk
    return pl.pallas_call(
        _gdn_body,
        grid=(bsz, nh),
        in_specs=[
            pl.BlockSpec((1, t_len, dk), lambda b, h: (b, 0, h)),
            pl.BlockSpec((1, t_len, dk), lambda b, h: (b, 0, nq + h)),
            pl.BlockSpec((1, t_len, dk), lambda b, h: (b, 0, 2 * nq + h)),
            pl.BlockSpec((1, t_len, dk), lambda b, h: (b, 0, 3 * nq + h)),
            pl.BlockSpec((1, 1, nc, GDN_CHUNK, 4), lambda b, h: (b, h, 0, 0, 0)),
            pl.BlockSpec((1, 1, nc, 4, GDN_CHUNK), lambda b, h: (b, h, 0, 0, 0)),
            pl.BlockSpec(memory_space=pltpu.SMEM),
            pl.BlockSpec(memory_space=pltpu.SMEM),
            pl.BlockSpec((1, dk), lambda b, h: (0, 0)),
        ],
        out_specs=pl.BlockSpec((1, t_len, dk), lambda b, h: (b, 0, h)),
        out_shape=jax.ShapeDtypeStruct((bsz, t_len, GDN_V), BF16),
        scratch_shapes=[
            pltpu.VMEM((2, nc, dk, dk), BF16),
            pltpu.VMEM((2, nc, dk, dk), F32),
            pltpu.VMEM((2, nc, GDN_CHUNK, dk), BF16),
            pltpu.VMEM((2, nc, GDN_CHUNK, dk), F32),
            pltpu.VMEM((2, nc, 1, dk), F32),
            pltpu.VMEM((2, nc, GDN_CHUNK, dk), F32),
        ],
        compiler_params=_cparams("parallel", "parallel"),
        name="gdn_scan",
    )(qkv, qkv, qkv, proj, gcol, grow, a_log.astype(F32), dt_bias.astype(F32),
      norm_w.astype(F32).reshape(1, dk))


def _pad_cols(w, n):
    return jnp.pad(w, ((0, 0), (0, n - w.shape[1])))


def gdn_layer(x, gain, w_in, conv_w, a_log, dt_bias, norm_w, w_out, bsz, t_len):
    n_main = GDN_CONV_CH + GDN_V
    proj = norm_matmul(x, gain, w_in[:, :n_main].astype(BF16), F32)
    gates = norm_matmul(x, gain, _pad_cols(w_in[:, n_main:], LANES).astype(BF16), F32)
    proj3 = proj.reshape(bsz, t_len, n_main)
    qkv = conv_silu(proj3, conv_w, jnp.zeros((GDN_CONV_CH,), F32), 0, GDN_CONV_CH,
                    n_l2=2 * GDN_QK // 512)
    y = gdn_scan(qkv, proj3, gates, a_log, dt_bias, norm_w, bsz, t_len)
    return matmul_residual(x, y.reshape(bsz * t_len, GDN_V), w_out.astype(BF16))


def _ssd_body(xs_ref, b_ref, c_ref, z_ref, dcol_ref, drow_ref, alogc_ref, alogr_ref, dtbc_ref, dtbr_ref, dskip_ref,
              nw_ref, o_ref, xw_s, ea_s, el_s, y_s, st_s):
    assert SSD_CHUNK == SSD_HEADDIM
    c_len = SSD_CHUNK
    t_len = xs_ref.shape[1]
    nc = t_len // c_len
    nr = SSD_HEADS // SSD_GROUPS
    hp = SSD_HEADDIM
    g = pl.program_id(1)
    ii = lax.broadcasted_iota(jnp.int32, (c_len, c_len), 0)
    jj = lax.broadcasted_iota(jnp.int32, (c_len, c_len), 1)
    cg = SSD_PHASE_A_CHUNKS

    def phase_a(gi, carry):
        c0 = pl.multiple_of(gi * cg, cg)
        rows = pl.ds(pl.multiple_of(gi * (cg * c_len), cg * c_len), cg * c_len)
        x3 = xs_ref[0, rows, :].reshape(cg, c_len, nr * hp)
        b3 = b_ref[0, rows, :].reshape(cg, c_len, SSD_STATE)
        c3 = c_ref[0, rows, :].reshape(cg, c_len, SSD_STATE)
        dcol = dcol_ref[0, 0, pl.ds(c0, cg)]
        drow = drow_ref[0, 0, pl.ds(c0, cg)]
        cb = _bmm_nt(c3, b3)
        dt_c = _softplus(dcol + dtbc_ref[0])
        dt_r = _softplus(drow + dtbr_ref[0])
        da_c = dt_c * -jnp.exp(alogc_ref[0])
        da_r = dt_r * -jnp.exp(alogr_ref[0])
        tri_f = jnp.broadcast_to((ii >= jj).astype(BF16)[None], (cg, c_len, c_len))
        tri_b = jnp.broadcast_to((ii <= jj).astype(BF16)[None], (cg, c_len, c_len))
        fwd_c = lax.broadcasted_iota(jnp.int32, (1, 1, 2 * nr), 2) < nr
        fwd_r = lax.broadcasted_iota(jnp.int32, (1, 2 * nr, 1), 1) < nr
        da_c3, da_r3 = _split3_bf16(da_c), _split3_bf16(da_r)
        ac_c = jnp.where(fwd_c, sum(_bmm(tri_f, p) for p in da_c3), sum(_bmm(tri_b, p) for p in da_c3))
        ac_r = jnp.where(fwd_r, sum(_bmm(p, tri_b) for p in da_r3), sum(_bmm(p, tri_f) for p in da_r3))
        ydiag = None
        for d in range(2):
            incl = (ii >= jj) if d == 0 else (ii <= jj)
            sel = (lax.broadcasted_iota(jnp.int32, (2 * nr, nr * hp), 1) // hp + d * nr
                   == lax.broadcasted_iota(jnp.int32, (2 * nr, nr * hp), 0)).astype(BF16)

            def spread(cols, pieces):
                parts = pieces(cols.reshape(cg * c_len, 2 * nr))
                return sum(jnp.dot(p, sel, preferred_element_type=F32) for p in parts).reshape(cg, c_len, nr * hp)

            dt_w = spread(dt_c, _split_bf16)
            ac_w = spread(ac_c, _split3_bf16)
            last = ac_w[:, c_len - 1:c_len, :] if d == 0 else ac_w[:, 0:1, :]
            xr = x3.astype(F32) * dt_w
            xw_s[d, pl.ds(c0, cg)] = (xr * jnp.exp(last - ac_w)).astype(BF16)
            ea_s[d, pl.ds(c0, cg)] = jnp.exp(ac_w)
            el_s[d, pl.ds(c0, cg)] = jnp.exp(last)
            xr = xr.astype(BF16)
            yd = []
            for r in range(nr):
                col = d * nr + r
                diff = ac_w[:, :, r * hp:r * hp + c_len] - ac_r[:, col:col + 1, :]
                seg = jnp.where(incl[None], jnp.exp(jnp.where(incl[None], diff, 0.0)), 0.0)
                yd.append(_bmm((cb * seg).astype(BF16), xr[:, :, r * hp:(r + 1) * hp]))
            yd = jnp.concatenate(yd, axis=-1)
            ydiag = yd if ydiag is None else ydiag + yd
        y_s[pl.ds(c0, cg)] = ydiag
        return carry

    lax.fori_loop(0, nc // cg, phase_a, 0)

    st_s[...] = jnp.zeros_like(st_s)

    def step(t, carry):
        for d in range(2):
            c = t if d == 0 else nc - 1 - t
            base = pl.multiple_of(c * c_len, c_len)
            cc = c_ref[0, pl.ds(base, c_len), :]
            bb = b_ref[0, pl.ds(base, c_len), :]
            st = st_s[d]
            y_s[c] += jnp.dot(cc, st.astype(BF16), preferred_element_type=F32) * ea_s[d, c]
            st_s[d] = st * el_s[d, c] + lax.dot_general(bb, xw_s[d, c], (((0,), (0,)), ((), ())),
                                                        preferred_element_type=F32)
        return carry

    lax.fori_loop(0, nc, step, 0)

    xs = xs_ref[0].astype(F32)
    z = z_ref[0]
    y = (y_s[...].reshape(t_len, nr * hp) + dskip_ref[...] * xs) * (z * jax.nn.sigmoid(z))
    ms = jnp.mean(y * y, axis=-1, keepdims=True)
    o_ref[0] = (y * lax.rsqrt(ms + NORM_EPS) * nw_ref[...]).astype(o_ref.dtype)


def ssd_scan(xbc, proj, dts, a_log, dt_bias, d_skip, norm_w, bsz, t_len):
    nc = t_len // SSD_CHUNK
    ng = SSD_GROUPS
    nr = SSD_HEADS // ng
    gw = nr * SSD_HEADDIM
    d6 = dts[:, :2 * SSD_HEADS].reshape(bsz, nc, SSD_CHUNK, 2, ng, nr)
    dcol = jnp.transpose(d6, (0, 4, 1, 2, 3, 5)).reshape(bsz, ng, nc, SSD_CHUNK, 2 * nr)
    drow = jnp.transpose(d6, (0, 4, 1, 3, 5, 2)).reshape(bsz, ng, nc, 2 * nr, SSD_CHUNK)
    nb0 = SSD_INNER // SSD_STATE
    dtb_g = jnp.transpose(dt_bias.astype(F32).reshape(2, ng, nr), (1, 0, 2)).reshape(ng, 2 * nr)
    alog_g = jnp.transpose(a_log.astype(F32).reshape(2, ng, nr), (1, 0, 2)).reshape(ng, 2 * nr)
    return pl.pallas_call(
        _ssd_body,
        grid=(bsz, ng),
        in_specs=[
            pl.BlockSpec((1, t_len, gw), lambda b, g: (b, 0, g)),
            pl.BlockSpec((1, t_len, SSD_STATE), lambda b, g: (b, 0, nb0 + g)),
            pl.BlockSpec((1, t_len, SSD_STATE), lambda b, g: (b, 0, nb0 + ng + g)),
            pl.BlockSpec((1, t_len, gw), lambda b, g: (b, 0, g)),
            pl.BlockSpec((1, 1, nc, SSD_CHUNK, 2 * nr), lambda b, g: (b, g, 0, 0, 0)),
            pl.BlockSpec((1, 1, nc, 2 * nr, SSD_CHUNK), lambda b, g: (b, g, 0, 0, 0)),
            pl.BlockSpec((1, 1, 2 * nr), lambda b, g: (g, 0, 0)),
            pl.BlockSpec((1, 2 * nr, 1), lambda b, g: (g, 0, 0)),
            pl.BlockSpec((1, 1, 2 * nr), lambda b, g: (g, 0, 0)),
            pl.BlockSpec((1, 2 * nr, 1), lambda b, g: (g, 0, 0)),
            pl.BlockSpec((1, gw), lambda b, g: (0, g)),
            pl.BlockSpec((1, gw), lambda b, g: (0, g)),
        ],
        out_specs=pl.BlockSpec((1, t_len, gw), lambda b, g: (b, 0, g)),
        out_shape=jax.ShapeDtypeStruct((bsz, t_len, SSD_INNER), BF16),
        scratch_shapes=[
            pltpu.VMEM((2, nc, SSD_CHUNK, gw), BF16),
            pltpu.VMEM((2, nc, SSD_CHUNK, gw), F32),
            pltpu.VMEM((2, nc, 1, gw), F32),
            pltpu.VMEM((nc, SSD_CHUNK, gw), F32),
            pltpu.VMEM((2, SSD_STATE, gw), F32),
        ],
        compiler_params=_cparams("parallel", "parallel"),
        name="ssd_scan",
    )(xbc, xbc, xbc, proj, dcol, drow, alog_g.reshape(ng, 1, 2 * nr), alog_g.reshape(ng, 2 * nr, 1),
      dtb_g.reshape(ng, 1, 2 * nr), dtb_g.reshape(ng, 2 * nr, 1),
      jnp.repeat(d_skip.astype(F32), SSD_HEADDIM).reshape(1, SSD_INNER),
      norm_w.astype(F32).reshape(1, SSD_INNER))


def ssd_layer(x, gain, w_in, conv_w, conv_b, a_log, dt_bias, d_skip, norm_w, w_out, bsz, t_len):
    n_main = SSD_INNER + SSD_CONV_CH
    proj = norm_matmul(x, gain, w_in[:, :n_main].astype(BF16), F32)
    dts = norm_matmul(x, gain, _pad_cols(w_in[:, n_main:], LANES).astype(BF16), F32)
    proj3 = proj.reshape(bsz, t_len, n_main)
    xbc = conv_silu(proj3, conv_w, conv_b, SSD_INNER, SSD_CONV_CH)
    y = ssd_scan(xbc, proj3, dts, a_log, dt_bias, d_skip, norm_w, bsz, t_len)
    return matmul_residual(x, y.reshape(bsz * t_len, SSD_INNER), w_out.astype(BF16))


def swa_layer(x, gain, w_in, sink, w_out, bsz, t_len):
    proj = norm_matmul(x, gain, w_in.astype(BF16), BF16)
    o = swa_attention(proj.reshape(bsz, t_len, -1), sink, bsz, t_len)
    return matmul_residual(x, o.reshape(bsz * t_len, -1), w_out.astype(BF16))


def na_layer(x, gain, w_in, rpb, w_out, bsz, t_len):
    proj = norm_matmul(x, gain, w_in.astype(BF16), BF16)
    o = na_attention(proj.reshape(bsz, t_len, -1), rpb, bsz, t_len)
    return matmul_residual(x, o.reshape(bsz * t_len, -1), w_out.astype(BF16))


def kernel(x, norm_mix, norm_ffn, norm_final, gdn_w_in, gdn_conv, gdn_a_log, gdn_dt_bias, gdn_norm, gdn_w_out, ssd_w_in, ssd_conv, ssd_conv_b, ssd_a_log, ssd_dt_bias, ssd_d, ssd_norm, ssd_w_out, swa_w_in, swa_sink, swa_w_out, na_w_in, na_rpb, na_w_out, peer_w_q, peer_keys, peer_u, peer_v):
    bsz, t_len, d = x.shape
    depth = norm_mix.shape[0]
    xf = x.reshape(bsz * t_len, d)
    for i in range(depth):
        mixer, j = i % 4, i // 4
        if mixer == 0:
            xf = gdn_layer(xf, norm_mix[i], gdn_w_in[j], gdn_conv[j], gdn_a_log[j], gdn_dt_bias[j],
                           gdn_norm[j], gdn_w_out[j], bsz, t_len)
        elif mixer == 1:
            xf = ssd_layer(xf, norm_mix[i], ssd_w_in[j], ssd_conv[j], ssd_conv_b[j], ssd_a_log[j],
                           ssd_dt_bias[j], ssd_d[j], ssd_norm[j], ssd_w_out[j], bsz, t_len)
        elif mixer == 2:
            xf = swa_layer(xf, norm_mix[i], swa_w_in[j], swa_sink[j], swa_w_out[j], bsz, t_len)
        else:
            xf = na_layer(xf, norm_mix[i], na_w_in[j], na_rpb[j], na_w_out[j], bsz, t_len)
        xf = peer_layer(xf, norm_ffn[i], peer_w_q[i], peer_keys[i], peer_u[i], peer_v[i])
    return final_norm(xf, norm_final).reshape(bsz, t_len, d)
```
